```python
import math
import jax
import jax.numpy as jnp
from jax import lax
import numpy as np

D_MODEL = 1024
BATCH = 4
SEQ = 4096
DEPTH = 4
DEC_BATCH = 128
DEC_SEQ = 4
PAST_LEN = 2048
PAGE_SIZE = 128

BR_W = D_MODEL // 2
H_A = 4
DK_A = BR_W // H_A
DV_A = BR_W // H_A
H_B = 4
DK_B = BR_W // H_B
DV_B = BR_W // H_B
CONV_W = 4
GDN_CONV_CH = H_B * (2 * DK_B + DV_B)
H_C = 4
DH_C = BR_W // H_C // 2
DV_C = 2 * DH_C
N_BRANCH = 3
N_BUCKETS = 32
MAX_DISTANCE = 128
N_MEM = 256
H_X = 4
DH_X = BR_W // H_X
D_FF = -(-8 * D_MODEL // (3 * 256)) * 256
CHUNK = 64
Q_BLOCK = 128
LN_EPS = 1e-5
NORM_EPS = 1e-6
NEG_INF = -1e30
ALPHA = (2 * DEPTH) ** 0.25
BETA = (8 * DEPTH) ** -0.25
IN_SIZES = (H_A * DK_A, H_A * DK_A, H_A * DV_A, H_A * DV_A,
            H_B * DK_B, H_B * DK_B, H_B * DV_B, H_B * DV_B, H_B, H_B,
            H_C * 2 * DH_C, H_C * 2 * DH_C, H_C * DV_C, N_BRANCH * D_MODEL)
IN_W = sum(IN_SIZES)

kernel_name = 'hybrid_hgrn2_gdn_diffattn_step'


def layer_norm(x, g, b):
    xf = x.astype(jnp.float32)
    mu = jnp.mean(xf, axis=-1, keepdims=True)
    var = jnp.mean(jnp.square(xf - mu), axis=-1, keepdims=True)
    return ((xf - mu) * lax.rsqrt(var + LN_EPS) * g + b).astype(x.dtype)


def rms_norm(x, g):
    xf = x.astype(jnp.float32)
    return xf * lax.rsqrt(jnp.mean(jnp.square(xf), axis=-1, keepdims=True) + NORM_EPS) * g


def l2_norm(x):
    return x * lax.rsqrt(jnp.sum(jnp.square(x), axis=-1, keepdims=True) + NORM_EPS)


def heads(t, h):
    return t.reshape(t.shape[0], t.shape[1], h, -1)


def _pad_time(t, pad):
    if pad == 0:
        return t
    return jnp.pad(t, [(0, 0), (0, pad)] + [(0, 0)] * (t.ndim - 2))


def _to_chunks(t, c):
    b, l = t.shape[:2]
    t = t.reshape((b, l // c, c) + t.shape[2:])
    return jnp.moveaxis(t, (1, 3), (0, 2))


def _from_chunks(t):
    t = jnp.moveaxis(t, (0, 2), (1, 3))
    return t.reshape(t.shape[0], t.shape[1] * t.shape[2], t.shape[3], t.shape[4])


def hgrn2_recurrence(q, k, logf, v, s0):
    l = q.shape[1]
    c = min(CHUNK, l)
    pad = (-l) % c
    qc, kc, fc, vc = (_to_chunks(_pad_time(t.astype(jnp.float32), pad), c) for t in (q, k, logf, v))
    incl = jnp.tril(jnp.ones((c, c), dtype=bool))[:, :, None]

    def step(s, inp):
        q_i, k_i, f_i, v_i = inp
        bcum = jnp.cumsum(f_i, axis=2)
        diff = bcum[:, :, :, None, :] - bcum[:, :, None, :, :]
        decay = jnp.where(incl, jnp.exp(jnp.where(incl, diff, 0.0)), 0.0)
        scores = jnp.einsum('bhtd,bhsd,bhtsd->bhts', q_i, k_i, decay)
        o = (jnp.einsum('bhtd,bhde->bhte', q_i * jnp.exp(bcum), s)
             + jnp.einsum('bhts,bhse->bhte', scores, v_i))
        b_last = bcum[:, :, -1]
        s = (s * jnp.exp(b_last)[..., None]
             + jnp.einsum('bhsd,bhse->bhde', k_i * jnp.exp(b_last[:, :, None, :] - bcum), v_i))
        return s, o

    s, o = lax.scan(step, s0.astype(jnp.float32), (qc, kc, fc, vc))
    return _from_chunks(o)[:, :l], s


def gated_delta_rule(q, k, v, g, beta, s0):
    l = q.shape[1]
    dv = v.shape[-1]
    c = min(CHUNK, l)
    pad = (-l) % c
    qc, kc, vc, gc, bc = (_to_chunks(_pad_time(t.astype(jnp.float32), pad), c) for t in (q, k, v, g, beta))
    incl = jnp.tril(jnp.ones((c, c), dtype=bool))
    strict = jnp.tril(jnp.ones((c, c), dtype=bool), -1)
    eye = jnp.eye(c, dtype=jnp.float32)

    def step(s, inp):
        q_i, k_i, v_i, g_i, b_i = inp
        gcum = jnp.cumsum(g_i, axis=-1)
        gdiff = gcum[..., :, None] - gcum[..., None, :]
        decay = jnp.where(incl, jnp.exp(jnp.where(incl, gdiff, 0.0)), 0.0)
        kb = k_i * b_i[..., None]
        lmat = jnp.where(strict, jnp.einsum('bhid,bhjd->bhij', kb, k_i) * decay, 0.0)
        rhs = jnp.concatenate([v_i * b_i[..., None], kb * jnp.exp(gcum)[..., None]], axis=-1)
        sol = lax.linalg.triangular_solve(eye + lmat, rhs, left_side=True, lower=True)
        u, w = sol[..., :dv], sol[..., dv:]
        v_new = u - jnp.einsum('bhck,bhkv->bhcv', w, s)
        attn = jnp.einsum('bhik,bhjk->bhij', q_i, k_i) * decay
        o = (jnp.einsum('bhck,bhkv->bhcv', q_i * jnp.exp(gcum)[..., None], s)
             + jnp.einsum('bhij,bhjv->bhiv', attn, v_new))
        g_last = gcum[..., -1]
        s = (s * jnp.exp(g_last)[..., None, None]
             + jnp.einsum('bhck,bhcv->bhkv', k_i * jnp.exp(g_last[..., None] - gcum)[..., None], v_new))
        return s, o

    s, o = lax.scan(step, s0.astype(jnp.float32), (qc, kc, vc, gc, bc))
    return _from_chunks(o)[:, :l], s


def causal_conv(x, buf, w):
    l = x.shape[1]
    xp = jnp.concatenate([buf.astype(x.dtype), x], axis=1)
    out = sum(xp[:, j:j + l] * w[j] for j in range(CONV_W))
    return out, xp[:, l:]


def t5_bucket(rel):
    n = jnp.maximum(-rel, 0)
    exact = N_BUCKETS // 2
    log_part = jnp.log(jnp.maximum(n, 1).astype(jnp.float32) / exact) / math.log(MAX_DISTANCE / exact)
    large = jnp.minimum(exact + (log_part * (N_BUCKETS - exact)).astype(jnp.int32), N_BUCKETS - 1)
    return jnp.where(n < exact, n, large)


def diff_core(q, k, v, q_pos, k_pos, lam, rel_bias):
    b, tk = k.shape[:2]
    k = k.reshape(b, tk, H_C, 2, DH_C)
    s = jnp.einsum('bqhmd,bkhmd->bhmqk', q, k).astype(jnp.float32) * DH_C ** -0.5
    rel = k_pos[None, :] - q_pos[:, None]
    bias = jnp.moveaxis(rel_bias[t5_bucket(rel)], -1, 0).astype(jnp.float32)
    s = jnp.where(rel <= 0, s + bias[None, :, None], NEG_INF)
    p = jax.nn.softmax(s, axis=-1)
    a = p[:, :, 0] - lam * p[:, :, 1]
    return jnp.einsum('bhqk,bkhd->bqhd', a, v.astype(jnp.float32))


def diff_attention_prompt(q, k, v, lam, rel_bias):
    b, l = q.shape[:2]
    qb = Q_BLOCK if l % Q_BLOCK == 0 else l
    nb = l // qb
    q_blocks = jnp.moveaxis(q.reshape(b, nb, qb, H_C, 2, DH_C), 1, 0)
    k_pos = jnp.arange(l, dtype=jnp.int32)

    def one_block(args):
        q_blk, i = args
        q_pos = i * qb + jnp.arange(qb, dtype=jnp.int32)
        return diff_core(q_blk, k, v, q_pos, k_pos, lam, rel_bias)

    out = lax.map(one_block, (q_blocks, jnp.arange(nb, dtype=jnp.int32)))
    return jnp.moveaxis(out, 0, 1).reshape(b, l, H_C, DV_C)


def cross_attention(x, mk, mv, wq, wo):
    b, l = x.shape[:2]
    q = (x @ wq).reshape(b, l, H_X, DH_X)
    s = jnp.einsum('blhd,bmhd->bhlm', q, mk).astype(jnp.float32) * DH_X ** -0.5
    p = jax.nn.softmax(s, axis=-1)
    o = jnp.einsum('bhlm,bmhd->blhd', p, mv.astype(jnp.float32)).reshape(b, l, H_X * DH_X)
    return o.astype(x.dtype) @ wo


def swiglu(x, w1, w2):
    gate, up = jnp.split(x @ w1, 2, axis=-1)
    return (jax.nn.silu(gate) * up) @ w2


def setup_inputs(seed: int = 0) -> dict:
    key = jax.random.key(seed)
    ks = jax.random.split(key, 40)
    f32 = jnp.float32
    n_pages = PAST_LEN // PAGE_SIZE
    n_used = DEC_BATCH * n_pages
    n_pool = n_used + (n_used + 3) // 4

    def nrm(k, shape, scale):
        return jax.random.normal(k, shape, f32) * scale

    dt = jnp.exp(jax.random.uniform(ks[14], (DEPTH, H_B), f32, math.log(1e-3), math.log(1e-1)))
    return {
        'x_prompt': nrm(ks[0], (BATCH, SEQ, D_MODEL), 1.0),
        'x_sample': nrm(ks[1], (DEC_BATCH, DEC_SEQ, D_MODEL), 1.0),
        'mem_prompt': nrm(ks[2], (BATCH, N_MEM, D_MODEL), 1.0),
        'cache_attn_k': nrm(ks[3], (DEPTH, n_pool, PAGE_SIZE, H_C, 2 * DH_C), 1.0),
        'cache_attn_v': nrm(ks[4], (DEPTH, n_pool, PAGE_SIZE, H_C, DV_C), 1.0),
        'cache_mem_k': nrm(ks[5], (DEPTH, DEC_BATCH, N_MEM, H_X, DH_X), 1.0),
        'cache_mem_v': nrm(ks[6], (DEPTH, DEC_BATCH, N_MEM, H_X, DH_X), 1.0),
        'state_hgrn': nrm(ks[7], (DEPTH, DEC_BATCH, H_A, DK_A, DV_A), 0.3),
        'state_gdn': nrm(ks[8], (DEPTH, DEC_BATCH, H_B, DK_B, DV_B), 0.1),
        'state_gdn_conv': nrm(ks[9], (DEPTH, DEC_BATCH, CONV_W - 1, GDN_CONV_CH), 1.0),
        'page_table': jax.random.permutation(ks[10], n_pool)[:n_used].reshape(DEC_BATCH, n_pages).astype(jnp.int32),
        'w_in': nrm(ks[11], (DEPTH, D_MODEL, IN_W), D_MODEL ** -0.5),
        'w_branch': nrm(ks[12], (DEPTH, N_BRANCH, BR_W, D_MODEL), BR_W ** -0.5),
        'w_out': nrm(ks[13], (DEPTH, D_MODEL, D_MODEL), BETA * D_MODEL ** -0.5),
        'lower_bounds': nrm(ks[15], (DEPTH, H_A * DK_A), 0.5),
        'hgrn_norm': 1.0 + nrm(ks[16], (DEPTH, DV_A), 0.02),
        'gdn_a_log': jnp.log(jax.random.uniform(ks[17], (DEPTH, H_B), f32, 1.0, 16.0)),
        'gdn_dt_bias': dt + jnp.log(-jnp.expm1(-dt)),
        'gdn_conv_w': nrm(ks[18], (DEPTH, CONV_W, GDN_CONV_CH), CONV_W ** -0.5),
        'gdn_norm': 1.0 + nrm(ks[19], (DEPTH, DV_B), 0.02),
        'diff_lambda': nrm(ks[20], (DEPTH, 4, DH_C), 0.1),
        'diff_norm': 1.0 + nrm(ks[21], (DEPTH, DV_C), 0.02),
        'rel_bias': nrm(ks[22], (N_BUCKETS, H_C), 0.3),
        'w_xq': nrm(ks[23], (DEPTH, D_MODEL, H_X * DH_X), D_MODEL ** -0.5),
        'w_mem_kv': nrm(ks[24], (DEPTH, D_MODEL, 2 * H_X * DH_X), D_MODEL ** -0.5),
        'w_xo': nrm(ks[25], (DEPTH, H_X * DH_X, D_MODEL), BETA * (H_X * DH_X) ** -0.5),
        'ln_g': 1.0 + nrm(ks[26], (DEPTH, 3, D_MODEL), 0.02),
        'ln_b': nrm(ks[27], (DEPTH, 3, D_MODEL), 0.02),
        'w_ffn_in': nrm(ks[28], (DEPTH, D_MODEL, 2 * D_FF), D_MODEL ** -0.5),
        'w_ffn_out': nrm(ks[29], (DEPTH, D_FF, D_MODEL), BETA * D_FF ** -0.5),
    }


def reference(x_prompt, x_sample, mem_prompt, cache_attn_k, cache_attn_v, cache_mem_k, cache_mem_v,
              state_hgrn, state_gdn, state_gdn_conv, page_table, w_in, w_branch, w_out, lower_bounds,
              hgrn_norm, gdn_a_log, gdn_dt_bias, gdn_conv_w, gdn_norm, diff_lambda, diff_norm, rel_bias,
              w_xq, w_mem_kv, w_xo, ln_g, ln_b, w_ffn_in, w_ffn_out):
    f32 = jnp.float32
    split_idx = tuple(int(s) for s in np.cumsum(IN_SIZES)[:-1])
    lb_cum = jnp.cumsum(jax.nn.softmax(lower_bounds.astype(f32), axis=0), axis=0)
    lower_bound = lb_cum - lb_cum[0]
    bp = x_prompt.shape[0]
    bs, ls = x_sample.shape[:2]
    n_past = page_table.shape[1] * PAGE_SIZE
    q_pos_s = n_past + jnp.arange(ls, dtype=jnp.int32)
    k_pos_s = jnp.arange(n_past + ls, dtype=jnp.int32)

    def token_mixer(x, l, s_hgrn0, s_gdn0, conv0, attend):
        bn, sl = x.shape[:2]
        lam_init = 0.8 - 0.6 * math.exp(-0.3 * l)
        (a_q, a_f, a_i, a_g, b_q, b_k, b_v, b_g, b_a, b_b,
         c_q, c_k, c_v, gates) = jnp.split(x @ w_in[l], split_idx, axis=-1)
        z = a_f.astype(f32)
        lb = lower_bound[l]
        logf = jnp.logaddexp(jnp.log(lb), jnp.log1p(-lb) + jax.nn.log_sigmoid(z))
        k_a = (1.0 - lb) * jax.nn.sigmoid(-z)
        o_a, s_hgrn = hgrn2_recurrence(heads(a_q, H_A), heads(k_a, H_A), heads(logf, H_A), heads(a_i, H_A), s_hgrn0)
        o_a = rms_norm(o_a, hgrn_norm[l]) * jax.nn.silu(heads(a_g, H_A).astype(f32))
        qkv, conv_new = causal_conv(jnp.concatenate([b_q, b_k, b_v], axis=-1), conv0, gdn_conv_w[l])
        q_b, k_b, v_b = jnp.split(jax.nn.silu(qkv.astype(f32)), (H_B * DK_B, 2 * H_B * DK_B), axis=-1)
        q_b = l2_norm(heads(q_b, H_B)) * DK_B ** -0.5
        k_b = l2_norm(heads(k_b, H_B))
        g_b = -jnp.exp(gdn_a_log[l].astype(f32)) * jax.nn.softplus(b_a.astype(f32) + gdn_dt_bias[l])
        beta_b = jax.nn.sigmoid(b_b.astype(f32))
        o_b, s_gdn = gated_delta_rule(q_b, k_b, heads(v_b, H_B), g_b, beta_b, s_gdn0)
        o_b = rms_norm(o_b, gdn_norm[l]) * jax.nn.silu(heads(b_g, H_B).astype(f32))
        q_c = c_q.reshape(bn, sl, H_C, 2, DH_C)
        k_c = c_k.reshape(bn, sl, H_C, 2 * DH_C)
        v_c = c_v.reshape(bn, sl, H_C, DV_C)
        lq1, lk1, lq2, lk2 = diff_lambda[l].astype(f32)
        lam = jnp.exp(jnp.sum(lq1 * lk1)) - jnp.exp(jnp.sum(lq2 * lk2)) + lam_init
        o_c = rms_norm(attend(q_c, k_c, v_c, lam, l), diff_norm[l]) * (1.0 - lam_init)
        branches = jnp.stack([o_a.reshape(bn, sl, BR_W), o_b.reshape(bn, sl, BR_W),
                              o_c.reshape(bn, sl, BR_W)], axis=2).astype(x.dtype)
        proj = jnp.einsum('blni,nid->blnd', branches, w_branch[l])
        gate = jax.nn.sigmoid(gates.reshape(bn, sl, N_BRANCH, D_MODEL))
        y = jnp.sum(gate * proj, axis=2) @ w_out[l]
        return y, k_c, v_c, s_hgrn, s_gdn, conv_new

    def attend_prompt(q, k, v, lam, l):
        return diff_attention_prompt(q, k, v, lam, rel_bias)

    def attend_sample(q, k, v, lam, l):
        k_past = cache_attn_k[l][page_table].reshape(bs, n_past, H_C, 2 * DH_C).astype(k.dtype)
        v_past = cache_attn_v[l][page_table].reshape(bs, n_past, H_C, DV_C).astype(v.dtype)
        return diff_core(q, jnp.concatenate([k_past, k], axis=1), jnp.concatenate([v_past, v], axis=1),
                         q_pos_s, k_pos_s, lam, rel_bias)

    def memory_and_ffn(x, l, mk, mv):
        x = layer_norm(ALPHA * x + cross_attention(x, mk, mv, w_xq[l], w_xo[l]), ln_g[l, 1], ln_b[l, 1])
        return layer_norm(ALPHA * x + swiglu(x, w_ffn_in[l], w_ffn_out[l]), ln_g[l, 2], ln_b[l, 2])

    xp, xs = x_prompt, x_sample
    pk, pv, pmk, pmv, ph, pg, pc = [], [], [], [], [], [], []
    sk, sv, sh, sg, sc = [], [], [], [], []
    for l in range(DEPTH):
        y, k_new, v_new, h_new, g_new, c_new = token_mixer(
            xp, l, jnp.zeros((bp, H_A, DK_A, DV_A), f32), jnp.zeros((bp, H_B, DK_B, DV_B), f32),
            jnp.zeros((bp, CONV_W - 1, GDN_CONV_CH), xp.dtype), attend_prompt)
        xp = layer_norm(ALPHA * xp + y, ln_g[l, 0], ln_b[l, 0])
        mkv = (mem_prompt @ w_mem_kv[l]).reshape(bp, mem_prompt.shape[1], 2, H_X, DH_X)
        mk, mv = mkv[:, :, 0], mkv[:, :, 1]
        xp = memory_and_ffn(xp, l, mk, mv)
        pk.append(k_new); pv.append(v_new); pmk.append(mk); pmv.append(mv)
        ph.append(h_new); pg.append(g_new); pc.append(c_new)
        y, k_new, v_new, h_new, g_new, c_new = token_mixer(
            xs, l, state_hgrn[l], state_gdn[l], state_gdn_conv[l], attend_sample)
        xs = layer_norm(ALPHA * xs + y, ln_g[l, 0], ln_b[l, 0])
        xs = memory_and_ffn(xs, l, cache_mem_k[l], cache_mem_v[l])
        sk.append(k_new); sv.append(v_new); sh.append(h_new); sg.append(g_new); sc.append(c_new)

    new_attn_k_prompt = jnp.stack(pk)
    new_attn_v_prompt = jnp.stack(pv)
    new_mem_k_prompt = jnp.stack(pmk)
    new_mem_v_prompt = jnp.stack(pmv)
    new_hgrn_prompt = jnp.stack(ph)
    new_gdn_prompt = jnp.stack(pg)
    new_gdn_conv_prompt = jnp.stack(pc)
    new_attn_k_sample = jnp.stack(sk)
    new_attn_v_sample = jnp.stack(sv)
    new_hgrn_sample = jnp.stack(sh)
    new_gdn_sample = jnp.stack(sg)
    new_gdn_conv_sample = jnp.stack(sc)
    return (xp, xs, new_attn_k_prompt, new_attn_v_prompt, new_mem_k_prompt, new_mem_v_prompt,
            new_hgrn_prompt, new_gdn_prompt, new_gdn_conv_prompt, new_attn_k_sample, new_attn_v_sample,
            new_hgrn_sample, new_gdn_sample, new_gdn_conv_sample)
```

```python
import functools
import math

import jax
import jax.numpy as jnp
from jax import lax
from jax.experimental import pallas as pl
from jax.experimental.pallas import tpu as pltpu

F32 = jnp.float32
BF16 = jnp.bfloat16

D_MODEL = 1024
DEPTH = 4
PAGE_SIZE = 128
BR_W = D_MODEL // 2
N_HEADS = 4
HEAD_W = BR_W // N_HEADS
DH_C = HEAD_W // 2
CONV_W = 4
CONV_CH = 3 * BR_W
N_BRANCH = 3
N_BUCKETS = 32
MAX_DISTANCE = 128
D_FF = -(-8 * D_MODEL // (3 * 256)) * 256
LN_EPS = 1e-5
NORM_EPS = 1e-6
NEG_INF = -1e30
ALPHA = (2 * DEPTH) ** 0.25

LANES = 128
SUBLANES = 8
VMEM_LIMIT_BYTES = 56 * 1024 * 1024

COL_GATES = 0
COL_A = N_BRANCH * D_MODEL // LANES
COL_B = COL_A + 4 * N_HEADS
COL_C = COL_B + 4 * N_HEADS
COL_AB = COL_C + 3 * N_HEADS
PROJ_W = (COL_AB + 1) * LANES

HGRN_SUB = 16
GDN_CHUNK = 64
SAMPLE_PAD = 8
SEQ_TILE = 512
ATT_BLOCK = 256


def _cparams(n_axes):
    return pltpu.CompilerParams(dimension_semantics=("arbitrary",) * n_axes,
                                vmem_limit_bytes=VMEM_LIMIT_BYTES)


def _dot(a, b):
    return jnp.dot(a.astype(BF16), b.astype(BF16), preferred_element_type=F32)


def _dot_nt(a, b):
    return lax.dot_general(a.astype(BF16), b.astype(BF16), (((1,), (1,)), ((), ())),
                           preferred_element_type=F32)


def _dot_tn(a, b):
    return lax.dot_general(a.astype(BF16), b.astype(BF16), (((0,), (0,)), ((), ())),
                           preferred_element_type=F32)


def _dot_exact_lhs(m01, x):
    hi = x.astype(BF16)
    r1 = x - hi.astype(F32)
    mid = r1.astype(BF16)
    lo = (r1 - mid.astype(F32)).astype(BF16)
    dot = functools.partial(jnp.dot, preferred_element_type=F32)
    return dot(m01, hi) + dot(m01, mid) + dot(m01, lo)


def _sigmoid(x):
    return 1.0 / (1.0 + jnp.exp(-x))


def _silu(x):
    return x * _sigmoid(x)


def _softplus(x):
    return jnp.maximum(x, 0.0) + jnp.log1p(jnp.exp(-jnp.abs(x)))


def _layer_norm(h, g, b):
    mu = jnp.mean(h, axis=-1, keepdims=True)
    c = h - mu
    var = jnp.mean(c * c, axis=-1, keepdims=True)
    return c * lax.rsqrt(var + LN_EPS) * g + b


def _rms_norm(x, g):
    return x * lax.rsqrt(jnp.mean(x * x, axis=-1, keepdims=True) + NORM_EPS) * g


def _bcast_rows(x, rows, reps):
    return jnp.concatenate([jnp.broadcast_to(x[r:r + 1, :], (reps, x.shape[1])) for r in rows], axis=0)


def _mm_body(x_ref, w_ref, o_ref):
    o_ref[...] = _dot(x_ref[...], w_ref[...])


def _matmul(x, w, tm, tn):
    t, k = x.shape
    n = w.shape[1]
    tm = min(tm, t)
    return pl.pallas_call(
        _mm_body,
        grid=(n // tn, t // tm),
        in_specs=[pl.BlockSpec((tm, k), lambda j, i: (i, 0)),
                  pl.BlockSpec((k, tn), lambda j, i: (0, j))],
        out_specs=pl.BlockSpec((tm, tn), lambda j, i: (i, j)),
        out_shape=jax.ShapeDtypeStruct((t, n), F32),
        compiler_params=_cparams(2),
        name="proj_matmul",
    )(x, w)


def _hgrn_rows(q, z, v, lbp, sub, valid):
    r = q.shape[0]
    n_sub = r // sub
    log_lb, log1m_lb, om_lb = lbp[0:1, :], lbp[1:2, :], lbp[2:3, :]
    log_sig = jnp.minimum(z, 0.0) - jnp.log1p(jnp.exp(-jnp.abs(z)))
    t2 = log1m_lb + log_sig
    logf = jnp.maximum(log_lb, t2) + jnp.log1p(jnp.exp(-jnp.abs(log_lb - t2)))
    k = om_lb * (1.0 / (1.0 + jnp.exp(z)))
    if valid is not None:
        logf = jnp.where(valid, logf, 0.0)
        k = jnp.where(valid, k, 0.0)
    ri = lax.broadcasted_iota(jnp.int32, (r, r), 0)
    ci = lax.broadcasted_iota(jnp.int32, (r, r), 1)
    tri = ((ri // sub == ci // sub) & (ci <= ri)).astype(BF16)
    bc = _dot_exact_lhs(tri, logf)
    last_rows = [i * sub + sub - 1 for i in range(n_sub)]
    b_last = _bcast_rows(bc, last_rows, sub)
    q_hat = q * jnp.exp(bc)
    k_hat = k * jnp.exp(b_last - bc)
    pos = lax.broadcasted_iota(jnp.int32, (r, 1), 0) % sub
    o = jnp.zeros_like(q)
    for s in range(sub):
        rows = [i * sub + s for i in range(n_sub)]
        ks = _bcast_rows(k, rows, sub)
        bs = _bcast_rows(bc, rows, sub)
        vs = _bcast_rows(v, rows, sub)
        a = q * ks * jnp.exp(jnp.minimum(bc - bs, 0.0))
        a = jnp.where(pos >= s, a, 0.0)
        o = o + jnp.sum(a, axis=-1, keepdims=True) * vs
    return q_hat, k_hat, o, bc


def _hgrn_finish(o, g, norm_w):
    return _rms_norm(o, norm_w) * _silu(g)


def _hgrn_prompt_body(q_ref, f_ref, i_ref, g_ref, lbp_ref, nw_ref, o_ref, s_out_ref, st_ref, *, n_tiles):
    t = pl.program_id(2)

    @pl.when(t == 0)
    def _():
        st_ref[...] = jnp.zeros_like(st_ref)

    rows = 4 * HGRN_SUB
    lbp = lbp_ref[0]
    nw = nw_ref[...]

    def chunk(c, carry):
        sl = pl.ds(pl.multiple_of(c * rows, rows), rows)
        q, z, v, g = q_ref[sl, :], f_ref[sl, :], i_ref[sl, :], g_ref[sl, :]
        q_hat, k_hat, o, bc = _hgrn_rows(q, z, v, lbp, HGRN_SUB, None)
        outs = []
        st = st_ref[...]
        for i in range(rows // HGRN_SUB):
            lo, hi = i * HGRN_SUB, (i + 1) * HGRN_SUB
            outs.append(o[lo:hi] + _dot_nt(q_hat[lo:hi], st))
            st = st * jnp.exp(bc[hi - 1:hi, :]) + _dot_tn(v[lo:hi], k_hat[lo:hi])
        st_ref[...] = st
        o_ref[sl, :] = _hgrn_finish(jnp.concatenate(outs, axis=0), g, nw)
        return carry

    lax.fori_loop(0, SEQ_TILE // rows, chunk, 0)

    @pl.when(t == n_tiles - 1)
    def _():
        s_out_ref[0, 0] = st_ref[...].T


def _hgrn_prompt(proj, lbp, norm_w, batch, seq):
    n_tiles = seq // SEQ_TILE

    def col(base):
        return pl.BlockSpec((SEQ_TILE, HEAD_W), lambda b, h, t: (b * n_tiles + t, base + h))

    return pl.pallas_call(
        functools.partial(_hgrn_prompt_body, n_tiles=n_tiles),
        grid=(batch, N_HEADS, n_tiles),
        in_specs=[col(COL_A), col(COL_A + N_HEADS), col(COL_A + 2 * N_HEADS), col(COL_A + 3 * N_HEADS),
                  pl.BlockSpec((1, SUBLANES, HEAD_W), lambda b, h, t: (h, 0, 0)),
                  pl.BlockSpec((1, HEAD_W), lambda b, h, t: (0, 0))],
        out_specs=[pl.BlockSpec((SEQ_TILE, HEAD_W), lambda b, h, t: (b * n_tiles + t, h)),
                   pl.BlockSpec((1, 1, HEAD_W, HEAD_W), lambda b, h, t: (b, h, 0, 0))],
        out_shape=[jax.ShapeDtypeStruct((batch * seq, BR_W), F32),
                   jax.ShapeDtypeStruct((batch, N_HEADS, HEAD_W, HEAD_W), F32)],
        scratch_shapes=[pltpu.VMEM((HEAD_W, HEAD_W), F32)],
        compiler_params=_cparams(3),
        name="hgrn_prompt",
    )(proj, proj, proj, proj, lbp, norm_w)


def _hgrn_sample_body(q_ref, f_ref, i_ref, g_ref, lbp_ref, nw_ref, s0_ref, o_ref, s_out_ref, *, n_seq, n_real):
    lbp = lbp_ref[0]
    nw = nw_ref[...]
    rows = n_seq * SAMPLE_PAD
    pos = lax.broadcasted_iota(jnp.int32, (rows, 1), 0) % SAMPLE_PAD
    valid = pos >= SAMPLE_PAD - n_real
    q, z, v, g = q_ref[...], f_ref[...], i_ref[...], g_ref[...]
    q_hat, k_hat, o, bc = _hgrn_rows(q, z, v, lbp, SAMPLE_PAD, valid)
    outs = []
    for i in range(n_seq):
        lo, hi = i * SAMPLE_PAD, (i + 1) * SAMPLE_PAD
        s0 = s0_ref[i, 0]
        outs.append(o[lo:hi] + _dot(q_hat[lo:hi], s0))
        decay = jnp.exp(bc[hi - 1:hi, :])
        s_t = s0.T * decay + _dot_tn(v[lo:hi], k_hat[lo:hi])
        s_out_ref[i, 0] = s_t.T
    o_ref[...] = _hgrn_finish(jnp.concatenate(outs, axis=0), g, nw)


def _hgrn_sample(proj, lbp, norm_w, s0, n_real, seq_per_step=8):
    n_batch = s0.shape[0]
    rows = seq_per_step * SAMPLE_PAD

    def col(base):
        return pl.BlockSpec((rows, HEAD_W), lambda g, h: (g, base + h))

    state_spec = pl.BlockSpec((seq_per_step, 1, HEAD_W, HEAD_W), lambda g, h: (g, h, 0, 0))
    return pl.pallas_call(
        functools.partial(_hgrn_sample_body, n_seq=seq_per_step, n_real=n_real),
        grid=(n_batch // seq_per_step, N_HEADS),
        in_specs=[col(COL_A), col(COL_A + N_HEADS), col(COL_A + 2 * N_HEADS), col(COL_A + 3 * N_HEADS),
                  pl.BlockSpec((1, SUBLANES, HEAD_W), lambda g, h: (h, 0, 0)),
                  pl.BlockSpec((1, HEAD_W), lambda g, h: (0, 0)),
                  state_spec],
        out_specs=[pl.BlockSpec((rows, HEAD_W), lambda g, h: (g, h)), state_spec],
        out_shape=[jax.ShapeDtypeStruct((n_batch * SAMPLE_PAD, BR_W), F32),
                   jax.ShapeDtypeStruct(s0.shape, F32)],
        compiler_params=_cparams(2),
        name="hgrn_sample",
    )(proj, proj, proj, proj, lbp, norm_w, s0)


def _conv_body(prev_ref, cur_ref, w_ref, o_ref, buf_ref, *, tile, seq_len):
    i = pl.program_id(0)
    c = pl.program_id(1)
    buf_ref[0:SUBLANES, :] = prev_ref[...]
    buf_ref[SUBLANES:, :] = cur_ref[...]
    pos = (i * tile + lax.broadcasted_iota(jnp.int32, (tile, 1), 0)) % seq_len
    w = w_ref[...]
    acc = cur_ref[...] * w[CONV_W - 1:CONV_W, :]
    for back in range(1, CONV_W):
        shifted = buf_ref[pl.ds(SUBLANES - back, tile), :]
        acc = acc + jnp.where(pos >= back, shifted, 0.0) * w[CONV_W - 1 - back:CONV_W - back, :]
    act = _silu(acc)
    inv = lax.rsqrt(jnp.sum(act * act, axis=-1, keepdims=True) + NORM_EPS)
    scale = jnp.where(c < N_HEADS, inv * HEAD_W ** -0.5, jnp.where(c < 2 * N_HEADS, inv, 1.0))
    o_ref[...] = act * scale


def _gdn_conv(conv_in, conv_w, tile, seq_len):
    t = conv_in.shape[0]
    per = tile // SUBLANES
    return pl.pallas_call(
        functools.partial(_conv_body, tile=tile, seq_len=seq_len),
        grid=(t // tile, CONV_CH // HEAD_W),
        in_specs=[pl.BlockSpec((SUBLANES, HEAD_W), lambda i, c: (jnp.maximum(i * per - 1, 0), c)),
                  pl.BlockSpec((tile, HEAD_W), lambda i, c: (i, c)),
                  pl.BlockSpec((CONV_W, HEAD_W), lambda i, c: (0, c))],
        out_specs=pl.BlockSpec((tile, HEAD_W), lambda i, c: (i, c)),
        out_shape=jax.ShapeDtypeStruct((t, CONV_CH), F32),
        scratch_shapes=[pltpu.VMEM((tile + SUBLANES, HEAD_W), F32)],
        compiler_params=_cparams(2),
        name="gdn_conv",
    )(conv_in, conv_in, conv_w)


def _forward_substitute(lmat, rhs):
    c = lmat.shape[0]
    nb = c // SUBLANES
    xb = [rhs[b * SUBLANES:(b + 1) * SUBLANES, :] for b in range(nb)]
    lb = [lmat[b * SUBLANES:(b + 1) * SUBLANES, :] for b in range(nb)]
    for j in range(c - 1):
        xj = xb[j // SUBLANES][j % SUBLANES:j % SUBLANES + 1, :]
        for b in range((j + 1) // SUBLANES, nb):
            xb[b] = xb[b] - lb[b][:, j:j + 1] * xj
    return jnp.concatenate(xb, axis=0) if nb > 1 else xb[0]


def _gdn_chunk(q, k, v, a_col, b_col, s, a_scale, dt_bias, valid):
    c = q.shape[0]
    g = -a_scale * _softplus(a_col + dt_bias)
    beta = _sigmoid(b_col)
    if valid is not None:
        g = jnp.where(valid, g, 0.0)
        beta = jnp.where(valid, beta, 0.0)
    ri = lax.broadcasted_iota(jnp.int32, (c, c), 0)
    ci = lax.broadcasted_iota(jnp.int32, (c, c), 1)
    incl = ci <= ri
    strict = ci < ri
    mr = lax.broadcasted_iota(jnp.int32, (c, LANES), 0)
    mc = lax.broadcasted_iota(jnp.int32, (c, LANES), 1)
    sel = ((mc < mr) & (mc < c)) | (mc == c)
    m = _dot_exact_lhs(incl.astype(BF16), jnp.where(sel, g, 0.0))
    gcum = m[:, c:c + 1]
    decay = jnp.where(incl, jnp.exp(jnp.where(incl, m[:, :c], 0.0)), 0.0)
    kb = k * beta
    lmat = jnp.where(strict, _dot_nt(kb, k) * decay, 0.0)
    e_g = jnp.exp(gcum)
    sol = _forward_substitute(lmat, jnp.concatenate([v * beta, kb * e_g], axis=1))
    u, w = sol[:, :HEAD_W], sol[:, HEAD_W:]
    v_new = u - _dot(w, s)
    attn = jnp.where(incl, _dot_nt(q, k) * decay, 0.0)
    o = _dot(q * e_g, s) + _dot(attn, v_new)
    g_last = gcum[c - 1:c, :]
    s_new = s * jnp.exp(g_last) + _dot_tn(k * jnp.exp(g_last - gcum), v_new)
    return o, s_new


def _gdn_prompt_body(hp_ref, q_ref, k_ref, v_ref, g_ref, ab_ref, nw_ref, o_ref, s_out_ref, s_ref, *, n_tiles):
    h = pl.program_id(1)
    t = pl.program_id(2)

    @pl.when(t == 0)
    def _():
        s_ref[...] = jnp.zeros_like(s_ref)

    a_scale = hp_ref[0, h]
    dt_bias = hp_ref[1, h]
    nw = nw_ref[...]
    lane = lax.broadcasted_iota(jnp.int32, (GDN_CHUNK, LANES), 1)

    def chunk(c, carry):
        sl = pl.ds(pl.multiple_of(c * GDN_CHUNK, GDN_CHUNK), GDN_CHUNK)
        ab = ab_ref[sl, :]
        a_col = jnp.sum(jnp.where(lane == h, ab, 0.0), axis=-1, keepdims=True)
        b_col = jnp.sum(jnp.where(lane == h + N_HEADS, ab, 0.0), axis=-1, keepdims=True)
        o, s_new = _gdn_chunk(q_ref[sl, :], k_ref[sl, :], v_ref[sl, :], a_col, b_col, s_ref[...],
                              a_scale, dt_bias, None)
        s_ref[...] = s_new
        o_ref[sl, :] = _rms_norm(o, nw) * _silu(g_ref[sl, :])
        return carry

    lax.fori_loop(0, SEQ_TILE // GDN_CHUNK, chunk, 0)

    @pl.when(t == n_tiles - 1)
    def _():
        s_out_ref[0, 0] = s_ref[...]


def _gdn_prompt(qkv, proj, head_params, norm_w, batch, seq):
    n_tiles = seq // SEQ_TILE

    def col(base):
        return pl.BlockSpec((SEQ_TILE, HEAD_W), lambda b, h, t: (b * n_tiles + t, base + h))

    return pl.pallas_call(
        functools.partial(_gdn_prompt_body, n_tiles=n_tiles),
        grid=(batch, N_HEADS, n_tiles),
        in_specs=[pl.BlockSpec(memory_space=pltpu.SMEM),
                  col(0), col(N_HEADS), col(2 * N_HEADS), col(COL_B + 3 * N_HEADS),
                  pl.BlockSpec((SEQ_TILE, LANES), lambda b, h, t: (b * n_tiles + t, COL_AB)),
                  pl.BlockSpec((1, HEAD_W), lambda b, h, t: (0, 0))],
        out_specs=[pl.BlockSpec((SEQ_TILE, HEAD_W), lambda b, h, t: (b * n_tiles + t, h)),
                   pl.BlockSpec((1, 1, HEAD_W, HEAD_W), lambda b, h, t: (b, h, 0, 0))],
        out_shape=[jax.ShapeDtypeStruct((batch * seq, BR_W), F32),
                   jax.ShapeDtypeStruct((batch, N_HEADS, HEAD_W, HEAD_W), F32)],
        scratch_shapes=[pltpu.VMEM((HEAD_W, HEAD_W), F32)],
        compiler_params=_cparams(3),
        name="gdn_prompt",
    )(head_params, qkv, qkv, qkv, proj, proj, norm_w)


def _gdn_sample_body(hp_ref, q_ref, k_ref, v_ref, g_ref, ab_ref, nw_ref, s0_ref, o_ref, s_out_ref,
                     *, n_seq, n_real):
    h = pl.program_id(1)
    a_scale = hp_ref[0, h]
    dt_bias = hp_ref[1, h]
    nw = nw_ref[...]
    lane = lax.broadcasted_iota(jnp.int32, (SAMPLE_PAD, LANES), 1)
    valid = lax.broadcasted_iota(jnp.int32, (SAMPLE_PAD, 1), 0) >= SAMPLE_PAD - n_real

    def one(i, carry):
        sl = pl.ds(pl.multiple_of(i * SAMPLE_PAD, SAMPLE_PAD), SAMPLE_PAD)
        ab = ab_ref[sl, :]
        a_col = jnp.sum(jnp.where(lane == h, ab, 0.0), axis=-1, keepdims=True)
        b_col = jnp.sum(jnp.where(lane == h + N_HEADS, ab, 0.0), axis=-1, keepdims=True)
        o, s_new = _gdn_chunk(q_ref[sl, :], k_ref[sl, :], v_ref[sl, :], a_col, b_col, s0_ref[i, 0],
                              a_scale, dt_bias, valid)
        s_out_ref[i, 0] = s_new
        o_ref[sl, :] = _rms_norm(o, nw) * _silu(g_ref[sl, :])
        return carry

    lax.fori_loop(0, n_seq, one, 0)


def _gdn_sample(qkv, proj, head_params, norm_w, s0, n_real, seq_per_step=8):
    n_batch = s0.shape[0]
    rows = seq_per_step * SAMPLE_PAD

    def col(base):
        return pl.BlockSpec((rows, HEAD_W), lambda g, h: (g, base + h))

    state_spec = pl.BlockSpec((seq_per_step, 1, HEAD_W, HEAD_W), lambda g, h: (g, h, 0, 0))
    return pl.pallas_call(
        functools.partial(_gdn_sample_body, n_seq=seq_per_step, n_real=n_real),
        grid=(n_batch // seq_per_step, N_HEADS),
        in_specs=[pl.BlockSpec(memory_space=pltpu.SMEM),
                  col(0), col(N_HEADS), col(2 * N_HEADS), col(COL_B + 3 * N_HEADS),
                  pl.BlockSpec((rows, LANES), lambda g, h: (g, COL_AB)),
                  pl.BlockSpec((1, HEAD_W), lambda g, h: (0, 0)),
                  state_spec],
        out_specs=[pl.BlockSpec((rows, HEAD_W), lambda g, h: (g, h)), state_spec],
        out_shape=[jax.ShapeDtypeStruct((n_batch * SAMPLE_PAD, BR_W), F32),
                   jax.ShapeDtypeStruct(s0.shape, F32)],
        compiler_params=_cparams(2),
        name="gdn_sample",
    )(head_params, qkv, qkv, qkv, proj, proj, norm_w, s0)


def t5_bucket(rel):
    n = jnp.maximum(-rel, 0)
    exact = N_BUCKETS // 2
    log_part = jnp.log(jnp.maximum(n, 1).astype(F32) / exact) / math.log(MAX_DISTANCE / exact)
    large = jnp.minimum(exact + (log_part * (N_BUCKETS - exact)).astype(jnp.int32), N_BUCKETS - 1)
    return jnp.where(n < exact, n, large)


def _stack_maps(q):
    lane = lax.broadcasted_iota(jnp.int32, q.shape, 1)
    return jnp.concatenate([jnp.where(lane < DH_C, q, 0.0), jnp.where(lane >= DH_C, q, 0.0)], axis=0)


def _diff_prompt_body(sc_ref, q_ref, k_ref, v_ref, d0_ref, d1_ref, nw_ref, o_ref,
                      k16_ref, v16_ref, m_ref, l_ref, acc_ref):
    h = pl.program_id(1)
    qi = pl.program_id(2)
    blk = ATT_BLOCK

    @pl.when(qi == 0)
    def _():
        k16_ref[...] = k_ref[...].astype(BF16)
        v16_ref[...] = v_ref[...].astype(BF16)

    lam = sc_ref[0, 0]
    out_scale = sc_ref[0, 1]
    far_bias = sc_ref[1, h]
    qs = _stack_maps(q_ref[...] * DH_C ** -0.5).astype(BF16)
    m_ref[...] = jnp.full_like(m_ref, -jnp.inf)
    l_ref[...] = jnp.zeros_like(l_ref)
    acc_ref[...] = jnp.zeros_like(acc_ref)

    def block(kj, bias):
        sl = pl.ds(pl.multiple_of(kj * blk, blk), blk)
        s = lax.dot_general(qs, k16_ref[sl, :], (((1,), (1,)), ((), ())), preferred_element_type=F32) + bias
        m_old = m_ref[...]
        m_new = jnp.maximum(m_old, jnp.max(s, axis=-1, keepdims=True))
        alpha = jnp.exp(m_old - m_new)
        p = jnp.exp(s - m_new)
        l_ref[...] = alpha * l_ref[...] + jnp.sum(p, axis=-1, keepdims=True)
        acc_ref[...] = alpha * acc_ref[...] + jnp.dot(p.astype(BF16), v16_ref[sl, :], preferred_element_type=F32)
        m_ref[...] = m_new

    def far(kj, carry):
        block(kj, far_bias)
        return carry

    lax.fori_loop(0, jnp.maximum(qi - 1, 0), far, 0)

    @pl.when(qi >= 1)
    def _():
        d1 = d1_ref[0]
        block(qi - 1, jnp.concatenate([d1, d1], axis=0))

    d0 = d0_ref[0]
    block(qi, jnp.concatenate([d0, d0], axis=0))
    o = acc_ref[...] / l_ref[...]
    o = o[:blk] - lam * o[blk:]
    o_ref[...] = _rms_norm(o, nw_ref[...]) * out_scale


def _diff_prompt(proj, scalars, d0, d1, norm_w, batch, seq):
    nq = seq // ATT_BLOCK

    def seq_col(base):
        return pl.BlockSpec((seq, HEAD_W), lambda b, h, q: (b, base + h))

    tile = pl.BlockSpec((1, ATT_BLOCK, ATT_BLOCK), lambda b, h, q: (h, 0, 0))
    return pl.pallas_call(
        _diff_prompt_body,
        grid=(batch, N_HEADS, nq),
        in_specs=[pl.BlockSpec(memory_space=pltpu.SMEM),
                  pl.BlockSpec((ATT_BLOCK, HEAD_W), lambda b, h, q: (b * nq + q, COL_C + h)),
                  seq_col(COL_C + N_HEADS), seq_col(COL_C + 2 * N_HEADS),
                  tile, tile,
                  pl.BlockSpec((1, HEAD_W), lambda b, h, q: (0, 0))],
        out_specs=pl.BlockSpec((ATT_BLOCK, HEAD_W), lambda b, h, q: (b * nq + q, h)),
        out_shape=jax.ShapeDtypeStruct((batch * seq, BR_W), F32),
        scratch_shapes=[pltpu.VMEM((seq, HEAD_W), BF16), pltpu.VMEM((seq, HEAD_W), BF16),
                        pltpu.VMEM((2 * ATT_BLOCK, 1), F32), pltpu.VMEM((2 * ATT_BLOCK, 1), F32),
                        pltpu.VMEM((2 * ATT_BLOCK, HEAD_W), F32)],
        compiler_params=_cparams(3),
        name="diff_prompt",
    )(scalars, proj, proj, proj, d0, d1, norm_w)


def _diff_sample_body(pt_ref, sc_ref, q_ref, kn_ref, vn_ref, kp_ref, vp_ref, bias_ref, nbias_ref, nw_ref,
                      o_ref, m_ref, l_ref, acc_ref, *, n_pages, n_real):
    j = pl.program_id(1)
    rows = 2 * n_real
    pad = SAMPLE_PAD - n_real

    @pl.when(j == 0)
    def _():
        m_ref[...] = jnp.full_like(m_ref, -jnp.inf)
        l_ref[...] = jnp.zeros_like(l_ref)
        acc_ref[...] = jnp.zeros_like(acc_ref)

    def stacked_q(h):
        q = q_ref[pad:, h * HEAD_W:(h + 1) * HEAD_W] * DH_C ** -0.5
        return _stack_maps(q)

    kp = kp_ref[0, 0]
    vp = vp_ref[0, 0]
    for h in range(N_HEADS):
        hs = slice(h * HEAD_W, (h + 1) * HEAD_W)
        s = _dot_nt(stacked_q(h), kp[:, hs]) + bias_ref[h, j]
        m_old = m_ref[h]
        m_new = jnp.maximum(m_old, jnp.max(s, axis=-1, keepdims=True))
        alpha = jnp.exp(m_old - m_new)
        p = jnp.exp(s - m_new)
        l_ref[h] = alpha * l_ref[h] + jnp.sum(p, axis=-1, keepdims=True)
        acc_ref[h] = alpha * acc_ref[h] + _dot(p, vp[:, hs])
        m_ref[h] = m_new

    @pl.when(j == n_pages - 1)
    def _():
        lam = sc_ref[0, 0]
        out_scale = sc_ref[0, 1]
        outs = []
        for h in range(N_HEADS):
            hs = slice(h * HEAD_W, (h + 1) * HEAD_W)
            qs = stacked_q(h)
            kn = kn_ref[pad:, hs]
            vn = vn_ref[pad:, hs]
            s_cols = [jnp.sum(qs * kn[t:t + 1, :], axis=-1, keepdims=True) + nbias_ref[h, t]
                      for t in range(n_real)]
            m_old = m_ref[h]
            m_new = m_old
            for sc in s_cols:
                m_new = jnp.maximum(m_new, sc)
            alpha = jnp.exp(m_old - m_new)
            l_new = alpha * l_ref[h]
            acc = alpha * acc_ref[h]
            for t, sc in enumerate(s_cols):
                p = jnp.exp(sc - m_new)
                l_new = l_new + p
                acc = acc + p * vn[t:t + 1, :]
            o = acc / l_new
            o = o[:n_real] - lam * o[n_real:]
            outs.append(_rms_norm(o, nw_ref[...]) * out_scale)
        o_ref[0:pad, :] = jnp.zeros((pad, BR_W), F32)
        o_ref[pad:, :] = jnp.concatenate(outs, axis=1)


def _diff_sample(page_table, scalars, proj, cache_k, cache_v, bias, nbias, norm_w, n_real):
    n_batch, n_pages = page_table.shape
    rows = 2 * n_real

    def tok(base):
        return pl.BlockSpec((SAMPLE_PAD, BR_W), lambda b, j, pt: (b, base // N_HEADS))

    page = pl.BlockSpec((1, 1, PAGE_SIZE, BR_W), lambda b, j, pt: (0, pt[b, j], 0, 0))
    grid_spec = pltpu.PrefetchScalarGridSpec(
        num_scalar_prefetch=1,
        grid=(n_batch, n_pages),
        in_specs=[pl.BlockSpec(memory_space=pltpu.SMEM),
                  tok(COL_C), tok(COL_C + N_HEADS), tok(COL_C + 2 * N_HEADS),
                  page, page,
                  pl.BlockSpec(bias.shape, lambda b, j, pt: (0, 0, 0, 0)),
                  pl.BlockSpec(nbias.shape, lambda b, j, pt: (0, 0, 0, 0)),
                  pl.BlockSpec((1, HEAD_W), lambda b, j, pt: (0, 0))],
        out_specs=pl.BlockSpec((SAMPLE_PAD, BR_W), lambda b, j, pt: (b, 0)),
        scratch_shapes=[pltpu.VMEM((N_HEADS, rows, 1), F32), pltpu.VMEM((N_HEADS, rows, 1), F32),
                        pltpu.VMEM((N_HEADS, rows, HEAD_W), F32)],
    )
    return pl.pallas_call(
        functools.partial(_diff_sample_body, n_pages=n_pages, n_real=n_real),
        grid_spec=grid_spec,
        out_shape=jax.ShapeDtypeStruct((n_batch * SAMPLE_PAD, BR_W), F32),
        compiler_params=_cparams(2),
        name="diff_sample",
    )(page_table, scalars, proj, proj, proj, cache_k[None], cache_v[None], bias, nbias, norm_w)


def _merge_body(oa_ref, ob_ref, oc_ref, g0_ref, g1_ref, g2_ref, x_ref, wb_ref, wo_ref, ln_ref, o_ref):
    m = (_sigmoid(g0_ref[...]) * _dot(oa_ref[...], wb_ref[0])
         + _sigmoid(g1_ref[...]) * _dot(ob_ref[...], wb_ref[1])
         + _sigmoid(g2_ref[...]) * _dot(oc_ref[...], wb_ref[2]))
    y = _dot(m, wo_ref[...])
    o_ref[...] = _layer_norm(ALPHA * x_ref[...] + y, ln_ref[0:1, :], ln_ref[1:2, :])


def _merge(oa, ob, oc, proj, x, w_branch, w_out, ln, tm):
    t = x.shape[0]
    tm = min(tm, t)
    br = pl.BlockSpec((tm, BR_W), lambda i: (i, 0))

    def gate(n):
        return pl.BlockSpec((tm, D_MODEL), lambda i: (i, n))

    return pl.pallas_call(
        _merge_body,
        grid=(t // tm,),
        in_specs=[br, br, br, gate(0), gate(1), gate(2),
                  pl.BlockSpec((tm, D_MODEL), lambda i: (i, 0)),
                  pl.BlockSpec(w_branch.shape, lambda i: (0, 0, 0)),
                  pl.BlockSpec(w_out.shape, lambda i: (0, 0)),
                  pl.BlockSpec((SUBLANES, D_MODEL), lambda i: (0, 0))],
        out_specs=pl.BlockSpec((tm, D_MODEL), lambda i: (i, 0)),
        out_shape=jax.ShapeDtypeStruct((t, D_MODEL), F32),
        compiler_params=_cparams(1),
        name="merge",
    )(oa, ob, oc, proj, proj, proj, x, w_branch, w_out, ln)


def _attend_memory(q, mk, mv):
    outs = []
    for h in range(N_HEADS):
        hs = slice(h * HEAD_W, (h + 1) * HEAD_W)
        s = _dot_nt(q[:, hs], mk[:, hs]) * HEAD_W ** -0.5
        p = jnp.exp(s - jnp.max(s, axis=-1, keepdims=True))
        p = p / jnp.sum(p, axis=-1, keepdims=True)
        outs.append(_dot(p, mv[:, hs]))
    return jnp.concatenate(outs, axis=1)


def _xattn_prompt_body(x_ref, mk_ref, mv_ref, wq_ref, wo_ref, ln_ref, o_ref):
    x = x_ref[...]
    o = _attend_memory(_dot(x, wq_ref[...]), mk_ref[0], mv_ref[0])
    o_ref[...] = _layer_norm(ALPHA * x + _dot(o, wo_ref[...]), ln_ref[0:1, :], ln_ref[1:2, :])


def _xattn_prompt(x, mkv, w_xq, w_xo, ln, batch, seq, tm):
    per = seq // tm
    n_mem = mkv.shape[1]
    return pl.pallas_call(
        _xattn_prompt_body,
        grid=(batch, per),
        in_specs=[pl.BlockSpec((tm, D_MODEL), lambda b, i: (b * per + i, 0)),
                  pl.BlockSpec((1, n_mem, BR_W), lambda b, i: (b, 0, 0)),
                  pl.BlockSpec((1, n_mem, BR_W), lambda b, i: (b, 0, 1)),
                  pl.BlockSpec(w_xq.shape, lambda b, i: (0, 0)),
                  pl.BlockSpec(w_xo.shape, lambda b, i: (0, 0)),
                  pl.BlockSpec((SUBLANES, D_MODEL), lambda b, i: (0, 0))],
        out_specs=pl.BlockSpec((tm, D_MODEL), lambda b, i: (b * per + i, 0)),
        out_shape=jax.ShapeDtypeStruct(x.shape, F32),
        compiler_params=_cparams(2),
        name="xattn_prompt",
    )(x, mkv, mkv, w_xq, w_xo, ln)


def _xattn_sample_body(x_ref, mk_ref, mv_ref, wq_ref, wo_ref, ln_ref, o_ref, att_ref, *, n_seq):
    x = x_ref[...]
    q = _dot(x, wq_ref[...])
    for i in range(n_seq):
        rs = slice(i * SAMPLE_PAD, (i + 1) * SAMPLE_PAD)
        att_ref[rs, :] = _attend_memory(q[rs], mk_ref[i], mv_ref[i])
    o_ref[...] = _layer_norm(ALPHA * x + _dot(att_ref[...], wo_ref[...]), ln_ref[0:1, :], ln_ref[1:2, :])


def _xattn_sample(x, mem_k, mem_v, w_xq, w_xo, ln, seq_per_step=8):
    n_batch, n_mem, _ = mem_k.shape
    rows = seq_per_step * SAMPLE_PAD
    mem = pl.BlockSpec((seq_per_step, n_mem, BR_W), lambda g: (g, 0, 0))
    return pl.pallas_call(
        functools.partial(_xattn_sample_body, n_seq=seq_per_step),
        grid=(n_batch // seq_per_step,),
        in_specs=[pl.BlockSpec((rows, D_MODEL), lambda g: (g, 0)), mem, mem,
                  pl.BlockSpec(w_xq.shape, lambda g: (0, 0)),
                  pl.BlockSpec(w_xo.shape, lambda g: (0, 0)),
                  pl.BlockSpec((SUBLANES, D_MODEL), lambda g: (0, 0))],
        out_specs=pl.BlockSpec((rows, D_MODEL), lambda g: (g, 0)),
        out_shape=jax.ShapeDtypeStruct(x.shape, F32),
        scratch_shapes=[pltpu.VMEM((rows, BR_W), F32)],
        compiler_params=_cparams(1),
        name="xattn_sample",
    )(x, mem_k, mem_v, w_xq, w_xo, ln)


def _ffn_body(x_ref, wg_ref, wu_ref, w2_ref, ln_ref, o_ref, acc_ref, *, n_f):
    f = pl.program_id(1)
    x = x_ref[...]
    xb = x.astype(BF16)
    gate = jnp.dot(xb, wg_ref[...], preferred_element_type=F32)
    up = jnp.dot(xb, wu_ref[...], preferred_element_type=F32)
    part = _dot(_silu(gate) * up, w2_ref[...])

    @pl.when(f == 0)
    def _():
        acc_ref[...] = part

    @pl.when(f > 0)
    def _():
        acc_ref[...] += part

    @pl.when(f == n_f - 1)
    def _():
        o_ref[...] = _layer_norm(ALPHA * x + acc_ref[...], ln_ref[0:1, :], ln_ref[1:2, :])


def _ffn(x, w1, w2, ln, tm, tf):
    t = x.shape[0]
    tm = min(tm, t)
    n_f = D_FF // tf
    return pl.pallas_call(
        functools.partial(_ffn_body, n_f=n_f),
        grid=(t // tm, n_f),
        in_specs=[pl.BlockSpec((tm, D_MODEL), lambda i, f: (i, 0)),
                  pl.BlockSpec((D_MODEL, tf), lambda i, f: (0, f)),
                  pl.BlockSpec((D_MODEL, tf), lambda i, f: (0, n_f + f)),
                  pl.BlockSpec((tf, D_MODEL), lambda i, f: (f, 0)),
                  pl.BlockSpec((SUBLANES, D_MODEL), lambda i, f: (0, 0))],
        out_specs=pl.BlockSpec((tm, D_MODEL), lambda i, f: (i, 0)),
        out_shape=jax.ShapeDtypeStruct(x.shape, F32),
        scratch_shapes=[pltpu.VMEM((tm, D_MODEL), F32)],
        compiler_params=_cparams(2),
        name="ffn",
    )(x, w1, w1, w2, ln)


def _permute_w_in(w):
    a_w = 4 * BR_W
    b_w = 4 * BR_W
    ab0 = a_w + b_w
    c0 = ab0 + 2 * N_HEADS
    g0 = c0 + 3 * BR_W
    pad = jnp.zeros((w.shape[0], LANES - 2 * N_HEADS), w.dtype)
    return jnp.concatenate([w[:, g0:], w[:, :ab0], w[:, c0:g0], w[:, ab0:c0], pad], axis=1).astype(BF16)


def _ln_rows(g, b):
    return jnp.concatenate([g[None], b[None], jnp.zeros((SUBLANES - 2, g.shape[0]), F32)], axis=0)


def _bias_by_distance(rel_bias, n_max):
    n = jnp.arange(n_max, dtype=jnp.int32)
    return rel_bias[t5_bucket(-n)].astype(F32).T


def kernel(x_prompt, x_sample, mem_prompt, cache_attn_k, cache_attn_v, cache_mem_k, cache_mem_v, state_hgrn,
           state_gdn, state_gdn_conv, page_table, w_in, w_branch, w_out, lower_bounds, hgrn_norm, gdn_a_log,
           gdn_dt_bias, gdn_conv_w, gdn_norm, diff_lambda, diff_norm, rel_bias, w_xq, w_mem_kv, w_xo, ln_g, ln_b,
           w_ffn_in, w_ffn_out):
    bp, seq, _ = x_prompt.shape
    bs, ls, _ = x_sample.shape
    n_mem = mem_prompt.shape[1]
    n_pages = page_table.shape[1]
    n_past = n_pages * PAGE_SIZE
    n_pool = cache_attn_k.shape[1]
    pad = SAMPLE_PAD - ls

    lb_cum = jnp.cumsum(jax.nn.softmax(lower_bounds.astype(F32), axis=0), axis=0)
    lb = (lb_cum - lb_cum[0]).reshape(DEPTH, N_HEADS, 1, HEAD_W)
    lbp = jnp.concatenate([jnp.log(lb), jnp.log1p(-lb), 1.0 - lb,
                           jnp.zeros((DEPTH, N_HEADS, SUBLANES - 3, HEAD_W), F32)], axis=2)

    dist = _bias_by_distance(rel_bias, max(2 * ATT_BLOCK + 2, n_past + ls))
    ii = jnp.arange(ATT_BLOCK)[:, None] - jnp.arange(ATT_BLOCK)[None, :]
    d0 = jnp.where(ii >= 0, dist[:, jnp.maximum(ii, 0)], NEG_INF)
    d1 = dist[:, ii + ATT_BLOCK]
    far_bias = dist[:, ATT_BLOCK + 1]
    q_pos = n_past + jnp.arange(ls)
    past = dist[:, q_pos[:, None] - jnp.arange(n_past)[None, :]]
    past = jnp.concatenate([past, past], axis=1).reshape(N_HEADS, 2 * ls, n_pages, PAGE_SIZE)
    bias_s = past.transpose(0, 2, 1, 3)
    tt = jnp.arange(ls)[:, None] - jnp.arange(ls)[None, :]
    new = jnp.where(tt >= 0, dist[:, jnp.maximum(tt, 0)], NEG_INF)
    new = jnp.concatenate([new, new], axis=1).transpose(0, 2, 1)[..., None]

    xp = x_prompt.reshape(bp * seq, D_MODEL)
    xs = jnp.concatenate([jnp.zeros((bs, pad, D_MODEL), F32), x_sample], axis=1).reshape(bs * SAMPLE_PAD, D_MODEL)
    mem = mem_prompt.reshape(bp * n_mem, D_MODEL)

    outs = {k: [] for k in ("pk", "pv", "pmk", "pmv", "ph", "pg", "pc", "sk", "sv", "sh", "sg", "sc")}
    kc0, vc0 = (COL_C + N_HEADS) * LANES, (COL_C + 2 * N_HEADS) * LANES
    b0 = COL_B * LANES

    for l in range(DEPTH):
        w_in_l = _permute_w_in(w_in[l])
        wb_l = w_branch[l].astype(BF16)
        wo_l = w_out[l].astype(BF16)
        wq_l = w_xq[l].astype(BF16)
        wxo_l = w_xo[l].astype(BF16)
        wkv_l = w_mem_kv[l].astype(BF16)
        w1_l = w_ffn_in[l].astype(BF16)
        w2_l = w_ffn_out[l].astype(BF16)
        lns = [_ln_rows(ln_g[l, i], ln_b[l, i]) for i in range(3)]
        hn = hgrn_norm[l][None]
        gn = gdn_norm[l][None]
        dn = diff_norm[l][None]
        head_params = jnp.stack([jnp.exp(gdn_a_log[l].astype(F32)), gdn_dt_bias[l].astype(F32)])
        lam_init = 0.8 - 0.6 * math.exp(-0.3 * l)
        lq1, lk1, lq2, lk2 = diff_lambda[l].astype(F32)
        lam = jnp.exp(jnp.sum(lq1 * lk1)) - jnp.exp(jnp.sum(lq2 * lk2)) + lam_init
        scalars = jnp.stack([jnp.stack([lam, jnp.float32(1.0 - lam_init), jnp.float32(0), jnp.float32(0)]),
                             far_bias])
        conv_w_l = gdn_conv_w[l]

        proj = _matmul(xp, w_in_l, 512, PROJ_W // 3)
        oa, h_new = _hgrn_prompt(proj, lbp[l], hn, bp, seq)
        conv_in = proj[:, b0:b0 + CONV_CH]
        qkv = _gdn_conv(conv_in, conv_w_l, SEQ_TILE, seq)
        ob, g_new = _gdn_prompt(qkv, proj, head_params, gn, bp, seq)
        oc = _diff_prompt(proj, scalars, d0, d1, dn, bp, seq)
        x1 = _merge(oa, ob, oc, proj, xp, wb_l, wo_l, lns[0], 512)
        mkv = _matmul(mem, wkv_l, 512, 2 * BR_W).reshape(bp, n_mem, 2 * BR_W)
        x2 = _xattn_prompt(x1, mkv, wq_l, wxo_l, lns[1], bp, seq, 512)
        xp = _ffn(x2, w1_l, w2_l, lns[2], 512, D_FF // 2)
        outs["pk"].append(proj[:, kc0:kc0 + BR_W].reshape(bp, seq, N_HEADS, HEAD_W))
        outs["pv"].append(proj[:, vc0:vc0 + BR_W].reshape(bp, seq, N_HEADS, HEAD_W))
        outs["pmk"].append(mkv[:, :, :BR_W].reshape(bp, n_mem, N_HEADS, HEAD_W))
        outs["pmv"].append(mkv[:, :, BR_W:].reshape(bp, n_mem, N_HEADS, HEAD_W))
        outs["ph"].append(h_new)
        outs["pg"].append(g_new)
        outs["pc"].append(conv_in.reshape(bp, seq, CONV_CH)[:, seq - (CONV_W - 1):])

        proj_s = _matmul(xs, w_in_l, bs * SAMPLE_PAD, PROJ_W // 3)
        oa, h_new = _hgrn_sample(proj_s, lbp[l], hn, state_hgrn[l], ls)
        conv_tok = proj_s[:, b0:b0 + CONV_CH].reshape(bs, SAMPLE_PAD, CONV_CH)
        conv_in = jnp.concatenate([conv_tok[:, :pad - (CONV_W - 1)], state_gdn_conv[l], conv_tok[:, pad:]], axis=1)
        qkv = _gdn_conv(conv_in.reshape(bs * SAMPLE_PAD, CONV_CH), conv_w_l, bs * SAMPLE_PAD, SAMPLE_PAD)
        ob, g_new = _gdn_sample(qkv, proj_s, head_params, gn, state_gdn[l], ls)
        oc = _diff_sample(page_table, scalars, proj_s,
                          cache_attn_k[l].reshape(n_pool, PAGE_SIZE, BR_W),
                          cache_attn_v[l].reshape(n_pool, PAGE_SIZE, BR_W), bias_s, new, dn, ls)
        x1 = _merge(oa, ob, oc, proj_s, xs, wb_l, wo_l, lns[0], 512)
        x2 = _xattn_sample(x1, cache_mem_k[l].reshape(bs, n_mem, BR_W), cache_mem_v[l].reshape(bs, n_mem, BR_W),
                           wq_l, wxo_l, lns[1])
        xs = _ffn(x2, w1_l, w2_l, lns[2], 512, D_FF // 2)
        tok = proj_s.reshape(bs, SAMPLE_PAD, PROJ_W)[:, pad:]
        outs["sk"].append(tok[:, :, kc0:kc0 + BR_W].reshape(bs, ls, N_HEADS, HEAD_W))
        outs["sv"].append(tok[:, :, vc0:vc0 + BR_W].reshape(bs, ls, N_HEADS, HEAD_W))
        outs["sh"].append(h_new)
        outs["sg"].append(g_new)
        outs["sc"].append(conv_in[:, SAMPLE_PAD - (CONV_W - 1):])

    y_prompt = xp.reshape(bp, seq, D_MODEL)
    y_sample = xs.reshape(bs, SAMPLE_PAD, D_MODEL)[:, pad:]
    st = {k: jnp.stack(v) for k, v in outs.items()}
    return (y_prompt, y_sample, st["pk"], st["pv"], st["pmk"], st["pmv"], st["ph"], st["pg"], st["pc"],
            st["sk"], st["sv"], st["sh"], st["sg"], st["sc"])
```

```python
import functools
import math

import jax
import jax.numpy as jnp
from jax import lax
from jax.experimental import pallas as pl
from jax.experimental.pallas import tpu as pltpu

F32 = jnp.float32
BF16 = jnp.bfloat16

D_MODEL = 1024
DEPTH = 4
PAGE_SIZE = 128
BR_W = D_MODEL // 2
N_HEADS = 4
HEAD_W = BR_W // N_HEADS
DH_C = HEAD_W // 2
CONV_W = 4
CONV_CH = 3 * BR_W
N_BRANCH = 3
N_BUCKETS = 32
MAX_DISTANCE = 128
D_FF = -(-8 * D_MODEL // (3 * 256)) * 256
LN_EPS = 1e-5
NORM_EPS = 1e-6
NEG_INF = -1e30
ALPHA = (2 * DEPTH) ** 0.25

LANES = 128
SUBLANES = 8
VMEM_LIMIT_BYTES = 56 * 1024 * 1024

BLK_A = N_BRANCH * D_MODEL // BR_W
BLK_B = BLK_A + 4
BLK_CQ = BLK_B + 4
MAIN_W = (BLK_CQ + 1) * BR_W
KV_W = 2 * BR_W + LANES

HGRN_SUB = 16
GDN_CHUNK = 64
SAMPLE_PAD = 8
SEQ_TILE = 512
ATT_BLOCK = 512
ROW_TILE = 512


def _cparams(n_axes):
    return pltpu.CompilerParams(dimension_semantics=("arbitrary",) * n_axes,
                                vmem_limit_bytes=VMEM_LIMIT_BYTES)


def _dot(a, b):
    return jnp.dot(a.astype(BF16), b.astype(BF16), preferred_element_type=F32)


def _dot_nt(a, b):
    return lax.dot_general(a.astype(BF16), b.astype(BF16), (((1,), (1,)), ((), ())),
                           preferred_element_type=F32)


def _dot_tn(a, b):
    return lax.dot_general(a.astype(BF16), b.astype(BF16), (((0,), (0,)), ((), ())),
                           preferred_element_type=F32)


def _dot_exact_lhs(m01, x):
    hi = x.astype(BF16)
    r1 = x - hi.astype(F32)
    mid = r1.astype(BF16)
    lo = (r1 - mid.astype(F32)).astype(BF16)
    dot = functools.partial(jnp.dot, preferred_element_type=F32)
    return dot(m01, hi) + dot(m01, mid) + dot(m01, lo)


def _sigmoid(x):
    return 1.0 / (1.0 + jnp.exp(-x))


def _silu(x):
    return x * _sigmoid(x)


def _softplus(x):
    return jnp.maximum(x, 0.0) + jnp.log1p(jnp.exp(-jnp.abs(x)))


def _layer_norm(h, g, b):
    mu = jnp.mean(h, axis=-1, keepdims=True)
    c = h - mu
    var = jnp.mean(c * c, axis=-1, keepdims=True)
    return c * lax.rsqrt(var + LN_EPS) * g + b


def _rms_norm(x, g):
    return x * lax.rsqrt(jnp.mean(x * x, axis=-1, keepdims=True) + NORM_EPS) * g


def _bcast_rows(x, rows, reps):
    return jnp.concatenate([jnp.broadcast_to(x[r:r + 1, :], (reps, x.shape[1])) for r in rows], axis=0)


def _head(x, h):
    return x[:, h * HEAD_W:(h + 1) * HEAD_W]


ALIASED = pl.BlockSpec(memory_space=pl.ANY)


def _mm_body(x_ref, w_ref, o_ref):
    o_ref[...] = _dot(x_ref[...], w_ref[...])


def _matmul(x, w, tm, tn):
    t, k = x.shape
    n = w.shape[1]
    tm = min(tm, t)
    return pl.pallas_call(
        _mm_body,
        grid=(n // tn, t // tm),
        in_specs=[pl.BlockSpec((tm, k), lambda j, i: (i, 0)),
                  pl.BlockSpec((k, tn), lambda j, i: (0, j))],
        out_specs=pl.BlockSpec((tm, tn), lambda j, i: (i, j)),
        out_shape=jax.ShapeDtypeStruct((t, n), F32),
        compiler_params=_cparams(2),
        name="proj_matmul",
    )(x, w)


def _kv_body(x_ref, w_ref, *refs):
    k_ref, v_ref, ab_ref = refs[-3:]
    y = _dot(x_ref[...], w_ref[...])
    k_ref[...] = y[:, :BR_W]
    v_ref[...] = y[:, BR_W:2 * BR_W]
    ab_ref[...] = y[:, 2 * BR_W:]


def _kv_proj(x, w, layer, k_prev, v_prev):
    t = x.shape[0]
    tm = min(ROW_TILE, t)
    slab = pl.BlockSpec((None, tm, BR_W), lambda i: (layer, i, 0))
    stacked = jax.ShapeDtypeStruct((DEPTH, t, BR_W), F32)
    return pl.pallas_call(
        _kv_body,
        grid=(t // tm,),
        in_specs=[pl.BlockSpec((tm, D_MODEL), lambda i: (i, 0)),
                  pl.BlockSpec(w.shape, lambda i: (0, 0)), ALIASED, ALIASED],
        out_specs=[slab, slab, pl.BlockSpec((tm, LANES), lambda i: (i, 0))],
        out_shape=[stacked, stacked, jax.ShapeDtypeStruct((t, LANES), F32)],
        input_output_aliases={2: 0, 3: 1},
        compiler_params=_cparams(1),
        name="kv_proj",
    )(x, w, k_prev, v_prev)


def _hgrn_rows(q, z, v, lbp, sub, tri, pos, valid):
    r = q.shape[0]
    n_sub = r // sub
    log_lb, log1m_lb, om_lb = lbp[0:1, :], lbp[1:2, :], lbp[2:3, :]
    log_sig = jnp.minimum(z, 0.0) - jnp.log1p(jnp.exp(-jnp.abs(z)))
    t2 = log1m_lb + log_sig
    logf = jnp.maximum(log_lb, t2) + jnp.log1p(jnp.exp(-jnp.abs(log_lb - t2)))
    k = om_lb * (1.0 / (1.0 + jnp.exp(z)))
    if valid is not None:
        logf = jnp.where(valid, logf, 0.0)
        k = jnp.where(valid, k, 0.0)
    bc = _dot_exact_lhs(tri, logf)
    last_rows = [i * sub + sub - 1 for i in range(n_sub)]
    b_last = _bcast_rows(bc, last_rows, sub)
    q_hat = q * jnp.exp(bc)
    k_hat = k * jnp.exp(b_last - bc)
    o = jnp.zeros_like(q)
    for s in range(sub):
        rows = [i * sub + s for i in range(n_sub)]
        ks = _bcast_rows(k, rows, sub)
        bs = _bcast_rows(bc, rows, sub)
        vs = _bcast_rows(v, rows, sub)
        a = q * ks * jnp.exp(jnp.minimum(bc - bs, 0.0))
        a = jnp.where(pos >= s, a, 0.0)
        o = o + jnp.sum(a, axis=-1, keepdims=True) * vs
    return q_hat, k_hat, o, bc


def _sub_chunk_masks(rows, sub):
    ri = lax.broadcasted_iota(jnp.int32, (rows, rows), 0)
    ci = lax.broadcasted_iota(jnp.int32, (rows, rows), 1)
    tri = ((ri // sub == ci // sub) & (ci <= ri)).astype(BF16)
    pos = lax.broadcasted_iota(jnp.int32, (rows, 1), 0) % sub
    return tri, pos


def _hgrn_finish(o, g, norm_w):
    return _rms_norm(o, norm_w) * _silu(g)


def _hgrn_prompt_body(q_ref, f_ref, i_ref, g_ref, lbp_ref, nw_ref, o_ref, s_out_ref, st_ref, *, n_tiles):
    t = pl.program_id(1)

    @pl.when(t == 0)
    def _():
        st_ref[...] = jnp.zeros_like(st_ref)

    rows = 4 * HGRN_SUB
    nw = nw_ref[...]
    tri, pos = _sub_chunk_masks(rows, HGRN_SUB)

    def chunk(c, carry):
        sl = pl.ds(pl.multiple_of(c * rows, rows), rows)
        qa, za, va, ga = q_ref[sl, :], f_ref[sl, :], i_ref[sl, :], g_ref[sl, :]
        heads = []
        for h in range(N_HEADS):
            v = _head(va, h)
            q_hat, k_hat, o, bc = _hgrn_rows(_head(qa, h), _head(za, h), v, lbp_ref[h], HGRN_SUB, tri, pos, None)
            outs = []
            st = st_ref[h]
            for i in range(rows // HGRN_SUB):
                lo, hi = i * HGRN_SUB, (i + 1) * HGRN_SUB
                outs.append(o[lo:hi] + _dot_nt(q_hat[lo:hi], st))
                st = st * jnp.exp(bc[hi - 1:hi, :]) + _dot_tn(v[lo:hi], k_hat[lo:hi])
            st_ref[h] = st
            heads.append(_hgrn_finish(jnp.concatenate(outs, axis=0), _head(ga, h), nw))
        o_ref[sl, :] = jnp.concatenate(heads, axis=1)
        return carry

    lax.fori_loop(0, SEQ_TILE // rows, chunk, 0)

    @pl.when(t == n_tiles - 1)
    def _():
        for h in range(N_HEADS):
            s_out_ref[0, h] = st_ref[h].T


def _hgrn_prompt(proj, lbp, norm_w, batch, seq):
    n_tiles = seq // SEQ_TILE

    def col(blk):
        return pl.BlockSpec((SEQ_TILE, BR_W), lambda b, t: (b * n_tiles + t, blk))

    return pl.pallas_call(
        functools.partial(_hgrn_prompt_body, n_tiles=n_tiles),
        grid=(batch, n_tiles),
        in_specs=[col(BLK_A), col(BLK_A + 1), col(BLK_A + 2), col(BLK_A + 3),
                  pl.BlockSpec((N_HEADS, SUBLANES, HEAD_W), lambda b, t: (0, 0, 0)),
                  pl.BlockSpec((1, HEAD_W), lambda b, t: (0, 0))],
        out_specs=[pl.BlockSpec((SEQ_TILE, BR_W), lambda b, t: (b * n_tiles + t, 0)),
                   pl.BlockSpec((1, N_HEADS, HEAD_W, HEAD_W), lambda b, t: (b, 0, 0, 0))],
        out_shape=[jax.ShapeDtypeStruct((batch * seq, BR_W), F32),
                   jax.ShapeDtypeStruct((batch, N_HEADS, HEAD_W, HEAD_W), F32)],
        scratch_shapes=[pltpu.VMEM((N_HEADS, HEAD_W, HEAD_W), F32)],
        compiler_params=_cparams(2),
        name="hgrn_prompt",
    )(proj, proj, proj, proj, lbp, norm_w)


def _hgrn_sample_body(q_ref, f_ref, i_ref, g_ref, lbp_ref, nw_ref, s0_ref, *refs, n_seq, n_real):
    o_ref, s_out_ref = refs[-2:]
    nw = nw_ref[...]
    rows = n_seq * SAMPLE_PAD
    tri, pos = _sub_chunk_masks(rows, SAMPLE_PAD)
    valid = pos >= SAMPLE_PAD - n_real
    qa, za, va, ga = q_ref[...], f_ref[...], i_ref[...], g_ref[...]
    heads = []
    for h in range(N_HEADS):
        v = _head(va, h)
        q_hat, k_hat, o, bc = _hgrn_rows(_head(qa, h), _head(za, h), v, lbp_ref[h], SAMPLE_PAD, tri, pos, valid)
        outs = []
        for i in range(n_seq):
            lo, hi = i * SAMPLE_PAD, (i + 1) * SAMPLE_PAD
            s0 = s0_ref[i, h]
            outs.append(o[lo:hi] + _dot(q_hat[lo:hi], s0))
            decay = jnp.exp(bc[hi - 1:hi, :])
            s_t = s0.T * decay + _dot_tn(v[lo:hi], k_hat[lo:hi])
            s_out_ref[i, h] = s_t.T
        heads.append(_hgrn_finish(jnp.concatenate(outs, axis=0), _head(ga, h), nw))
    o_ref[...] = jnp.concatenate(heads, axis=1)


def _state_specs(layer, seq_per_step):
    return pl.BlockSpec((None, seq_per_step, N_HEADS, HEAD_W, HEAD_W), lambda g: (layer, g, 0, 0, 0))


def _hgrn_sample(proj, lbp, norm_w, states, layer, prev, n_real, seq_per_step=8):
    n_batch = states.shape[1]
    rows = seq_per_step * SAMPLE_PAD

    def col(blk):
        return pl.BlockSpec((rows, BR_W), lambda g: (g, blk))

    return pl.pallas_call(
        functools.partial(_hgrn_sample_body, n_seq=seq_per_step, n_real=n_real),
        grid=(n_batch // seq_per_step,),
        in_specs=[col(BLK_A), col(BLK_A + 1), col(BLK_A + 2), col(BLK_A + 3),
                  pl.BlockSpec((N_HEADS, SUBLANES, HEAD_W), lambda g: (0, 0, 0)),
                  pl.BlockSpec((1, HEAD_W), lambda g: (0, 0)),
                  _state_specs(layer, seq_per_step), ALIASED],
        out_specs=[pl.BlockSpec((rows, BR_W), lambda g: (g, 0)), _state_specs(layer, seq_per_step)],
        out_shape=[jax.ShapeDtypeStruct((n_batch * SAMPLE_PAD, BR_W), F32),
                   jax.ShapeDtypeStruct(states.shape, F32)],
        input_output_aliases={7: 1},
        compiler_params=_cparams(1),
        name="hgrn_sample",
    )(proj, proj, proj, proj, lbp, norm_w, states, prev)


def _conv_body(prev_ref, cur_ref, w_ref, o_ref, buf_ref, *, tile, seq_len):
    i = pl.program_id(0)
    c = pl.program_id(1)
    buf_ref[0:SUBLANES, :] = prev_ref[...]
    buf_ref[SUBLANES:, :] = cur_ref[...]
    pos = (i * tile + lax.broadcasted_iota(jnp.int32, (tile, 1), 0)) % seq_len
    w = w_ref[...]
    acc = cur_ref[...] * w[CONV_W - 1:CONV_W, :]
    for back in range(1, CONV_W):
        shifted = buf_ref[pl.ds(SUBLANES - back, tile), :]
        acc = acc + jnp.where(pos >= back, shifted, 0.0) * w[CONV_W - 1 - back:CONV_W - back, :]
    act = _silu(acc)
    inv = lax.rsqrt(jnp.sum(act * act, axis=-1, keepdims=True) + NORM_EPS)
    scale = jnp.where(c < N_HEADS, inv * HEAD_W ** -0.5, jnp.where(c < 2 * N_HEADS, inv, 1.0))
    o_ref[...] = act * scale


def _gdn_conv(conv_in, col0, conv_w, tile, seq_len):
    t = conv_in.shape[0]
    per = tile // SUBLANES
    return pl.pallas_call(
        functools.partial(_conv_body, tile=tile, seq_len=seq_len),
        grid=(t // tile, CONV_CH // HEAD_W),
        in_specs=[pl.BlockSpec((SUBLANES, HEAD_W), lambda i, c: (jnp.maximum(i * per - 1, 0), col0 + c)),
                  pl.BlockSpec((tile, HEAD_W), lambda i, c: (i, col0 + c)),
                  pl.BlockSpec((CONV_W, HEAD_W), lambda i, c: (0, c))],
        out_specs=pl.BlockSpec((tile, HEAD_W), lambda i, c: (i, c)),
        out_shape=jax.ShapeDtypeStruct((t, CONV_CH), F32),
        scratch_shapes=[pltpu.VMEM((tile + SUBLANES, HEAD_W), F32)],
        compiler_params=_cparams(2),
        name="gdn_conv",
    )(conv_in, conv_in, conv_w)


def _forward_substitute(lmat, rhs):
    c = lmat.shape[0]
    nb = c // SUBLANES
    xb = [rhs[b * SUBLANES:(b + 1) * SUBLANES, :] for b in range(nb)]
    lb = [lmat[b * SUBLANES:(b + 1) * SUBLANES, :] for b in range(nb)]
    for j in range(c - 1):
        xj = xb[j // SUBLANES][j % SUBLANES:j % SUBLANES + 1, :]
        for b in range((j + 1) // SUBLANES, nb):
            xb[b] = xb[b] - lb[b][:, j:j + 1] * xj
    return jnp.concatenate(xb, axis=0) if nb > 1 else xb[0]


def _gdn_masks(c):
    ri = lax.broadcasted_iota(jnp.int32, (c, c), 0)
    ci = lax.broadcasted_iota(jnp.int32, (c, c), 1)
    mr = lax.broadcasted_iota(jnp.int32, (c, LANES), 0)
    mc = lax.broadcasted_iota(jnp.int32, (c, LANES), 1)
    return ci <= ri, ci < ri, ((mc < mr) & (mc < c)) | (mc == c)


def _gdn_chunk(q, k, v, a_col, b_col, s, a_scale, dt_bias, masks, valid):
    c = q.shape[0]
    incl, strict, sel = masks
    g = -a_scale * _softplus(a_col + dt_bias)
    beta = _sigmoid(b_col)
    if valid is not None:
        g = jnp.where(valid, g, 0.0)
        beta = jnp.where(valid, beta, 0.0)
    m = _dot_exact_lhs(incl.astype(BF16), jnp.where(sel, g, 0.0))
    gcum = m[:, c:c + 1]
    decay = jnp.where(incl, jnp.exp(jnp.where(incl, m[:, :c], 0.0)), 0.0)
    kb = k * beta
    lmat = jnp.where(strict, _dot_nt(kb, k) * decay, 0.0)
    e_g = jnp.exp(gcum)
    sol = _forward_substitute(lmat, jnp.concatenate([v * beta, kb * e_g], axis=1))
    u, w = sol[:, :HEAD_W], sol[:, HEAD_W:]
    v_new = u - _dot(w, s)
    attn = jnp.where(incl, _dot_nt(q, k) * decay, 0.0)
    o = _dot(q * e_g, s) + _dot(attn, v_new)
    g_last = gcum[c - 1:c, :]
    s_new = s * jnp.exp(g_last) + _dot_tn(k * jnp.exp(g_last - gcum), v_new)
    return o, s_new


def _gdn_heads(hp_ref, qa, ka, va, ga, ab, nw, get_state, put_state, masks, valid):
    lane = lax.broadcasted_iota(jnp.int32, ab.shape, 1)
    heads = []
    for h in range(N_HEADS):
        a_col = jnp.sum(jnp.where(lane == h, ab, 0.0), axis=-1, keepdims=True)
        b_col = jnp.sum(jnp.where(lane == h + N_HEADS, ab, 0.0), axis=-1, keepdims=True)
        o, s_new = _gdn_chunk(_head(qa, h), _head(ka, h), _head(va, h), a_col, b_col, get_state(h),
                              hp_ref[0, h], hp_ref[1, h], masks, valid)
        put_state(h, s_new)
        heads.append(_rms_norm(o, nw) * _silu(_head(ga, h)))
    return jnp.concatenate(heads, axis=1)


def _gdn_prompt_body(hp_ref, q_ref, k_ref, v_ref, g_ref, ab_ref, nw_ref, o_ref, s_out_ref, s_ref, *, n_tiles):
    t = pl.program_id(1)

    @pl.when(t == 0)
    def _():
        s_ref[...] = jnp.zeros_like(s_ref)

    nw = nw_ref[...]
    masks = _gdn_masks(GDN_CHUNK)

    def put(h, s_new):
        s_ref[h] = s_new

    def chunk(c, carry):
        sl = pl.ds(pl.multiple_of(c * GDN_CHUNK, GDN_CHUNK), GDN_CHUNK)
        o_ref[sl, :] = _gdn_heads(hp_ref, q_ref[sl, :], k_ref[sl, :], v_ref[sl, :], g_ref[sl, :], ab_ref[sl, :],
                                  nw, lambda h: s_ref[h], put, masks, None)
        return carry

    lax.fori_loop(0, SEQ_TILE // GDN_CHUNK, chunk, 0)

    @pl.when(t == n_tiles - 1)
    def _():
        s_out_ref[0] = s_ref[...]


def _gdn_prompt(qkv, proj, ab, head_params, norm_w, batch, seq):
    n_tiles = seq // SEQ_TILE

    def col(blk):
        return pl.BlockSpec((SEQ_TILE, BR_W), lambda b, t: (b * n_tiles + t, blk))

    return pl.pallas_call(
        functools.partial(_gdn_prompt_body, n_tiles=n_tiles),
        grid=(batch, n_tiles),
        in_specs=[pl.BlockSpec(memory_space=pltpu.SMEM),
                  col(0), col(1), col(2), col(BLK_B + 3),
                  pl.BlockSpec((SEQ_TILE, LANES), lambda b, t: (b * n_tiles + t, 0)),
                  pl.BlockSpec((1, HEAD_W), lambda b, t: (0, 0))],
        out_specs=[pl.BlockSpec((SEQ_TILE, BR_W), lambda b, t: (b * n_tiles + t, 0)),
                   pl.BlockSpec((1, N_HEADS, HEAD_W, HEAD_W), lambda b, t: (b, 0, 0, 0))],
        out_shape=[jax.ShapeDtypeStruct((batch * seq, BR_W), F32),
                   jax.ShapeDtypeStruct((batch, N_HEADS, HEAD_W, HEAD_W), F32)],
        scratch_shapes=[pltpu.VMEM((N_HEADS, HEAD_W, HEAD_W), F32)],
        compiler_params=_cparams(2),
        name="gdn_prompt",
    )(head_params, qkv, qkv, qkv, proj, ab, norm_w)


def _gdn_sample_body(hp_ref, q_ref, k_ref, v_ref, g_ref, ab_ref, nw_ref, s0_ref, *refs, n_seq, n_real):
    o_ref, s_out_ref = refs[-2:]
    nw = nw_ref[...]
    masks = _gdn_masks(SAMPLE_PAD)
    valid = lax.broadcasted_iota(jnp.int32, (SAMPLE_PAD, 1), 0) >= SAMPLE_PAD - n_real

    def one(i, carry):
        sl = pl.ds(pl.multiple_of(i * SAMPLE_PAD, SAMPLE_PAD), SAMPLE_PAD)

        def put(h, s_new):
            s_out_ref[i, h] = s_new

        o_ref[sl, :] = _gdn_heads(hp_ref, q_ref[sl, :], k_ref[sl, :], v_ref[sl, :], g_ref[sl, :], ab_ref[sl, :],
                                  nw, lambda h: s0_ref[i, h], put, masks, valid)
        return carry

    lax.fori_loop(0, n_seq, one, 0)


def _gdn_sample(qkv, proj, ab, head_params, norm_w, states, layer, prev, n_real, seq_per_step=8):
    n_batch = states.shape[1]
    rows = seq_per_step * SAMPLE_PAD

    def col(blk):
        return pl.BlockSpec((rows, BR_W), lambda g: (g, blk))

    return pl.pallas_call(
        functools.partial(_gdn_sample_body, n_seq=seq_per_step, n_real=n_real),
        grid=(n_batch // seq_per_step,),
        in_specs=[pl.BlockSpec(memory_space=pltpu.SMEM),
                  col(0), col(1), col(2), col(BLK_B + 3),
                  pl.BlockSpec((rows, LANES), lambda g: (g, 0)),
                  pl.BlockSpec((1, HEAD_W), lambda g: (0, 0)),
                  _state_specs(layer, seq_per_step), ALIASED],
        out_specs=[pl.BlockSpec((rows, BR_W), lambda g: (g, 0)), _state_specs(layer, seq_per_step)],
        out_shape=[jax.ShapeDtypeStruct((n_batch * SAMPLE_PAD, BR_W), F32),
                   jax.ShapeDtypeStruct(states.shape, F32)],
        input_output_aliases={8: 1},
        compiler_params=_cparams(1),
        name="gdn_sample",
    )(head_params, qkv, qkv, qkv, proj, ab, norm_w, states, prev)


def t5_bucket(rel):
    n = jnp.maximum(-rel, 0)
    exact = N_BUCKETS // 2
    log_part = jnp.log(jnp.maximum(n, 1).astype(F32) / exact) / math.log(MAX_DISTANCE / exact)
    large = jnp.minimum(exact + (log_part * (N_BUCKETS - exact)).astype(jnp.int32), N_BUCKETS - 1)
    return jnp.where(n < exact, n, large)


def _bias_tile_body(thr_ref, rb_ref, o_ref):
    h = pl.program_id(0)
    blk = ATT_BLOCK
    row = lax.broadcasted_iota(jnp.int32, (blk, blk), 0)
    col = lax.broadcasted_iota(jnp.int32, (blk, blk), 1)
    for which in range(2):
        n = row - col + which * blk
        val = jnp.full((blk, blk), rb_ref[h, 0], F32)
        for b in range(1, N_BUCKETS):
            val = jnp.where(n >= thr_ref[b], rb_ref[h, b], val)
        if which == 0:
            val = jnp.where(n >= 0, val, NEG_INF)
        o_ref[which] = val


def _bias_tiles(thr, rel_bias_t):
    return pl.pallas_call(
        _bias_tile_body,
        grid=(N_HEADS,),
        in_specs=[pl.BlockSpec(memory_space=pltpu.SMEM), pl.BlockSpec(memory_space=pltpu.SMEM)],
        out_specs=pl.BlockSpec((None, 2, ATT_BLOCK, ATT_BLOCK), lambda h: (h, 0, 0, 0)),
        out_shape=jax.ShapeDtypeStruct((N_HEADS, 2, ATT_BLOCK, ATT_BLOCK), F32),
        compiler_params=_cparams(1),
        name="bias_tiles",
    )(thr, rel_bias_t)


def _stack_maps(q):
    lane = lax.broadcasted_iota(jnp.int32, q.shape, 1)
    return jnp.concatenate([jnp.where(lane < DH_C, q, 0.0), jnp.where(lane >= DH_C, q, 0.0)], axis=0)


def _diff_prompt_body(sc_ref, q_ref, k_ref, v_ref, d_ref, nw_ref, o_ref, k16_ref, v16_ref, m_ref, acc_ref):
    h = pl.program_id(1)
    qi = pl.program_id(2)
    blk = ATT_BLOCK

    @pl.when(qi == 0)
    def _():
        k16_ref[...] = k_ref[...].astype(BF16)
        v16_ref[:, :HEAD_W] = v_ref[...].astype(BF16)
        v16_ref[:, HEAD_W:] = jnp.ones((v16_ref.shape[0], HEAD_W), BF16)

    lam = sc_ref[0, 0]
    out_scale = sc_ref[0, 1]
    far_bias = sc_ref[1, h]
    qs = _stack_maps(q_ref[...] * DH_C ** -0.5).astype(BF16)
    m_ref[...] = jnp.full_like(m_ref, -jnp.inf)
    acc_ref[...] = jnp.zeros_like(acc_ref)

    def block(kj, bias):
        sl = pl.ds(pl.multiple_of(kj * blk, blk), blk)
        s = lax.dot_general(qs, k16_ref[sl, :], (((1,), (1,)), ((), ())), preferred_element_type=F32)
        if bias.ndim == 2:
            s = (s.reshape(2, blk, blk) + bias[None]).reshape(2 * blk, blk)
        else:
            s = s + bias
        m_old = m_ref[...]
        m_new = jnp.maximum(m_old, jnp.max(s, axis=-1, keepdims=True))
        alpha = jnp.exp(m_old - m_new)
        p = jnp.exp(s - m_new)
        acc_ref[...] = alpha * acc_ref[...] + jnp.dot(p.astype(BF16), v16_ref[sl, :], preferred_element_type=F32)
        m_ref[...] = m_new

    def far(kj, carry):
        block(kj, far_bias)
        return carry

    lax.fori_loop(0, jnp.maximum(qi - 1, 0), far, 0)

    @pl.when(qi >= 1)
    def _():
        block(qi - 1, d_ref[1])

    block(qi, d_ref[0])
    acc = acc_ref[...]
    o = acc[:, :HEAD_W] / acc[:, HEAD_W:]
    o = o[:blk] - lam * o[blk:]
    o_ref[...] = _rms_norm(o, nw_ref[...]) * out_scale


def _diff_prompt(proj, k_all, v_all, layer, scalars, tiles, norm_w, batch, seq):
    nq = seq // ATT_BLOCK
    kv = pl.BlockSpec((None, seq, HEAD_W), lambda b, h, q: (layer, b, h))
    return pl.pallas_call(
        _diff_prompt_body,
        grid=(batch, N_HEADS, nq),
        in_specs=[pl.BlockSpec(memory_space=pltpu.SMEM),
                  pl.BlockSpec((ATT_BLOCK, HEAD_W), lambda b, h, q: (b * nq + q, BLK_CQ * N_HEADS + h)),
                  kv, kv,
                  pl.BlockSpec((None, 2, ATT_BLOCK, ATT_BLOCK), lambda b, h, q: (h, 0, 0, 0)),
                  pl.BlockSpec((1, HEAD_W), lambda b, h, q: (0, 0))],
        out_specs=pl.BlockSpec((ATT_BLOCK, HEAD_W), lambda b, h, q: (b * nq + q, h)),
        out_shape=jax.ShapeDtypeStruct((batch * seq, BR_W), F32),
        scratch_shapes=[pltpu.VMEM((seq, HEAD_W), BF16), pltpu.VMEM((seq, 2 * HEAD_W), BF16),
                        pltpu.VMEM((2 * ATT_BLOCK, 1), F32),
                        pltpu.VMEM((2 * ATT_BLOCK, 2 * HEAD_W), F32)],
        compiler_params=_cparams(3),
        name="diff_prompt",
    )(scalars, proj, k_all, v_all, tiles, norm_w)


def _diff_sample_body(pt_ref, sc_ref, q_ref, kn_ref, vn_ref, ck_ref, cv_ref, bias_ref, nbias_ref, nw_ref,
                      o_ref, kbuf, vbuf, sem, *, layer, n_pages, n_real, n_batch):
    b = pl.program_id(0)
    slot = b % 2
    pad = SAMPLE_PAD - n_real

    def page_copies(seq_idx, sl):
        cps = []
        for j in range(n_pages):
            page = pt_ref[seq_idx, j]
            rows = pl.ds(j * PAGE_SIZE, PAGE_SIZE)
            cps.append(pltpu.make_async_copy(ck_ref.at[layer, page], kbuf.at[sl, rows], sem.at[sl, 0]))
            cps.append(pltpu.make_async_copy(cv_ref.at[layer, page], vbuf.at[sl, rows], sem.at[sl, 1]))
        return cps

    @pl.when(b == 0)
    def _():
        for cp in page_copies(0, 0):
            cp.start()

    @pl.when(b + 1 < n_batch)
    def _():
        for cp in page_copies(b + 1, 1 - slot):
            cp.start()

    for cp in page_copies(b, slot):
        cp.wait()

    lam = sc_ref[0, 0]
    out_scale = sc_ref[0, 1]
    outs = []
    for h in range(N_HEADS):
        hs = slice(h * HEAD_W, (h + 1) * HEAD_W)
        qs = _stack_maps(q_ref[pad:, hs] * DH_C ** -0.5)
        s = _dot_nt(qs, kbuf[slot, :, hs]) + bias_ref[h]
        kn = kn_ref[pad:, hs]
        vn = vn_ref[pad:, hs]
        s_new = [jnp.sum(qs * kn[t:t + 1, :], axis=-1, keepdims=True) + nbias_ref[h, t] for t in range(n_real)]
        m = jnp.max(s, axis=-1, keepdims=True)
        for sn in s_new:
            m = jnp.maximum(m, sn)
        p = jnp.exp(s - m)
        l = jnp.sum(p, axis=-1, keepdims=True)
        acc = _dot(p, vbuf[slot, :, hs])
        for t, sn in enumerate(s_new):
            pn = jnp.exp(sn - m)
            l = l + pn
            acc = acc + pn * vn[t:t + 1, :]
        o = acc / l
        o = o[:n_real] - lam * o[n_real:]
        outs.append(_rms_norm(o, nw_ref[...]) * out_scale)
    o_ref[0:pad, :] = jnp.zeros((pad, BR_W), F32)
    o_ref[pad:, :] = jnp.concatenate(outs, axis=1)


def _diff_sample(page_table, scalars, proj, k_new, v_new, cache_k, cache_v, layer, bias, nbias, norm_w, n_real):
    n_batch, n_pages = page_table.shape
    n_past = n_pages * PAGE_SIZE
    new_tok = pl.BlockSpec((None, SAMPLE_PAD, BR_W), lambda b, pt: (layer, b, 0))
    grid_spec = pltpu.PrefetchScalarGridSpec(
        num_scalar_prefetch=1,
        grid=(n_batch,),
        in_specs=[pl.BlockSpec(memory_space=pltpu.SMEM),
                  pl.BlockSpec((SAMPLE_PAD, BR_W), lambda b, pt: (b, BLK_CQ)),
                  new_tok, new_tok,
                  pl.BlockSpec(memory_space=pl.ANY), pl.BlockSpec(memory_space=pl.ANY),
                  pl.BlockSpec(bias.shape, lambda b, pt: (0, 0, 0)),
                  pl.BlockSpec(nbias.shape, lambda b, pt: (0, 0, 0, 0)),
                  pl.BlockSpec((1, HEAD_W), lambda b, pt: (0, 0))],
        out_specs=pl.BlockSpec((SAMPLE_PAD, BR_W), lambda b, pt: (b, 0)),
        scratch_shapes=[pltpu.VMEM((2, n_past, BR_W), F32), pltpu.VMEM((2, n_past, BR_W), F32),
                        pltpu.SemaphoreType.DMA((2, 2))],
    )
    return pl.pallas_call(
        functools.partial(_diff_sample_body, layer=layer, n_pages=n_pages, n_real=n_real, n_batch=n_batch),
        grid_spec=grid_spec,
        out_shape=jax.ShapeDtypeStruct((n_batch * SAMPLE_PAD, BR_W), F32),
        compiler_params=_cparams(1),
        name="diff_sample",
    )(page_table, scalars, proj, k_new, v_new, cache_k, cache_v, bias, nbias, norm_w)


def _merge_body(oa_ref, ob_ref, oc_ref, g0_ref, g1_ref, g2_ref, x_ref, wb_ref, wo_ref, ln_ref, o_ref):
    m = (_sigmoid(g0_ref[...]) * _dot(oa_ref[...], wb_ref[0])
         + _sigmoid(g1_ref[...]) * _dot(ob_ref[...], wb_ref[1])
         + _sigmoid(g2_ref[...]) * _dot(oc_ref[...], wb_ref[2]))
    y = _dot(m, wo_ref[...])
    o_ref[...] = _layer_norm(ALPHA * x_ref[...] + y, ln_ref[0:1, :], ln_ref[1:2, :])


def _merge(oa, ob, oc, proj, x, w_branch, w_out, ln):
    t = x.shape[0]
    tm = min(ROW_TILE, t)
    br = pl.BlockSpec((tm, BR_W), lambda i: (i, 0))

    def gate(n):
        return pl.BlockSpec((tm, D_MODEL), lambda i: (i, n))

    return pl.pallas_call(
        _merge_body,
        grid=(t // tm,),
        in_specs=[br, br, br, gate(0), gate(1), gate(2),
                  pl.BlockSpec((tm, D_MODEL), lambda i: (i, 0)),
                  pl.BlockSpec(w_branch.shape, lambda i: (0, 0, 0)),
                  pl.BlockSpec(w_out.shape, lambda i: (0, 0)),
                  pl.BlockSpec((SUBLANES, D_MODEL), lambda i: (0, 0))],
        out_specs=pl.BlockSpec((tm, D_MODEL), lambda i: (i, 0)),
        out_shape=jax.ShapeDtypeStruct((t, D_MODEL), F32),
        compiler_params=_cparams(1),
        name="merge",
    )(oa, ob, oc, proj, proj, proj, x, w_branch, w_out, ln)


def _attend_memory(q, mk, mv):
    outs = []
    for h in range(N_HEADS):
        s = _dot_nt(_head(q, h), _head(mk, h)) * HEAD_W ** -0.5
        p = jnp.exp(s - jnp.max(s, axis=-1, keepdims=True))
        p = p / jnp.sum(p, axis=-1, keepdims=True)
        outs.append(_dot(p, _head(mv, h)))
    return jnp.concatenate(outs, axis=1)


def _xattn_prompt_body(x_ref, mk_ref, mv_ref, wq_ref, wo_ref, ln_ref, o_ref):
    x = x_ref[...]
    o = _attend_memory(_dot(x, wq_ref[...]), mk_ref[0], mv_ref[0])
    o_ref[...] = _layer_norm(ALPHA * x + _dot(o, wo_ref[...]), ln_ref[0:1, :], ln_ref[1:2, :])


def _xattn_prompt(x, mkv, w_xq, w_xo, ln, batch, seq):
    tm = min(ROW_TILE, seq)
    per = seq // tm
    n_mem = mkv.shape[1]
    return pl.pallas_call(
        _xattn_prompt_body,
        grid=(batch, per),
        in_specs=[pl.BlockSpec((tm, D_MODEL), lambda b, i: (b * per + i, 0)),
                  pl.BlockSpec((1, n_mem, BR_W), lambda b, i: (b, 0, 0)),
                  pl.BlockSpec((1, n_mem, BR_W), lambda b, i: (b, 0, 1)),
                  pl.BlockSpec(w_xq.shape, lambda b, i: (0, 0)),
                  pl.BlockSpec(w_xo.shape, lambda b, i: (0, 0)),
                  pl.BlockSpec((SUBLANES, D_MODEL), lambda b, i: (0, 0))],
        out_specs=pl.BlockSpec((tm, D_MODEL), lambda b, i: (b * per + i, 0)),
        out_shape=jax.ShapeDtypeStruct(x.shape, F32),
        compiler_params=_cparams(2),
        name="xattn_prompt",
    )(x, mkv, mkv, w_xq, w_xo, ln)


def _xattn_sample_body(x_ref, mk_ref, mv_ref, wq_ref, wo_ref, ln_ref, o_ref, att_ref, *, n_seq):
    x = x_ref[...]
    q = _dot(x, wq_ref[...])
    for i in range(n_seq):
        rs = slice(i * SAMPLE_PAD, (i + 1) * SAMPLE_PAD)
        att_ref[rs, :] = _attend_memory(q[rs], mk_ref[i], mv_ref[i])
    o_ref[...] = _layer_norm(ALPHA * x + _dot(att_ref[...], wo_ref[...]), ln_ref[0:1, :], ln_ref[1:2, :])


def _xattn_sample(x, mem_k, mem_v, layer, w_xq, w_xo, ln, seq_per_step=8):
    _, n_batch, n_mem, _ = mem_k.shape
    rows = seq_per_step * SAMPLE_PAD
    mem = pl.BlockSpec((None, seq_per_step, n_mem, BR_W), lambda g: (layer, g, 0, 0))
    return pl.pallas_call(
        functools.partial(_xattn_sample_body, n_seq=seq_per_step),
        grid=(n_batch // seq_per_step,),
        in_specs=[pl.BlockSpec((rows, D_MODEL), lambda g: (g, 0)), mem, mem,
                  pl.BlockSpec(w_xq.shape, lambda g: (0, 0)),
                  pl.BlockSpec(w_xo.shape, lambda g: (0, 0)),
                  pl.BlockSpec((SUBLANES, D_MODEL), lambda g: (0, 0))],
        out_specs=pl.BlockSpec((rows, D_MODEL), lambda g: (g, 0)),
        out_shape=jax.ShapeDtypeStruct(x.shape, F32),
        scratch_shapes=[pltpu.VMEM((rows, BR_W), F32)],
        compiler_params=_cparams(1),
        name="xattn_sample",
    )(x, mem_k, mem_v, w_xq, w_xo, ln)


def _ffn_body(x_ref, wg_ref, wu_ref, w2_ref, ln_ref, o_ref, acc_ref, *, n_f):
    f = pl.program_id(1)
    x = x_ref[...]
    xb = x.astype(BF16)
    gate = jnp.dot(xb, wg_ref[...], preferred_element_type=F32)
    up = jnp.dot(xb, wu_ref[...], preferred_element_type=F32)
    part = _dot(_silu(gate) * up, w2_ref[...])

    @pl.when(f == 0)
    def _():
        acc_ref[...] = part

    @pl.when(f > 0)
    def _():
        acc_ref[...] += part

    @pl.when(f == n_f - 1)
    def _():
        o_ref[...] = _layer_norm(ALPHA * x + acc_ref[...], ln_ref[0:1, :], ln_ref[1:2, :])


def _ffn(x, w1, w2, ln, tf):
    t = x.shape[0]
    tm = min(ROW_TILE, t)
    n_f = D_FF // tf
    return pl.pallas_call(
        functools.partial(_ffn_body, n_f=n_f),
        grid=(t // tm, n_f),
        in_specs=[pl.BlockSpec((tm, D_MODEL), lambda i, f: (i, 0)),
                  pl.BlockSpec((D_MODEL, tf), lambda i, f: (0, f)),
                  pl.BlockSpec((D_MODEL, tf), lambda i, f: (0, n_f + f)),
                  pl.BlockSpec((tf, D_MODEL), lambda i, f: (f, 0)),
                  pl.BlockSpec((SUBLANES, D_MODEL), lambda i, f: (0, 0))],
        out_specs=pl.BlockSpec((tm, D_MODEL), lambda i, f: (i, 0)),
        out_shape=jax.ShapeDtypeStruct(x.shape, F32),
        scratch_shapes=[pltpu.VMEM((tm, D_MODEL), F32)],
        compiler_params=_cparams(2),
        name="ffn",
    )(x, w1, w1, w2, ln)


def _split_w_in(w):
    ab0 = 8 * BR_W
    c0 = ab0 + 2 * N_HEADS
    g0 = c0 + 3 * BR_W
    pad = jnp.zeros((w.shape[0], LANES - 2 * N_HEADS), w.dtype)
    main = jnp.concatenate([w[:, g0:], w[:, :ab0], w[:, c0:c0 + BR_W]], axis=1)
    kv = jnp.concatenate([w[:, c0 + BR_W:g0], w[:, ab0:c0], pad], axis=1)
    return main.astype(BF16), kv.astype(BF16)


def _ln_rows(g, b):
    return jnp.concatenate([g[None], b[None], jnp.zeros((SUBLANES - 2, g.shape[0]), F32)], axis=0)


def kernel(x_prompt, x_sample, mem_prompt, cache_attn_k, cache_attn_v, cache_mem_k, cache_mem_v, state_hgrn,
           state_gdn, state_gdn_conv, page_table, w_in, w_branch, w_out, lower_bounds, hgrn_norm, gdn_a_log,
           gdn_dt_bias, gdn_conv_w, gdn_norm, diff_lambda, diff_norm, rel_bias, w_xq, w_mem_kv, w_xo, ln_g, ln_b,
           w_ffn_in, w_ffn_out):
    bp, seq, _ = x_prompt.shape
    bs, ls, _ = x_sample.shape
    n_mem = mem_prompt.shape[1]
    n_pages = page_table.shape[1]
    n_past = n_pages * PAGE_SIZE
    n_pool = cache_attn_k.shape[1]
    pad = SAMPLE_PAD - ls

    lb_cum = jnp.cumsum(jax.nn.softmax(lower_bounds.astype(F32), axis=0), axis=0)
    lb = (lb_cum - lb_cum[0]).reshape(DEPTH, N_HEADS, 1, HEAD_W)
    lbp = jnp.concatenate([jnp.log(lb), jnp.log1p(-lb), 1.0 - lb,
                           jnp.zeros((DEPTH, N_HEADS, SUBLANES - 3, HEAD_W), F32)], axis=2)

    n_max = max(2 * ATT_BLOCK + 2, n_past + ls)
    bucket = t5_bucket(-jnp.arange(n_max, dtype=jnp.int32))
    thr = jnp.searchsorted(bucket, jnp.arange(N_BUCKETS, dtype=jnp.int32), side="left").astype(jnp.int32)
    rel_bias_t = rel_bias.astype(F32).T
    tiles = _bias_tiles(thr, rel_bias_t)
    dist = rel_bias_t[:, bucket]
    far_bias = dist[:, ATT_BLOCK + 1]
    q_pos = n_past + jnp.arange(ls)
    past = dist[:, q_pos[:, None] - jnp.arange(n_past)[None, :]]
    bias_s = jnp.concatenate([past, past], axis=1)
    tt = jnp.arange(ls)[:, None] - jnp.arange(ls)[None, :]
    new = jnp.where(tt >= 0, dist[:, jnp.maximum(tt, 0)], NEG_INF)
    nbias = jnp.concatenate([new, new], axis=1).transpose(0, 2, 1)[..., None]

    xp = x_prompt.reshape(bp * seq, D_MODEL)
    xs = jnp.concatenate([jnp.zeros((bs, pad, D_MODEL), F32), x_sample], axis=1).reshape(bs * SAMPLE_PAD, D_MODEL)
    mem = mem_prompt.reshape(bp * n_mem, D_MODEL)
    cache_k = cache_attn_k.reshape(DEPTH, n_pool, PAGE_SIZE, BR_W)
    cache_v = cache_attn_v.reshape(DEPTH, n_pool, PAGE_SIZE, BR_W)
    mem_k = cache_mem_k.reshape(DEPTH, bs, n_mem, BR_W)
    mem_v = cache_mem_v.reshape(DEPTH, bs, n_mem, BR_W)

    pk = pv = jnp.zeros((DEPTH, bp * seq, BR_W), F32)
    sk = sv = jnp.zeros((DEPTH, bs * SAMPLE_PAD, BR_W), F32)
    sh = sg = jnp.zeros(state_hgrn.shape, F32)
    pmk, pmv, ph, pg, pc, sc = [], [], [], [], [], []
    b_col0 = BLK_B * BR_W

    for l in range(DEPTH):
        w_main, w_kv = _split_w_in(w_in[l])
        wb_l = w_branch[l].astype(BF16)
        wo_l = w_out[l].astype(BF16)
        wq_l = w_xq[l].astype(BF16)
        wxo_l = w_xo[l].astype(BF16)
        wkv_l = w_mem_kv[l].astype(BF16)
        w1_l = w_ffn_in[l].astype(BF16)
        w2_l = w_ffn_out[l].astype(BF16)
        lns = [_ln_rows(ln_g[l, i], ln_b[l, i]) for i in range(3)]
        hn = hgrn_norm[l][None]
        gn = gdn_norm[l][None]
        dn = diff_norm[l][None]
        head_params = jnp.stack([jnp.exp(gdn_a_log[l].astype(F32)), gdn_dt_bias[l].astype(F32)])
        lam_init = 0.8 - 0.6 * math.exp(-0.3 * l)
        lq1, lk1, lq2, lk2 = diff_lambda[l].astype(F32)
        lam = jnp.exp(jnp.sum(lq1 * lk1)) - jnp.exp(jnp.sum(lq2 * lk2)) + lam_init
        scalars = jnp.stack([jnp.stack([lam, jnp.float32(1.0 - lam_init), jnp.float32(0), jnp.float32(0)]),
                             far_bias])
        conv_w_l = gdn_conv_w[l]

        proj = _matmul(xp, w_main, ROW_TILE, MAIN_W // 3)
        pk, pv, ab = _kv_proj(xp, w_kv, l, pk, pv)
        oa, h_new = _hgrn_prompt(proj, lbp[l], hn, bp, seq)
        qkv = _gdn_conv(proj, BLK_B * N_HEADS, conv_w_l, SEQ_TILE, seq)
        ob, g_new = _gdn_prompt(qkv, proj, ab, head_params, gn, bp, seq)
        oc = _diff_prompt(proj, pk, pv, l, scalars, tiles, dn, bp, seq)
        x1 = _merge(oa, ob, oc, proj, xp, wb_l, wo_l, lns[0])
        mkv = _matmul(mem, wkv_l, ROW_TILE, 2 * BR_W).reshape(bp, n_mem, 2 * BR_W)
        x2 = _xattn_prompt(x1, mkv, wq_l, wxo_l, lns[1], bp, seq)
        xp = _ffn(x2, w1_l, w2_l, lns[2], D_FF // 2)
        pmk.append(mkv[:, :, :BR_W].reshape(bp, n_mem, N_HEADS, HEAD_W))
        pmv.append(mkv[:, :, BR_W:].reshape(bp, n_mem, N_HEADS, HEAD_W))
        ph.append(h_new)
        pg.append(g_new)
        pc.append(proj.reshape(bp, seq, MAIN_W)[:, seq - (CONV_W - 1):, b_col0:b_col0 + CONV_CH])

        proj_s = _matmul(xs, w_main, ROW_TILE, MAIN_W // 3)
        sk, sv, ab_s = _kv_proj(xs, w_kv, l, sk, sv)
        oa, sh = _hgrn_sample(proj_s, lbp[l], hn, state_hgrn, l, sh, ls)
        conv_tok = proj_s[:, b_col0:b_col0 + CONV_CH].reshape(bs, SAMPLE_PAD, CONV_CH)
        conv_in = jnp.concatenate([conv_tok[:, :pad - (CONV_W - 1)], state_gdn_conv[l], conv_tok[:, pad:]], axis=1)
        qkv = _gdn_conv(conv_in.reshape(bs * SAMPLE_PAD, CONV_CH), 0, conv_w_l, bs * SAMPLE_PAD, SAMPLE_PAD)
        ob, sg = _gdn_sample(qkv, proj_s, ab_s, head_params, gn, state_gdn, l, sg, ls)
        oc = _diff_sample(page_table, scalars, proj_s, sk, sv, cache_k, cache_v, l, bias_s, nbias, dn, ls)
        x1 = _merge(oa, ob, oc, proj_s, xs, wb_l, wo_l, lns[0])
        x2 = _xattn_sample(x1, mem_k, mem_v, l, wq_l, wxo_l, lns[1])
        xs = _ffn(x2, w1_l, w2_l, lns[2], D_FF // 2)
        sc.append(conv_in[:, SAMPLE_PAD - (CONV_W - 1):])

    y_prompt = xp.reshape(bp, seq, D_MODEL)
    y_sample = xs.reshape(bs, SAMPLE_PAD, D_MODEL)[:, pad:]
    new_k_p = pk.reshape(DEPTH, bp, seq, N_HEADS, HEAD_W)
    new_v_p = pv.reshape(DEPTH, bp, seq, N_HEADS, HEAD_W)
    new_k_s = sk.reshape(DEPTH, bs, SAMPLE_PAD, N_HEADS, HEAD_W)[:, :, pad:]
    new_v_s = sv.reshape(DEPTH, bs, SAMPLE_PAD, N_HEADS, HEAD_W)[:, :, pad:]
    return (y_prompt, y_sample, new_k_p, new_v_p, jnp.stack(pmk), jnp.stack(pmv), jnp.stack(ph), jnp.stack(pg),
            jnp.stack(pc), new_k_s, new_v_s, sh, sg, jnp.stack(sc))
```

```python
import functools
import math

import jax
import jax.numpy as jnp
from jax import lax
from jax.experimental import pallas as pl
from jax.experimental.pallas import tpu as pltpu

F32 = jnp.float32
BF16 = jnp.bfloat16

D_MODEL = 1024
DEPTH = 4
PAGE_SIZE = 128
BR_W = D_MODEL // 2
N_HEADS = 4
HEAD_W = BR_W // N_HEADS
DH_C = HEAD_W // 2
CONV_W = 4
CONV_CH = 3 * BR_W
N_BRANCH = 3
N_BUCKETS = 32
MAX_DISTANCE = 128
D_FF = -(-8 * D_MODEL // (3 * 256)) * 256
LN_EPS = 1e-5
NORM_EPS = 1e-6
NEG_INF = -1e30
ALPHA = (2 * DEPTH) ** 0.25

LANES = 128
SUBLANES = 8
VMEM_LIMIT_BYTES = 56 * 1024 * 1024

BLK_A = N_BRANCH * D_MODEL // BR_W
BLK_B = BLK_A + 4
BLK_CQ = BLK_B + 4
MAIN_W = (BLK_CQ + 1) * BR_W
KV_W = 2 * BR_W + LANES

HGRN_SUB = 16
GDN_CHUNK = 64
SOLVE_BLOCK = 16
SAMPLE_PAD = 8
SEQ_TILE = 512
ATT_BLOCK = 512
ATT_HEADS = 2
ROW_TILE = 512


def _cparams(n_axes):
    return pltpu.CompilerParams(dimension_semantics=("arbitrary",) * n_axes,
                                vmem_limit_bytes=VMEM_LIMIT_BYTES)


def _dot(a, b):
    return jnp.dot(a.astype(BF16), b.astype(BF16), preferred_element_type=F32)


def _dot_nt(a, b):
    return lax.dot_general(a.astype(BF16), b.astype(BF16), (((1,), (1,)), ((), ())),
                           preferred_element_type=F32)


def _dot_tn(a, b):
    return lax.dot_general(a.astype(BF16), b.astype(BF16), (((0,), (0,)), ((), ())),
                           preferred_element_type=F32)


def _dot_exact_lhs(m01, x):
    hi = x.astype(BF16)
    r1 = x - hi.astype(F32)
    mid = r1.astype(BF16)
    lo = (r1 - mid.astype(F32)).astype(BF16)
    dot = functools.partial(jnp.dot, preferred_element_type=F32)
    return dot(m01, hi) + dot(m01, mid) + dot(m01, lo)


def _sigmoid(x):
    return 1.0 / (1.0 + jnp.exp(-x))


def _silu(x):
    return x * _sigmoid(x)


def _softplus(x):
    return jnp.maximum(x, 0.0) + jnp.log1p(jnp.exp(-jnp.abs(x)))


def _layer_norm(h, g, b):
    mu = jnp.mean(h, axis=-1, keepdims=True)
    c = h - mu
    var = jnp.mean(c * c, axis=-1, keepdims=True)
    return c * lax.rsqrt(var + LN_EPS) * g + b


def _rms_norm(x, g):
    return x * lax.rsqrt(jnp.mean(x * x, axis=-1, keepdims=True) + NORM_EPS) * g


def _bcast_rows(x, rows, reps):
    return jnp.concatenate([jnp.broadcast_to(x[r:r + 1, :], (reps, x.shape[1])) for r in rows], axis=0)


def _head(x, h):
    return x[:, h * HEAD_W:(h + 1) * HEAD_W]


ALIASED = pl.BlockSpec(memory_space=pl.ANY)


def _mm_body(x_ref, w_ref, o_ref):
    o_ref[...] = _dot(x_ref[...], w_ref[...])


def _matmul(x, w, tm, tn):
    t, k = x.shape
    n = w.shape[1]
    tm = min(tm, t)
    return pl.pallas_call(
        _mm_body,
        grid=(n // tn, t // tm),
        in_specs=[pl.BlockSpec((tm, k), lambda j, i: (i, 0)),
                  pl.BlockSpec((k, tn), lambda j, i: (0, j))],
        out_specs=pl.BlockSpec((tm, tn), lambda j, i: (i, j)),
        out_shape=jax.ShapeDtypeStruct((t, n), F32),
        compiler_params=_cparams(2),
        name="proj_matmul",
    )(x, w)


def _kv_body(x_ref, w_ref, *refs):
    k_ref, v_ref, ab_ref = refs[-3:]
    y = _dot(x_ref[...], w_ref[...])
    k_ref[...] = y[:, :BR_W]
    v_ref[...] = y[:, BR_W:2 * BR_W]
    ab_ref[...] = y[:, 2 * BR_W:]


def _kv_proj(x, w, layer, k_prev, v_prev):
    t = x.shape[0]
    tm = min(ROW_TILE, t)
    slab = pl.BlockSpec((None, tm, BR_W), lambda i: (layer, i, 0))
    stacked = jax.ShapeDtypeStruct((DEPTH, t, BR_W), F32)
    return pl.pallas_call(
        _kv_body,
        grid=(t // tm,),
        in_specs=[pl.BlockSpec((tm, D_MODEL), lambda i: (i, 0)),
                  pl.BlockSpec(w.shape, lambda i: (0, 0)), ALIASED, ALIASED],
        out_specs=[slab, slab, pl.BlockSpec((tm, LANES), lambda i: (i, 0))],
        out_shape=[stacked, stacked, jax.ShapeDtypeStruct((t, LANES), F32)],
        input_output_aliases={2: 0, 3: 1},
        compiler_params=_cparams(1),
        name="kv_proj",
    )(x, w, k_prev, v_prev)


def _hgrn_rows(q, z, v, lbp, sub, tri, pos, valid):
    r = q.shape[0]
    n_sub = r // sub
    log_lb, log1m_lb, om_lb = lbp[0:1, :], lbp[1:2, :], lbp[2:3, :]
    log_sig = jnp.minimum(z, 0.0) - jnp.log(1.0 + jnp.exp(-jnp.abs(z)))
    t2 = log1m_lb + log_sig
    logf = jnp.maximum(log_lb, t2) + jnp.log(1.0 + jnp.exp(-jnp.abs(log_lb - t2)))
    k = om_lb * (1.0 / (1.0 + jnp.exp(z)))
    if valid is not None:
        logf = jnp.where(valid, logf, 0.0)
        k = jnp.where(valid, k, 0.0)
    bc = _dot_exact_lhs(tri, logf)
    last_rows = [i * sub + sub - 1 for i in range(n_sub)]
    b_last = _bcast_rows(bc, last_rows, sub)
    q_hat = q * jnp.exp(bc)
    k_hat = k * jnp.exp(b_last - bc)
    o = jnp.zeros_like(q)
    for s in range(sub):
        rows = [i * sub + s for i in range(n_sub)]
        ks = _bcast_rows(k, rows, sub)
        bs = _bcast_rows(bc, rows, sub)
        vs = _bcast_rows(v, rows, sub)
        a = q * ks * jnp.exp(jnp.minimum(bc - bs, 0.0))
        a = jnp.where(pos >= s, a, 0.0)
        o = o + jnp.sum(a, axis=-1, keepdims=True) * vs
    return q_hat, k_hat, o, bc


def _sub_chunk_masks(rows, sub):
    ri = lax.broadcasted_iota(jnp.int32, (rows, rows), 0)
    ci = lax.broadcasted_iota(jnp.int32, (rows, rows), 1)
    tri = ((ri // sub == ci // sub) & (ci <= ri)).astype(BF16)
    pos = lax.broadcasted_iota(jnp.int32, (rows, 1), 0) % sub
    return tri, pos


def _hgrn_finish(o, g, norm_w):
    return _rms_norm(o, norm_w) * _silu(g)


def _hgrn_prompt_body(q_ref, f_ref, i_ref, g_ref, lbp_ref, nw_ref, o_ref, s_out_ref, st_ref, *, n_tiles):
    t = pl.program_id(1)

    @pl.when(t == 0)
    def _():
        st_ref[...] = jnp.zeros_like(st_ref)

    rows = 4 * HGRN_SUB
    nw = nw_ref[...]
    tri, pos = _sub_chunk_masks(rows, HGRN_SUB)

    def chunk(c, carry):
        sl = pl.ds(pl.multiple_of(c * rows, rows), rows)
        qa, za, va, ga = q_ref[sl, :], f_ref[sl, :], i_ref[sl, :], g_ref[sl, :]
        hs = range(N_HEADS)
        v = [_head(va, h) for h in hs]
        parts = [_hgrn_rows(_head(qa, h), _head(za, h), v[h], lbp_ref[h], HGRN_SUB, tri, pos, None) for h in hs]
        st = [st_ref[h] for h in hs]
        outs = [[] for _ in hs]
        for i in range(rows // HGRN_SUB):
            lo, hi = i * HGRN_SUB, (i + 1) * HGRN_SUB
            for h in hs:
                q_hat, k_hat, o, bc = parts[h]
                outs[h].append(o[lo:hi] + _dot_nt(q_hat[lo:hi], st[h]))
                st[h] = st[h] * jnp.exp(bc[hi - 1:hi, :]) + _dot_tn(v[h][lo:hi], k_hat[lo:hi])
        for h in hs:
            st_ref[h] = st[h]
        o_ref[sl, :] = jnp.concatenate(
            [_hgrn_finish(jnp.concatenate(outs[h], axis=0), _head(ga, h), nw) for h in hs], axis=1)
        return carry

    lax.fori_loop(0, SEQ_TILE // rows, chunk, 0)

    @pl.when(t == n_tiles - 1)
    def _():
        for h in range(N_HEADS):
            s_out_ref[0, h] = st_ref[h].T


def _hgrn_prompt(proj, lbp, norm_w, batch, seq):
    n_tiles = seq // SEQ_TILE

    def col(blk):
        return pl.BlockSpec((SEQ_TILE, BR_W), lambda b, t: (b * n_tiles + t, blk))

    return pl.pallas_call(
        functools.partial(_hgrn_prompt_body, n_tiles=n_tiles),
        grid=(batch, n_tiles),
        in_specs=[col(BLK_A), col(BLK_A + 1), col(BLK_A + 2), col(BLK_A + 3),
                  pl.BlockSpec((N_HEADS, SUBLANES, HEAD_W), lambda b, t: (0, 0, 0)),
                  pl.BlockSpec((1, HEAD_W), lambda b, t: (0, 0))],
        out_specs=[pl.BlockSpec((SEQ_TILE, BR_W), lambda b, t: (b * n_tiles + t, 0)),
                   pl.BlockSpec((1, N_HEADS, HEAD_W, HEAD_W), lambda b, t: (b, 0, 0, 0))],
        out_shape=[jax.ShapeDtypeStruct((batch * seq, BR_W), F32),
                   jax.ShapeDtypeStruct((batch, N_HEADS, HEAD_W, HEAD_W), F32)],
        scratch_shapes=[pltpu.VMEM((N_HEADS, HEAD_W, HEAD_W), F32)],
        compiler_params=_cparams(2),
        name="hgrn_prompt",
    )(proj, proj, proj, proj, lbp, norm_w)


def _hgrn_sample_body(q_ref, f_ref, i_ref, g_ref, lbp_ref, nw_ref, s0_ref, *refs, n_seq, n_real):
    o_ref, s_out_ref = refs[-2:]
    nw = nw_ref[...]
    rows = n_seq * SAMPLE_PAD
    tri, pos = _sub_chunk_masks(rows, SAMPLE_PAD)
    valid = pos >= SAMPLE_PAD - n_real
    qa, za, va, ga = q_ref[...], f_ref[...], i_ref[...], g_ref[...]
    heads = []
    for h in range(N_HEADS):
        v = _head(va, h)
        q_hat, k_hat, o, bc = _hgrn_rows(_head(qa, h), _head(za, h), v, lbp_ref[h], SAMPLE_PAD, tri, pos, valid)
        outs = []
        for i in range(n_seq):
            lo, hi = i * SAMPLE_PAD, (i + 1) * SAMPLE_PAD
            s0 = s0_ref[i, h]
            outs.append(o[lo:hi] + _dot(q_hat[lo:hi], s0))
            decay = jnp.exp(bc[hi - 1:hi, :])
            s_t = s0.T * decay + _dot_tn(v[lo:hi], k_hat[lo:hi])
            s_out_ref[i, h] = s_t.T
        heads.append(_hgrn_finish(jnp.concatenate(outs, axis=0), _head(ga, h), nw))
    o_ref[...] = jnp.concatenate(heads, axis=1)


def _state_specs(layer, seq_per_step):
    return pl.BlockSpec((None, seq_per_step, N_HEADS, HEAD_W, HEAD_W), lambda g: (layer, g, 0, 0, 0))


def _hgrn_sample(proj, lbp, norm_w, states, layer, prev, n_real, seq_per_step=8):
    n_batch = states.shape[1]
    rows = seq_per_step * SAMPLE_PAD

    def col(blk):
        return pl.BlockSpec((rows, BR_W), lambda g: (g, blk))

    return pl.pallas_call(
        functools.partial(_hgrn_sample_body, n_seq=seq_per_step, n_real=n_real),
        grid=(n_batch // seq_per_step,),
        in_specs=[col(BLK_A), col(BLK_A + 1), col(BLK_A + 2), col(BLK_A + 3),
                  pl.BlockSpec((N_HEADS, SUBLANES, HEAD_W), lambda g: (0, 0, 0)),
                  pl.BlockSpec((1, HEAD_W), lambda g: (0, 0)),
                  _state_specs(layer, seq_per_step), ALIASED],
        out_specs=[pl.BlockSpec((rows, BR_W), lambda g: (g, 0)), _state_specs(layer, seq_per_step)],
        out_shape=[jax.ShapeDtypeStruct((n_batch * SAMPLE_PAD, BR_W), F32),
                   jax.ShapeDtypeStruct(states.shape, F32)],
        input_output_aliases={7: 1},
        compiler_params=_cparams(1),
        name="hgrn_sample",
    )(proj, proj, proj, proj, lbp, norm_w, states, prev)


def _conv_body(prev_ref, cur_ref, w_ref, o_ref, buf_ref, *, tile, seq_len):
    i = pl.program_id(0)
    c = pl.program_id(1)
    buf_ref[0:SUBLANES, :] = prev_ref[...]
    buf_ref[SUBLANES:, :] = cur_ref[...]
    pos = (i * tile + lax.broadcasted_iota(jnp.int32, (tile, 1), 0)) % seq_len
    w = w_ref[...]
    acc = cur_ref[...] * w[CONV_W - 1:CONV_W, :]
    for back in range(1, CONV_W):
        shifted = buf_ref[pl.ds(SUBLANES - back, tile), :]
        acc = acc + jnp.where(pos >= back, shifted, 0.0) * w[CONV_W - 1 - back:CONV_W - back, :]
    act = _silu(acc)
    inv = lax.rsqrt(jnp.sum(act * act, axis=-1, keepdims=True) + NORM_EPS)
    scale = jnp.where(c < N_HEADS, inv * HEAD_W ** -0.5, jnp.where(c < 2 * N_HEADS, inv, 1.0))
    o_ref[...] = act * scale


def _gdn_conv(conv_in, col0, conv_w, tile, seq_len):
    t = conv_in.shape[0]
    per = tile // SUBLANES
    return pl.pallas_call(
        functools.partial(_conv_body, tile=tile, seq_len=seq_len),
        grid=(t // tile, CONV_CH // HEAD_W),
        in_specs=[pl.BlockSpec((SUBLANES, HEAD_W), lambda i, c: (jnp.maximum(i * per - 1, 0), col0 + c)),
                  pl.BlockSpec((tile, HEAD_W), lambda i, c: (i, col0 + c)),
                  pl.BlockSpec((CONV_W, HEAD_W), lambda i, c: (0, c))],
        out_specs=pl.BlockSpec((tile, HEAD_W), lambda i, c: (i, c)),
        out_shape=jax.ShapeDtypeStruct((t, CONV_CH), F32),
        scratch_shapes=[pltpu.VMEM((tile + SUBLANES, HEAD_W), F32)],
        compiler_params=_cparams(2),
        name="gdn_conv",
    )(conv_in, conv_in, conv_w)


def _dot_f32(a, b):
    ah = a.astype(BF16)
    al = (a - ah.astype(F32)).astype(BF16)
    bh = b.astype(BF16)
    bl = (b - bh.astype(F32)).astype(BF16)
    dot = functools.partial(jnp.dot, preferred_element_type=F32)
    return dot(ah, bh) + dot(ah, bl) + dot(al, bh)


def _forward_substitute(lmats, rhss):
    n_sys = len(lmats)
    c = lmats[0].shape[0]
    nb = min(c, SOLVE_BLOCK)
    done = [[] for _ in range(n_sys)]
    for i0 in range(0, c, nb):
        xbs, lbs = [], []
        for i in range(n_sys):
            r = rhss[i][i0:i0 + nb, :]
            if i0:
                r = r - _dot_f32(lmats[i][i0:i0 + nb, :i0], jnp.concatenate(done[i], axis=0))
            xbs.append([r[s:s + SUBLANES, :] for s in range(0, nb, SUBLANES)])
            lbs.append([lmats[i][i0 + s:i0 + s + SUBLANES, :] for s in range(0, nb, SUBLANES)])
        for j in range(nb - 1):
            for i in range(n_sys):
                xb = xbs[i]
                xj = xb[j // SUBLANES][j % SUBLANES:j % SUBLANES + 1, :]
                for b in range((j + 1) // SUBLANES, len(xb)):
                    xb[b] = xb[b] - lbs[i][b][:, i0 + j:i0 + j + 1] * xj
        for i in range(n_sys):
            done[i].extend(xbs[i])
    return [jnp.concatenate(d, axis=0) if len(d) > 1 else d[0] for d in done]


def _gdn_masks(c):
    ri = lax.broadcasted_iota(jnp.int32, (c, c), 0)
    ci = lax.broadcasted_iota(jnp.int32, (c, c), 1)
    mr = lax.broadcasted_iota(jnp.int32, (c, LANES), 0)
    mc = lax.broadcasted_iota(jnp.int32, (c, LANES), 1)
    return ci <= ri, ci < ri, ((mc < mr) & (mc < c)) | (mc == c)


def _gdn_heads(hp_ref, qa, ka, va, ga, ab, nw, get_state, put_state, masks, valid):
    c = qa.shape[0]
    incl, strict, sel = masks
    hs = range(N_HEADS)
    lane = lax.broadcasted_iota(jnp.int32, ab.shape, 1)
    q = [_head(qa, h) for h in hs]
    k = [_head(ka, h) for h in hs]
    g, beta = [], []
    for h in hs:
        a_col = jnp.sum(jnp.where(lane == h, ab, 0.0), axis=-1, keepdims=True)
        b_col = jnp.sum(jnp.where(lane == h + N_HEADS, ab, 0.0), axis=-1, keepdims=True)
        g_h = -hp_ref[0, h] * _softplus(a_col + hp_ref[1, h])
        beta_h = _sigmoid(b_col)
        if valid is not None:
            g_h = jnp.where(valid, g_h, 0.0)
            beta_h = jnp.where(valid, beta_h, 0.0)
        g.append(g_h)
        beta.append(beta_h)
    incl_b = incl.astype(BF16)
    m = [_dot_exact_lhs(incl_b, jnp.where(sel, g[h], 0.0)) for h in hs]
    gcum = [m[h][:, c:c + 1] for h in hs]
    decay = [jnp.where(incl, jnp.exp(jnp.where(incl, m[h][:, :c], 0.0)), 0.0) for h in hs]
    kb = [k[h] * beta[h] for h in hs]
    lmat = [jnp.where(strict, _dot_nt(kb[h], k[h]) * decay[h], 0.0) for h in hs]
    e_g = [jnp.exp(gcum[h]) for h in hs]
    sol = _forward_substitute(lmat, [jnp.concatenate([_head(va, h) * beta[h], kb[h] * e_g[h]], axis=1) for h in hs])
    s = [get_state(h) for h in hs]
    v_new = [sol[h][:, :HEAD_W] - _dot(sol[h][:, HEAD_W:], s[h]) for h in hs]
    attn = [jnp.where(incl, _dot_nt(q[h], k[h]) * decay[h], 0.0) for h in hs]
    o = [_dot(q[h] * e_g[h], s[h]) + _dot(attn[h], v_new[h]) for h in hs]
    for h in hs:
        g_last = gcum[h][c - 1:c, :]
        put_state(h, s[h] * jnp.exp(g_last) + _dot_tn(k[h] * jnp.exp(g_last - gcum[h]), v_new[h]))
    return jnp.concatenate([_rms_norm(o[h], nw) * _silu(_head(ga, h)) for h in hs], axis=1)


def _gdn_prompt_body(hp_ref, q_ref, k_ref, v_ref, g_ref, ab_ref, nw_ref, o_ref, s_out_ref, s_ref, *, n_tiles):
    t = pl.program_id(1)

    @pl.when(t == 0)
    def _():
        s_ref[...] = jnp.zeros_like(s_ref)

    nw = nw_ref[...]
    masks = _gdn_masks(GDN_CHUNK)

    def put(h, s_new):
        s_ref[h] = s_new

    def chunk(c, carry):
        sl = pl.ds(pl.multiple_of(c * GDN_CHUNK, GDN_CHUNK), GDN_CHUNK)
        o_ref[sl, :] = _gdn_heads(hp_ref, q_ref[sl, :], k_ref[sl, :], v_ref[sl, :], g_ref[sl, :], ab_ref[sl, :],
                                  nw, lambda h: s_ref[h], put, masks, None)
        return carry

    lax.fori_loop(0, SEQ_TILE // GDN_CHUNK, chunk, 0)

    @pl.when(t == n_tiles - 1)
    def _():
        s_out_ref[0] = s_ref[...]


def _gdn_prompt(qkv, proj, ab, head_params, norm_w, batch, seq):
    n_tiles = seq // SEQ_TILE

    def col(blk):
        return pl.BlockSpec((SEQ_TILE, BR_W), lambda b, t: (b * n_tiles + t, blk))

    return pl.pallas_call(
        functools.partial(_gdn_prompt_body, n_tiles=n_tiles),
        grid=(batch, n_tiles),
        in_specs=[pl.BlockSpec(memory_space=pltpu.SMEM),
                  col(0), col(1), col(2), col(BLK_B + 3),
                  pl.BlockSpec((SEQ_TILE, LANES), lambda b, t: (b * n_tiles + t, 0)),
                  pl.BlockSpec((1, HEAD_W), lambda b, t: (0, 0))],
        out_specs=[pl.BlockSpec((SEQ_TILE, BR_W), lambda b, t: (b * n_tiles + t, 0)),
                   pl.BlockSpec((1, N_HEADS, HEAD_W, HEAD_W), lambda b, t: (b, 0, 0, 0))],
        out_shape=[jax.ShapeDtypeStruct((batch * seq, BR_W), F32),
                   jax.ShapeDtypeStruct((batch, N_HEADS, HEAD_W, HEAD_W), F32)],
        scratch_shapes=[pltpu.VMEM((N_HEADS, HEAD_W, HEAD_W), F32)],
        compiler_params=_cparams(2),
        name="gdn_prompt",
    )(head_params, qkv, qkv, qkv, proj, ab, norm_w)


def _gdn_sample_body(hp_ref, q_ref, k_ref, v_ref, g_ref, ab_ref, nw_ref, s0_ref, *refs, n_seq, n_real):
    o_ref, s_out_ref = refs[-2:]
    nw = nw_ref[...]
    masks = _gdn_masks(SAMPLE_PAD)
    valid = lax.broadcasted_iota(jnp.int32, (SAMPLE_PAD, 1), 0) >= SAMPLE_PAD - n_real

    def one(i, carry):
        sl = pl.ds(pl.multiple_of(i * SAMPLE_PAD, SAMPLE_PAD), SAMPLE_PAD)

        def put(h, s_new):
            s_out_ref[i, h] = s_new

        o_ref[sl, :] = _gdn_heads(hp_ref, q_ref[sl, :], k_ref[sl, :], v_ref[sl, :], g_ref[sl, :], ab_ref[sl, :],
                                  nw, lambda h: s0_ref[i, h], put, masks, valid)
        return carry

    lax.fori_loop(0, n_seq, one, 0)


def _gdn_sample(qkv, proj, ab, head_params, norm_w, states, layer, prev, n_real, seq_per_step=8):
    n_batch = states.shape[1]
    rows = seq_per_step * SAMPLE_PAD

    def col(blk):
        return pl.BlockSpec((rows, BR_W), lambda g: (g, blk))

    return pl.pallas_call(
        functools.partial(_gdn_sample_body, n_seq=seq_per_step, n_real=n_real),
        grid=(n_batch // seq_per_step,),
        in_specs=[pl.BlockSpec(memory_space=pltpu.SMEM),
                  col(0), col(1), col(2), col(BLK_B + 3),
                  pl.BlockSpec((rows, LANES), lambda g: (g, 0)),
                  pl.BlockSpec((1, HEAD_W), lambda g: (0, 0)),
                  _state_specs(layer, seq_per_step), ALIASED],
        out_specs=[pl.BlockSpec((rows, BR_W), lambda g: (g, 0)), _state_specs(layer, seq_per_step)],
        out_shape=[jax.ShapeDtypeStruct((n_batch * SAMPLE_PAD, BR_W), F32),
                   jax.ShapeDtypeStruct(states.shape, F32)],
        input_output_aliases={8: 1},
        compiler_params=_cparams(1),
        name="gdn_sample",
    )(head_params, qkv, qkv, qkv, proj, ab, norm_w, states, prev)


def t5_bucket(rel):
    n = jnp.maximum(-rel, 0)
    exact = N_BUCKETS // 2
    log_part = jnp.log(jnp.maximum(n, 1).astype(F32) / exact) / math.log(MAX_DISTANCE / exact)
    large = jnp.minimum(exact + (log_part * (N_BUCKETS - exact)).astype(jnp.int32), N_BUCKETS - 1)
    return jnp.where(n < exact, n, large)


def _bias_tile_body(thr_ref, rb_ref, o_ref):
    h = pl.program_id(0)
    blk = ATT_BLOCK
    row = lax.broadcasted_iota(jnp.int32, (blk, blk), 0)
    col = lax.broadcasted_iota(jnp.int32, (blk, blk), 1)
    for which in range(2):
        n = row - col + which * blk
        val = jnp.full((blk, blk), rb_ref[h, 0], F32)
        for b in range(1, N_BUCKETS):
            val = jnp.where(n >= thr_ref[b], rb_ref[h, b], val)
        if which == 0:
            val = jnp.where(n >= 0, val, NEG_INF)
        o_ref[which] = val


def _bias_tiles(thr, rel_bias_t):
    return pl.pallas_call(
        _bias_tile_body,
        grid=(N_HEADS,),
        in_specs=[pl.BlockSpec(memory_space=pltpu.SMEM), pl.BlockSpec(memory_space=pltpu.SMEM)],
        out_specs=pl.BlockSpec((None, 2, ATT_BLOCK, ATT_BLOCK), lambda h: (h, 0, 0, 0)),
        out_shape=jax.ShapeDtypeStruct((N_HEADS, 2, ATT_BLOCK, ATT_BLOCK), F32),
        compiler_params=_cparams(1),
        name="bias_tiles",
    )(thr, rel_bias_t)


def _stack_maps(q):
    lane = lax.broadcasted_iota(jnp.int32, q.shape, 1)
    return jnp.concatenate([jnp.where(lane < DH_C, q, 0.0), jnp.where(lane >= DH_C, q, 0.0)], axis=0)


def _diff_prompt_body(sc_ref, q_ref, k_ref, v_ref, d_ref, nw_ref, o_ref, k16_ref, v16_ref, m_ref, acc_ref):
    hp = pl.program_id(1)
    qi = pl.program_id(2)
    blk = ATT_BLOCK
    seq = k16_ref.shape[1]

    @pl.when(qi == 0)
    def _():
        for hh in range(ATT_HEADS):
            k16_ref[hh] = _head(k_ref[...], hh).astype(BF16)
            v16_ref[hh, :, :HEAD_W] = _head(v_ref[...], hh).astype(BF16)
            v16_ref[hh, :, HEAD_W:] = jnp.ones((seq, HEAD_W), BF16)

    lam = sc_ref[0, 0]
    out_scale = sc_ref[0, 1]
    qs = [_stack_maps(_head(q_ref[...], hh) * DH_C ** -0.5).astype(BF16) for hh in range(ATT_HEADS)]
    m_ref[...] = jnp.full_like(m_ref, -jnp.inf)
    acc_ref[...] = jnp.zeros_like(acc_ref)

    def block(kj, bias_of):
        sl = pl.ds(pl.multiple_of(kj * blk, blk), blk)
        for hh in range(ATT_HEADS):
            s = lax.dot_general(qs[hh], k16_ref[hh, sl, :], (((1,), (1,)), ((), ())), preferred_element_type=F32)
            bias = bias_of(hh)
            if bias.ndim == 2:
                s = (s.reshape(2, blk, blk) + bias[None]).reshape(2 * blk, blk)
            else:
                s = s + bias
            m_old = m_ref[hh]
            m_new = jnp.maximum(m_old, jnp.max(s, axis=-1, keepdims=True))
            alpha = jnp.exp(m_old - m_new)
            p = jnp.exp(s - jnp.concatenate([m_new] * (blk // LANES), axis=1))
            pv = jnp.dot(p.astype(BF16), v16_ref[hh, sl, :], preferred_element_type=F32)
            acc_ref[hh] = jnp.concatenate([alpha, alpha], axis=1) * acc_ref[hh] + pv
            m_ref[hh] = m_new

    def far(kj, carry):
        block(kj, lambda hh: sc_ref[1, hp * ATT_HEADS + hh])
        return carry

    lax.fori_loop(0, jnp.maximum(qi - 1, 0), far, 0)

    @pl.when(qi >= 1)
    def _():
        block(qi - 1, lambda hh: d_ref[hh, 1])

    block(qi, lambda hh: d_ref[hh, 0])
    outs = []
    for hh in range(ATT_HEADS):
        acc = acc_ref[hh]
        o = acc[:, :HEAD_W] / acc[:, HEAD_W:]
        o = o[:blk] - lam * o[blk:]
        outs.append(_rms_norm(o, nw_ref[...]) * out_scale)
    o_ref[...] = jnp.concatenate(outs, axis=1)


def _diff_prompt(proj, k_all, v_all, layer, scalars, tiles, norm_w, batch, seq):
    nq = seq // ATT_BLOCK
    width = ATT_HEADS * HEAD_W
    per_row = BR_W // width
    kv = pl.BlockSpec((None, seq, width), lambda b, h, q: (layer, b, h))
    return pl.pallas_call(
        _diff_prompt_body,
        grid=(batch, N_HEADS // ATT_HEADS, nq),
        in_specs=[pl.BlockSpec(memory_space=pltpu.SMEM),
                  pl.BlockSpec((ATT_BLOCK, width), lambda b, h, q: (b * nq + q, BLK_CQ * per_row + h)),
                  kv, kv,
                  pl.BlockSpec((ATT_HEADS, 2, ATT_BLOCK, ATT_BLOCK), lambda b, h, q: (h, 0, 0, 0)),
                  pl.BlockSpec((1, HEAD_W), lambda b, h, q: (0, 0))],
        out_specs=pl.BlockSpec((ATT_BLOCK, width), lambda b, h, q: (b * nq + q, h)),
        out_shape=jax.ShapeDtypeStruct((batch * seq, BR_W), F32),
        scratch_shapes=[pltpu.VMEM((ATT_HEADS, seq, HEAD_W), BF16), pltpu.VMEM((ATT_HEADS, seq, 2 * HEAD_W), BF16),
                        pltpu.VMEM((ATT_HEADS, 2 * ATT_BLOCK, LANES), F32),
                        pltpu.VMEM((ATT_HEADS, 2 * ATT_BLOCK, 2 * HEAD_W), F32)],
        compiler_params=_cparams(3),
        name="diff_prompt",
    )(scalars, proj, k_all, v_all, tiles, norm_w)


def _diff_sample_body(pt_ref, sc_ref, q_ref, kn_ref, vn_ref, ck_ref, cv_ref, bias_ref, nbias_ref, nw_ref,
                      o_ref, kbuf, vbuf, sem, *, layer, n_pages, n_real, n_batch):
    b = pl.program_id(0)
    slot = b % 2
    pad = SAMPLE_PAD - n_real

    def page_copies(seq_idx, sl):
        cps = []
        for j in range(n_pages):
            page = pt_ref[seq_idx, j]
            rows = pl.ds(j * PAGE_SIZE * N_HEADS, PAGE_SIZE * N_HEADS)
            cps.append(pltpu.make_async_copy(ck_ref.at[layer, page], kbuf.at[sl, rows], sem.at[sl, 0]))
            cps.append(pltpu.make_async_copy(cv_ref.at[layer, page], vbuf.at[sl, rows], sem.at[sl, 1]))
        return cps

    @pl.when(b == 0)
    def _():
        for cp in page_copies(0, 0):
            cp.start()

    @pl.when(b + 1 < n_batch)
    def _():
        for cp in page_copies(b + 1, 1 - slot):
            cp.start()

    for cp in page_copies(b, slot):
        cp.wait()

    lam = sc_ref[0, 0]
    out_scale = sc_ref[0, 1]
    outs = []
    for h in range(N_HEADS):
        hs = slice(h * HEAD_W, (h + 1) * HEAD_W)
        qs = _stack_maps(q_ref[pad:, hs] * DH_C ** -0.5)
        head_rows = pl.ds(h, n_pages * PAGE_SIZE, stride=N_HEADS)
        s = _dot_nt(qs, kbuf[slot, head_rows, :]) + bias_ref[h]
        kn = kn_ref[pad:, hs]
        vn = vn_ref[pad:, hs]
        s_new = [jnp.sum(qs * kn[t:t + 1, :], axis=-1, keepdims=True) + nbias_ref[h, t] for t in range(n_real)]
        m = jnp.max(s, axis=-1, keepdims=True)
        for sn in s_new:
            m = jnp.maximum(m, sn)
        p = jnp.exp(s - m)
        l = jnp.sum(p, axis=-1, keepdims=True)
        acc = _dot(p, vbuf[slot, head_rows, :])
        for t, sn in enumerate(s_new):
            pn = jnp.exp(sn - m)
            l = l + pn
            acc = acc + pn * vn[t:t + 1, :]
        o = acc / l
        o = o[:n_real] - lam * o[n_real:]
        outs.append(_rms_norm(o, nw_ref[...]) * out_scale)
    o_ref[0:pad, :] = jnp.zeros((pad, BR_W), F32)
    o_ref[pad:, :] = jnp.concatenate(outs, axis=1)


def _diff_sample(page_table, scalars, proj, k_new, v_new, cache_k, cache_v, layer, bias, nbias, norm_w, n_real):
    n_batch, n_pages = page_table.shape
    n_past = n_pages * PAGE_SIZE
    new_tok = pl.BlockSpec((None, SAMPLE_PAD, BR_W), lambda b, pt: (layer, b, 0))
    grid_spec = pltpu.PrefetchScalarGridSpec(
        num_scalar_prefetch=1,
        grid=(n_batch,),
        in_specs=[pl.BlockSpec(memory_space=pltpu.SMEM),
                  pl.BlockSpec((SAMPLE_PAD, BR_W), lambda b, pt: (b, BLK_CQ)),
                  new_tok, new_tok,
                  pl.BlockSpec(memory_space=pl.ANY), pl.BlockSpec(memory_space=pl.ANY),
                  pl.BlockSpec(bias.shape, lambda b, pt: (0, 0, 0)),
                  pl.BlockSpec(nbias.shape, lambda b, pt: (0, 0, 0, 0)),
                  pl.BlockSpec((1, HEAD_W), lambda b, pt: (0, 0))],
        out_specs=pl.BlockSpec((SAMPLE_PAD, BR_W), lambda b, pt: (b, 0)),
        scratch_shapes=[pltpu.VMEM((2, n_past * N_HEADS, HEAD_W), F32), pltpu.VMEM((2, n_past * N_HEADS, HEAD_W), F32),
                        pltpu.SemaphoreType.DMA((2, 2))],
    )
    return pl.pallas_call(
        functools.partial(_diff_sample_body, layer=layer, n_pages=n_pages, n_real=n_real, n_batch=n_batch),
        grid_spec=grid_spec,
        out_shape=jax.ShapeDtypeStruct((n_batch * SAMPLE_PAD, BR_W), F32),
        compiler_params=_cparams(1),
        name="diff_sample",
    )(page_table, scalars, proj, k_new, v_new, cache_k, cache_v, bias, nbias, norm_w)


def _merge_body(oa_ref, ob_ref, oc_ref, g0_ref, g1_ref, g2_ref, x_ref, wb_ref, wo_ref, ln_ref, o_ref):
    m = (_sigmoid(g0_ref[...]) * _dot(oa_ref[...], wb_ref[0])
         + _sigmoid(g1_ref[...]) * _dot(ob_ref[...], wb_ref[1])
         + _sigmoid(g2_ref[...]) * _dot(oc_ref[...], wb_ref[2]))
    y = _dot(m, wo_ref[...])
    o_ref[...] = _layer_norm(ALPHA * x_ref[...] + y, ln_ref[0:1, :], ln_ref[1:2, :])


def _merge(oa, ob, oc, proj, x, w_branch, w_out, ln):
    t = x.shape[0]
    tm = min(ROW_TILE, t)
    br = pl.BlockSpec((tm, BR_W), lambda i: (i, 0))

    def gate(n):
        return pl.BlockSpec((tm, D_MODEL), lambda i: (i, n))

    return pl.pallas_call(
        _merge_body,
        grid=(t // tm,),
        in_specs=[br, br, br, gate(0), gate(1), gate(2),
                  pl.BlockSpec((tm, D_MODEL), lambda i: (i, 0)),
                  pl.BlockSpec(w_branch.shape, lambda i: (0, 0, 0)),
                  pl.BlockSpec(w_out.shape, lambda i: (0, 0)),
                  pl.BlockSpec((SUBLANES, D_MODEL), lambda i: (0, 0))],
        out_specs=pl.BlockSpec((tm, D_MODEL), lambda i: (i, 0)),
        out_shape=jax.ShapeDtypeStruct((t, D_MODEL), F32),
        compiler_params=_cparams(1),
        name="merge",
    )(oa, ob, oc, proj, proj, proj, x, w_branch, w_out, ln)


def _attend_memory(qs, kvs):
    n = range(len(qs))
    s = [_dot_nt(qs[i], kvs[i][0]) * HEAD_W ** -0.5 for i in n]
    p = [jnp.exp(s[i] - jnp.max(s[i], axis=-1, keepdims=True)) for i in n]
    p = [p[i] / jnp.sum(p[i], axis=-1, keepdims=True) for i in n]
    return [_dot(p[i], kvs[i][1]) for i in n]


def _xattn_prompt_body(x_ref, mk_ref, mv_ref, wq_ref, wo_ref, ln_ref, o_ref):
    x = x_ref[...]
    mk, mv = mk_ref[0], mv_ref[0]
    q = _dot(x, wq_ref[...])
    o = jnp.concatenate(_attend_memory([_head(q, h) for h in range(N_HEADS)],
                                       [(_head(mk, h), _head(mv, h)) for h in range(N_HEADS)]), axis=1)
    o_ref[...] = _layer_norm(ALPHA * x + _dot(o, wo_ref[...]), ln_ref[0:1, :], ln_ref[1:2, :])


def _xattn_prompt(x, mkv, w_xq, w_xo, ln, batch, seq):
    tm = min(ROW_TILE, seq)
    per = seq // tm
    n_mem = mkv.shape[1]
    return pl.pallas_call(
        _xattn_prompt_body,
        grid=(batch, per),
        in_specs=[pl.BlockSpec((tm, D_MODEL), lambda b, i: (b * per + i, 0)),
                  pl.BlockSpec((1, n_mem, BR_W), lambda b, i: (b, 0, 0)),
                  pl.BlockSpec((1, n_mem, BR_W), lambda b, i: (b, 0, 1)),
                  pl.BlockSpec(w_xq.shape, lambda b, i: (0, 0)),
                  pl.BlockSpec(w_xo.shape, lambda b, i: (0, 0)),
                  pl.BlockSpec((SUBLANES, D_MODEL), lambda b, i: (0, 0))],
        out_specs=pl.BlockSpec((tm, D_MODEL), lambda b, i: (b * per + i, 0)),
        out_shape=jax.ShapeDtypeStruct(x.shape, F32),
        compiler_params=_cparams(2),
        name="xattn_prompt",
    )(x, mkv, mkv, w_xq, w_xo, ln)


def _xattn_sample_body(x_ref, mk_ref, mv_ref, wq_ref, wo_ref, ln_ref, o_ref, att_ref, *, n_seq):
    x = x_ref[...]
    q = _dot(x, wq_ref[...])
    n_mem = mk_ref.shape[1] // N_HEADS
    tiles = [(i, h) for i in range(n_seq) for h in range(N_HEADS)]
    qs, kvs = [], []
    for i, h in tiles:
        rows = pl.ds(h, n_mem, stride=N_HEADS)
        qs.append(_head(q[i * SAMPLE_PAD:(i + 1) * SAMPLE_PAD], h))
        kvs.append((mk_ref[i, rows, :], mv_ref[i, rows, :]))
    for (i, h), o in zip(tiles, _attend_memory(qs, kvs)):
        att_ref[i * SAMPLE_PAD:(i + 1) * SAMPLE_PAD, h * HEAD_W:(h + 1) * HEAD_W] = o
    o_ref[...] = _layer_norm(ALPHA * x + _dot(att_ref[...], wo_ref[...]), ln_ref[0:1, :], ln_ref[1:2, :])


def _xattn_sample(x, mem_k, mem_v, layer, w_xq, w_xo, ln, seq_per_step=8):
    n_batch, mem_rows = mem_k.shape[1:3]
    rows = seq_per_step * SAMPLE_PAD
    mem = pl.BlockSpec((None, seq_per_step, mem_rows, HEAD_W), lambda g: (layer, g, 0, 0))
    return pl.pallas_call(
        functools.partial(_xattn_sample_body, n_seq=seq_per_step),
        grid=(n_batch // seq_per_step,),
        in_specs=[pl.BlockSpec((rows, D_MODEL), lambda g: (g, 0)), mem, mem,
                  pl.BlockSpec(w_xq.shape, lambda g: (0, 0)),
                  pl.BlockSpec(w_xo.shape, lambda g: (0, 0)),
                  pl.BlockSpec((SUBLANES, D_MODEL), lambda g: (0, 0))],
        out_specs=pl.BlockSpec((rows, D_MODEL), lambda g: (g, 0)),
        out_shape=jax.ShapeDtypeStruct(x.shape, F32),
        scratch_shapes=[pltpu.VMEM((rows, BR_W), F32)],
        compiler_params=_cparams(1),
        name="xattn_sample",
    )(x, mem_k, mem_v, w_xq, w_xo, ln)


def _ffn_body(x_ref, wg_ref, wu_ref, w2_ref, ln_ref, o_ref, acc_ref, *, n_f):
    f = pl.program_id(1)
    x = x_ref[...]
    xb = x.astype(BF16)
    gate = jnp.dot(xb, wg_ref[...], preferred_element_type=F32)
    up = jnp.dot(xb, wu_ref[...], preferred_element_type=F32)
    part = _dot(_silu(gate) * up, w2_ref[...])

    @pl.when(f == 0)
    def _():
        acc_ref[...] = part

    @pl.when(f > 0)
    def _():
        acc_ref[...] += part

    @pl.when(f == n_f - 1)
    def _():
        o_ref[...] = _layer_norm(ALPHA * x + acc_ref[...], ln_ref[0:1, :], ln_ref[1:2, :])


def _ffn(x, w1, w2, ln, tf):
    t = x.shape[0]
    tm = min(ROW_TILE, t)
    n_f = D_FF // tf
    return pl.pallas_call(
        functools.partial(_ffn_body, n_f=n_f),
        grid=(t // tm, n_f),
        in_specs=[pl.BlockSpec((tm, D_MODEL), lambda i, f: (i, 0)),
                  pl.BlockSpec((D_MODEL, tf), lambda i, f: (0, f)),
                  pl.BlockSpec((D_MODEL, tf), lambda i, f: (0, n_f + f)),
                  pl.BlockSpec((tf, D_MODEL), lambda i, f: (f, 0)),
                  pl.BlockSpec((SUBLANES, D_MODEL), lambda i, f: (0, 0))],
        out_specs=pl.BlockSpec((tm, D_MODEL), lambda i, f: (i, 0)),
        out_shape=jax.ShapeDtypeStruct(x.shape, F32),
        scratch_shapes=[pltpu.VMEM((tm, D_MODEL), F32)],
        compiler_params=_cparams(2),
        name="ffn",
    )(x, w1, w1, w2, ln)


def _split_w_in(w):
    ab0 = 8 * BR_W
    c0 = ab0 + 2 * N_HEADS
    g0 = c0 + 3 * BR_W
    pad = jnp.zeros((w.shape[0], LANES - 2 * N_HEADS), w.dtype)
    main = jnp.concatenate([w[:, g0:], w[:, :ab0], w[:, c0:c0 + BR_W]], axis=1)
    kv = jnp.concatenate([w[:, c0 + BR_W:g0], w[:, ab0:c0], pad], axis=1)
    return main.astype(BF16), kv.astype(BF16)


def _ln_rows(g, b):
    return jnp.concatenate([g[None], b[None], jnp.zeros((SUBLANES - 2, g.shape[0]), F32)], axis=0)


def kernel(x_prompt, x_sample, mem_prompt, cache_attn_k, cache_attn_v, cache_mem_k, cache_mem_v, state_hgrn,
           state_gdn, state_gdn_conv, page_table, w_in, w_branch, w_out, lower_bounds, hgrn_norm, gdn_a_log,
           gdn_dt_bias, gdn_conv_w, gdn_norm, diff_lambda, diff_norm, rel_bias, w_xq, w_mem_kv, w_xo, ln_g, ln_b,
           w_ffn_in, w_ffn_out):
    bp, seq, _ = x_prompt.shape
    bs, ls, _ = x_sample.shape
    n_mem = mem_prompt.shape[1]
    n_pages = page_table.shape[1]
    n_past = n_pages * PAGE_SIZE
    n_pool = cache_attn_k.shape[1]
    pad = SAMPLE_PAD - ls

    lb_cum = jnp.cumsum(jax.nn.softmax(lower_bounds.astype(F32), axis=0), axis=0)
    lb = (lb_cum - lb_cum[0]).reshape(DEPTH, N_HEADS, 1, HEAD_W)
    lbp = jnp.concatenate([jnp.log(lb), jnp.log1p(-lb), 1.0 - lb,
                           jnp.zeros((DEPTH, N_HEADS, SUBLANES - 3, HEAD_W), F32)], axis=2)

    n_max = max(2 * ATT_BLOCK + 2, n_past + ls)
    bucket = t5_bucket(-jnp.arange(n_max, dtype=jnp.int32))
    thr = jnp.searchsorted(bucket, jnp.arange(N_BUCKETS, dtype=jnp.int32), side="left").astype(jnp.int32)
    rel_bias_t = rel_bias.astype(F32).T
    tiles = _bias_tiles(thr, rel_bias_t)
    dist = rel_bias_t[:, bucket]
    far_bias = dist[:, ATT_BLOCK + 1]
    q_pos = n_past + jnp.arange(ls)
    past = dist[:, q_pos[:, None] - jnp.arange(n_past)[None, :]]
    bias_s = jnp.concatenate([past, past], axis=1)
    tt = jnp.arange(ls)[:, None] - jnp.arange(ls)[None, :]
    new = jnp.where(tt >= 0, dist[:, jnp.maximum(tt, 0)], NEG_INF)
    nbias = jnp.concatenate([new, new], axis=1).transpose(0, 2, 1)[..., None]

    xp = x_prompt.reshape(bp * seq, D_MODEL)
    xs = jnp.concatenate([jnp.zeros((bs, pad, D_MODEL), F32), x_sample], axis=1).reshape(bs * SAMPLE_PAD, D_MODEL)
    mem = mem_prompt.reshape(bp * n_mem, D_MODEL)
    cache_k = cache_attn_k.reshape(DEPTH, n_pool, PAGE_SIZE * N_HEADS, HEAD_W)
    cache_v = cache_attn_v.reshape(DEPTH, n_pool, PAGE_SIZE * N_HEADS, HEAD_W)
    mem_k = cache_mem_k.reshape(DEPTH, bs, n_mem * N_HEADS, HEAD_W)
    mem_v = cache_mem_v.reshape(DEPTH, bs, n_mem * N_HEADS, HEAD_W)

    pk = pv = jnp.zeros((DEPTH, bp * seq, BR_W), F32)
    sk = sv = jnp.zeros((DEPTH, bs * SAMPLE_PAD, BR_W), F32)
    sh = sg = jnp.zeros(state_hgrn.shape, F32)
    pmk, pmv, ph, pg, pc, sc = [], [], [], [], [], []
    b_col0 = BLK_B * BR_W

    for l in range(DEPTH):
        w_main, w_kv = _split_w_in(w_in[l])
        wb_l = w_branch[l].astype(BF16)
        wo_l = w_out[l].astype(BF16)
        wq_l = w_xq[l].astype(BF16)
        wxo_l = w_xo[l].astype(BF16)
        wkv_l = w_mem_kv[l].astype(BF16)
        w1_l = w_ffn_in[l].astype(BF16)
        w2_l = w_ffn_out[l].astype(BF16)
        lns = [_ln_rows(ln_g[l, i], ln_b[l, i]) for i in range(3)]
        hn = hgrn_norm[l][None]
        gn = gdn_norm[l][None]
        dn = diff_norm[l][None]
        head_params = jnp.stack([jnp.exp(gdn_a_log[l].astype(F32)), gdn_dt_bias[l].astype(F32)])
        lam_init = 0.8 - 0.6 * math.exp(-0.3 * l)
        lq1, lk1, lq2, lk2 = diff_lambda[l].astype(F32)
        lam = jnp.exp(jnp.sum(lq1 * lk1)) - jnp.exp(jnp.sum(lq2 * lk2)) + lam_init
        scalars = jnp.stack([jnp.stack([lam, jnp.float32(1.0 - lam_init), jnp.float32(0), jnp.float32(0)]),
                             far_bias])
        conv_w_l = gdn_conv_w[l]

        proj = _matmul(xp, w_main, ROW_TILE, MAIN_W // 3)
        pk, pv, ab = _kv_proj(xp, w_kv, l, pk, pv)
        oa, h_new = _hgrn_prompt(proj, lbp[l], hn, bp, seq)
        qkv = _gdn_conv(proj, BLK_B * N_HEADS, conv_w_l, SEQ_TILE, seq)
        ob, g_new = _gdn_prompt(qkv, proj, ab, head_params, gn, bp, seq)
        oc = _diff_prompt(proj, pk, pv, l, scalars, tiles, dn, bp, seq)
        x1 = _merge(oa, ob, oc, proj, xp, wb_l, wo_l, lns[0])
        mkv = _matmul(mem, wkv_l, ROW_TILE, 2 * BR_W).reshape(bp, n_mem, 2 * BR_W)
        x2 = _xattn_prompt(x1, mkv, wq_l, wxo_l, lns[1], bp, seq)
        xp = _ffn(x2, w1_l, w2_l, lns[2], D_FF // 2)
        pmk.append(mkv[:, :, :BR_W].reshape(bp, n_mem, N_HEADS, HEAD_W))
        pmv.append(mkv[:, :, BR_W:].reshape(bp, n_mem, N_HEADS, HEAD_W))
        ph.append(h_new)
        pg.append(g_new)
        pc.append(proj.reshape(bp, seq, MAIN_W)[:, seq - (CONV_W - 1):, b_col0:b_col0 + CONV_CH])

        proj_s = _matmul(xs, w_main, ROW_TILE, MAIN_W // 3)
        sk, sv, ab_s = _kv_proj(xs, w_kv, l, sk, sv)
        oa, sh = _hgrn_sample(proj_s, lbp[l], hn, state_hgrn, l, sh, ls)
        conv_tok = proj_s[:, b_col0:b_col0 + CONV_CH].reshape(bs, SAMPLE_PAD, CONV_CH)
        conv_in = jnp.concatenate([conv_tok[:, :pad - (CONV_W - 1)], state_gdn_conv[l], conv_tok[:, pad:]], axis=1)
        qkv = _gdn_conv(conv_in.reshape(bs * SAMPLE_PAD, CONV_CH), 0, conv_w_l, bs * SAMPLE_PAD, SAMPLE_PAD)
        ob, sg = _gdn_sample(qkv, proj_s, ab_s, head_params, gn, state_gdn, l, sg, ls)
        oc = _diff_sample(page_table, scalars, proj_s, sk, sv, cache_k, cache_v, l, bias_s, nbias, dn, ls)
        x1 = _merge(oa, ob, oc, proj_s, xs, wb_l, wo_l, lns[0])
        x2 = _xattn_sample(x1, mem_k, mem_v, l, wq_l, wxo_l, lns[1])
        xs = _ffn(x2, w1_l, w2_l, lns[2], D_FF // 2)
        sc.append(conv_in[:, SAMPLE_PAD - (CONV_W - 1):])

    y_prompt = xp.reshape(bp, seq, D_MODEL)
    y_sample = xs.reshape(bs, SAMPLE_PAD, D_MODEL)[:, pad:]
    new_k_p = pk.reshape(DEPTH, bp, seq, N_HEADS, HEAD_W)
    new_v_p = pv.reshape(DEPTH, bp, seq, N_HEADS, HEAD_W)
    new_k_s = sk.reshape(DEPTH, bs, SAMPLE_PAD, N_HEADS, HEAD_W)[:, :, pad:]
    new_v_s = sv.reshape(DEPTH, bs, SAMPLE_PAD, N_HEADS, HEAD_W)[:, :, pad:]
    return (y_prompt, y_sample, new_k_p, new_v_p, jnp.stack(pmk), jnp.stack(pmv), jnp.stack(ph), jnp.stack(pg),
            jnp.stack(pc), new_k_s, new_v_s, sh, sg, jnp.stack(sc))
```

```python
import functools
import math

import jax
import jax.numpy as jnp
from jax import lax
from jax.experimental import pallas as pl
from jax.experimental.pallas import tpu as pltpu

F32 = jnp.float32
BF16 = jnp.bfloat16

D_MODEL = 1024
DEPTH = 4
PAGE_SIZE = 128
BR_W = D_MODEL // 2
N_HEADS = 4
HEAD_W = BR_W // N_HEADS
DH_C = HEAD_W // 2
CONV_W = 4
CONV_CH = 3 * BR_W
N_BRANCH = 3
N_BUCKETS = 32
MAX_DISTANCE = 128
D_FF = -(-8 * D_MODEL // (3 * 256)) * 256
LN_EPS = 1e-5
NORM_EPS = 1e-6
NEG_INF = -1e30
ALPHA = (2 * DEPTH) ** 0.25

LANES = 128
SUBLANES = 8
VMEM_LIMIT_BYTES = 56 * 1024 * 1024

BLK_A = N_BRANCH * D_MODEL // BR_W
BLK_B = BLK_A + 4
BLK_CQ = BLK_B + 4
MAIN_W = (BLK_CQ + 1) * BR_W
KV_W = 2 * BR_W + LANES

HGRN_SUB = 16
HGRN_ROWS = 64
GDN_CHUNK = 64
SOLVE_BLOCK = 16
SAMPLE_PAD = 8
SEQ_TILE = 512
ATT_BLOCK = 512
ATT_HEADS = 2
ROW_TILE = 512


def _cparams(n_axes):
    return pltpu.CompilerParams(dimension_semantics=("arbitrary",) * n_axes,
                                vmem_limit_bytes=VMEM_LIMIT_BYTES)


def _dot(a, b):
    return jnp.dot(a.astype(BF16), b.astype(BF16), preferred_element_type=F32)


def _dot_nt(a, b):
    return lax.dot_general(a.astype(BF16), b.astype(BF16), (((1,), (1,)), ((), ())),
                           preferred_element_type=F32)


def _dot_tn(a, b):
    return lax.dot_general(a.astype(BF16), b.astype(BF16), (((0,), (0,)), ((), ())),
                           preferred_element_type=F32)


def _dot_exact_lhs(m01, x):
    hi = x.astype(BF16)
    r1 = x - hi.astype(F32)
    mid = r1.astype(BF16)
    lo = (r1 - mid.astype(F32)).astype(BF16)
    dot = functools.partial(jnp.dot, preferred_element_type=F32)
    return dot(m01, hi) + dot(m01, mid) + dot(m01, lo)


def _sigmoid(x):
    return 1.0 / (1.0 + jnp.exp(-x))


def _silu(x):
    return x * _sigmoid(x)


def _softplus(x):
    return jnp.maximum(x, 0.0) + jnp.log1p(jnp.exp(-jnp.abs(x)))


def _layer_norm(h, g, b):
    mu = jnp.mean(h, axis=-1, keepdims=True)
    c = h - mu
    var = jnp.mean(c * c, axis=-1, keepdims=True)
    return c * lax.rsqrt(var + LN_EPS) * g + b


def _rms_norm(x, g):
    return x * lax.rsqrt(jnp.mean(x * x, axis=-1, keepdims=True) + NORM_EPS) * g


def _bcast_rows(x, rows, reps):
    return jnp.concatenate([jnp.broadcast_to(x[r:r + 1, :], (reps, x.shape[1])) for r in rows], axis=0)


def _head(x, h):
    return x[:, h * HEAD_W:(h + 1) * HEAD_W]


ALIASED = pl.BlockSpec(memory_space=pl.ANY)


def _mm_body(x_ref, w_ref, o_ref):
    o_ref[...] = _dot(x_ref[...], w_ref[...])


def _matmul(x, w, tm, tn):
    t, k = x.shape
    n = w.shape[1]
    tm = min(tm, t)
    return pl.pallas_call(
        _mm_body,
        grid=(n // tn, t // tm),
        in_specs=[pl.BlockSpec((tm, k), lambda j, i: (i, 0)),
                  pl.BlockSpec((k, tn), lambda j, i: (0, j))],
        out_specs=pl.BlockSpec((tm, tn), lambda j, i: (i, j)),
        out_shape=jax.ShapeDtypeStruct((t, n), F32),
        compiler_params=_cparams(2),
        name="proj_matmul",
    )(x, w)


def _kv_body(x_ref, w_ref, *refs):
    k_ref, v_ref, ab_ref = refs[-3:]
    y = _dot(x_ref[...], w_ref[...])
    k_ref[...] = y[:, :BR_W]
    v_ref[...] = y[:, BR_W:2 * BR_W]
    ab_ref[...] = y[:, 2 * BR_W:]


def _kv_proj(x, w, layer, k_prev, v_prev):
    t = x.shape[0]
    tm = min(ROW_TILE, t)
    slab = pl.BlockSpec((None, tm, BR_W), lambda i: (layer, i, 0))
    stacked = jax.ShapeDtypeStruct((DEPTH, t, BR_W), F32)
    return pl.pallas_call(
        _kv_body,
        grid=(t // tm,),
        in_specs=[pl.BlockSpec((tm, D_MODEL), lambda i: (i, 0)),
                  pl.BlockSpec(w.shape, lambda i: (0, 0)), ALIASED, ALIASED],
        out_specs=[slab, slab, pl.BlockSpec((tm, LANES), lambda i: (i, 0))],
        out_shape=[stacked, stacked, jax.ShapeDtypeStruct((t, LANES), F32)],
        input_output_aliases={2: 0, 3: 1},
        compiler_params=_cparams(1),
        name="kv_proj",
    )(x, w, k_prev, v_prev)


def _hgrn_rows(q, z, v, lbp, sub, masks, valid):
    r = q.shape[0]
    n_sub = r // sub
    tri, pos = masks
    log_lb, log1m_lb, om_lb = lbp[0:1, :], lbp[1:2, :], lbp[2:3, :]
    log_sig = jnp.minimum(z, 0.0) - jnp.log(1.0 + jnp.exp(-jnp.abs(z)))
    t2 = log1m_lb + log_sig
    logf = jnp.maximum(log_lb, t2) + jnp.log(1.0 + jnp.exp(-jnp.abs(log_lb - t2)))
    k = om_lb * (1.0 / (1.0 + jnp.exp(z)))
    if valid is not None:
        logf = jnp.where(valid, logf, 0.0)
        k = jnp.where(valid, k, 0.0)
    bc = _dot_exact_lhs(tri, logf)
    last_rows = [i * sub + sub - 1 for i in range(n_sub)]
    b_last = _bcast_rows(bc, last_rows, sub)
    q_hat = q * jnp.exp(bc)
    k_hat = k * jnp.exp(b_last - bc)
    o = jnp.zeros_like(q)
    for s in range(sub):
        rows = [i * sub + s for i in range(n_sub)]
        ks = _bcast_rows(k, rows, sub)
        bs = _bcast_rows(bc, rows, sub)
        vs = _bcast_rows(v, rows, sub)
        a = q * ks * jnp.exp(jnp.minimum(bc - bs, 0.0))
        a = jnp.where(pos >= s, a, 0.0)
        o = o + jnp.sum(a, axis=-1, keepdims=True) * vs
    return q_hat, k_hat, o, bc


def _sub_chunk_masks(rows, sub):
    ri = lax.broadcasted_iota(jnp.int32, (rows, rows), 0)
    ci = lax.broadcasted_iota(jnp.int32, (rows, rows), 1)
    tri = ((ri // sub == ci // sub) & (ci <= ri)).astype(BF16)
    pos = lax.broadcasted_iota(jnp.int32, (rows, 1), 0) % sub
    return tri, pos


def _hgrn_finish(o, g, norm_w):
    return _rms_norm(o, norm_w) * _silu(g)


def _hgrn_prompt_body(q_ref, f_ref, i_ref, g_ref, lbp_ref, nw_ref, o_ref, s_out_ref, st_ref, *, n_tiles):
    t = pl.program_id(1)

    @pl.when(t == 0)
    def _():
        st_ref[...] = jnp.zeros_like(st_ref)

    rows = HGRN_ROWS
    nw = nw_ref[...]
    masks = _sub_chunk_masks(rows, HGRN_SUB)

    def chunk(c, carry):
        sl = pl.ds(pl.multiple_of(c * rows, rows), rows)
        qa, za, va, ga = q_ref[sl, :], f_ref[sl, :], i_ref[sl, :], g_ref[sl, :]
        hs = range(N_HEADS)
        v = [_head(va, h) for h in hs]
        parts = [_hgrn_rows(_head(qa, h), _head(za, h), v[h], lbp_ref[h], HGRN_SUB, masks, None) for h in hs]
        st = [st_ref[h] for h in hs]
        outs = [[] for _ in hs]
        for i in range(rows // HGRN_SUB):
            lo, hi = i * HGRN_SUB, (i + 1) * HGRN_SUB
            for h in hs:
                q_hat, k_hat, o, bc = parts[h]
                outs[h].append(o[lo:hi] + _dot_nt(q_hat[lo:hi], st[h]))
                st[h] = st[h] * jnp.exp(bc[hi - 1:hi, :]) + _dot_tn(v[h][lo:hi], k_hat[lo:hi])
        for h in hs:
            st_ref[h] = st[h]
        o_ref[sl, :] = jnp.concatenate(
            [_hgrn_finish(jnp.concatenate(outs[h], axis=0), _head(ga, h), nw) for h in hs], axis=1)
        return carry

    lax.fori_loop(0, SEQ_TILE // rows, chunk, 0)

    @pl.when(t == n_tiles - 1)
    def _():
        for h in range(N_HEADS):
            s_out_ref[0, h] = st_ref[h].T


def _hgrn_prompt(proj, lbp, norm_w, batch, seq):
    n_tiles = seq // SEQ_TILE

    def col(blk):
        return pl.BlockSpec((SEQ_TILE, BR_W), lambda b, t: (b * n_tiles + t, blk))

    return pl.pallas_call(
        functools.partial(_hgrn_prompt_body, n_tiles=n_tiles),
        grid=(batch, n_tiles),
        in_specs=[col(BLK_A), col(BLK_A + 1), col(BLK_A + 2), col(BLK_A + 3),
                  pl.BlockSpec((N_HEADS, SUBLANES, HEAD_W), lambda b, t: (0, 0, 0)),
                  pl.BlockSpec((1, HEAD_W), lambda b, t: (0, 0))],
        out_specs=[pl.BlockSpec((SEQ_TILE, BR_W), lambda b, t: (b * n_tiles + t, 0)),
                   pl.BlockSpec((1, N_HEADS, HEAD_W, HEAD_W), lambda b, t: (b, 0, 0, 0))],
        out_shape=[jax.ShapeDtypeStruct((batch * seq, BR_W), F32),
                   jax.ShapeDtypeStruct((batch, N_HEADS, HEAD_W, HEAD_W), F32)],
        scratch_shapes=[pltpu.VMEM((N_HEADS, HEAD_W, HEAD_W), F32)],
        compiler_params=_cparams(2),
        name="hgrn_prompt",
    )(proj, proj, proj, proj, lbp, norm_w)


def _hgrn_sample_body(q_ref, f_ref, i_ref, g_ref, lbp_ref, nw_ref, s0_ref, *refs, n_seq, n_real):
    o_ref, s_out_ref = refs[-2:]
    nw = nw_ref[...]
    rows = n_seq * SAMPLE_PAD
    masks = _sub_chunk_masks(rows, SAMPLE_PAD)
    valid = masks[1] >= SAMPLE_PAD - n_real
    qa, za, va, ga = q_ref[...], f_ref[...], i_ref[...], g_ref[...]
    hs = range(N_HEADS)
    v = [_head(va, h) for h in hs]
    parts = [_hgrn_rows(_head(qa, h), _head(za, h), v[h], lbp_ref[h], SAMPLE_PAD, masks, valid) for h in hs]
    eye = (lax.broadcasted_iota(jnp.int32, (HEAD_W, HEAD_W), 0)
           == lax.broadcasted_iota(jnp.int32, (HEAD_W, HEAD_W), 1))
    outs = [[] for _ in hs]
    for i in range(n_seq):
        lo, hi = i * SAMPLE_PAD, (i + 1) * SAMPLE_PAD
        for h in hs:
            q_hat, k_hat, o, bc = parts[h]
            s0 = s0_ref[i, h]
            outs[h].append(o[lo:hi] + _dot(q_hat[lo:hi], s0))
            decay = jnp.exp(bc[hi - 1:hi, :])
            decay_col = jnp.sum(jnp.where(eye, decay, 0.0), axis=-1, keepdims=True)
            s_out_ref[i, h] = s0 * decay_col + _dot_tn(k_hat[lo:hi], v[h][lo:hi])
    o_ref[...] = jnp.concatenate(
        [_hgrn_finish(jnp.concatenate(outs[h], axis=0), _head(ga, h), nw) for h in hs], axis=1)


def _state_specs(layer, seq_per_step):
    return pl.BlockSpec((None, seq_per_step, N_HEADS, HEAD_W, HEAD_W), lambda g: (layer, g, 0, 0, 0))


def _hgrn_sample(proj, lbp, norm_w, states, layer, prev, n_real, seq_per_step=8):
    n_batch = states.shape[1]
    rows = seq_per_step * SAMPLE_PAD

    def col(blk):
        return pl.BlockSpec((rows, BR_W), lambda g: (g, blk))

    return pl.pallas_call(
        functools.partial(_hgrn_sample_body, n_seq=seq_per_step, n_real=n_real),
        grid=(n_batch // seq_per_step,),
        in_specs=[col(BLK_A), col(BLK_A + 1), col(BLK_A + 2), col(BLK_A + 3),
                  pl.BlockSpec((N_HEADS, SUBLANES, HEAD_W), lambda g: (0, 0, 0)),
                  pl.BlockSpec((1, HEAD_W), lambda g: (0, 0)),
                  _state_specs(layer, seq_per_step), ALIASED],
        out_specs=[pl.BlockSpec((rows, BR_W), lambda g: (g, 0)), _state_specs(layer, seq_per_step)],
        out_shape=[jax.ShapeDtypeStruct((n_batch * SAMPLE_PAD, BR_W), F32),
                   jax.ShapeDtypeStruct(states.shape, F32)],
        input_output_aliases={7: 1},
        compiler_params=_cparams(1),
        name="hgrn_sample",
    )(proj, proj, proj, proj, lbp, norm_w, states, prev)


def _conv_body(prev_ref, cur_ref, w_ref, o_ref, buf_ref, *, tile, seq_len):
    i = pl.program_id(0)
    c = pl.program_id(1)
    buf_ref[0:SUBLANES, :] = prev_ref[...]
    buf_ref[SUBLANES:, :] = cur_ref[...]
    pos = (i * tile + lax.broadcasted_iota(jnp.int32, (tile, 1), 0)) % seq_len
    w = w_ref[...]
    acc = cur_ref[...] * w[CONV_W - 1:CONV_W, :]
    for back in range(1, CONV_W):
        shifted = buf_ref[pl.ds(SUBLANES - back, tile), :]
        acc = acc + jnp.where(pos >= back, shifted, 0.0) * w[CONV_W - 1 - back:CONV_W - back, :]
    act = _silu(acc)
    q_scale = jnp.where(c == 0, HEAD_W ** -0.5, 1.0)
    heads = []
    for h in range(N_HEADS):
        a = _head(act, h)
        inv = lax.rsqrt(jnp.sum(a * a, axis=-1, keepdims=True) + NORM_EPS)
        heads.append(a * jnp.where(c < 2, inv * q_scale, 1.0))
    o_ref[...] = jnp.concatenate(heads, axis=1)


def _gdn_conv(conv_in, col0, conv_w, tile, seq_len):
    t = conv_in.shape[0]
    per = tile // SUBLANES
    return pl.pallas_call(
        functools.partial(_conv_body, tile=tile, seq_len=seq_len),
        grid=(t // tile, CONV_CH // BR_W),
        in_specs=[pl.BlockSpec((SUBLANES, BR_W), lambda i, c: (jnp.maximum(i * per - 1, 0), col0 + c)),
                  pl.BlockSpec((tile, BR_W), lambda i, c: (i, col0 + c)),
                  pl.BlockSpec((CONV_W, BR_W), lambda i, c: (0, c))],
        out_specs=pl.BlockSpec((tile, BR_W), lambda i, c: (i, c)),
        out_shape=jax.ShapeDtypeStruct((t, CONV_CH), F32),
        scratch_shapes=[pltpu.VMEM((tile + SUBLANES, BR_W), F32)],
        compiler_params=_cparams(2),
        name="gdn_conv",
    )(conv_in, conv_in, conv_w)


def _dot_f32(a, b):
    ah = a.astype(BF16)
    al = (a - ah.astype(F32)).astype(BF16)
    bh = b.astype(BF16)
    bl = (b - bh.astype(F32)).astype(BF16)
    dot = functools.partial(jnp.dot, preferred_element_type=F32)
    return dot(ah, bh) + dot(ah, bl) + dot(al, bh)


def _forward_substitute(lmats, rhss):
    n_sys = len(lmats)
    c = lmats[0].shape[0]
    nb = min(c, SOLVE_BLOCK)
    done = [[] for _ in range(n_sys)]
    for i0 in range(0, c, nb):
        xbs, lbs = [], []
        for i in range(n_sys):
            r = rhss[i][i0:i0 + nb, :]
            if i0:
                r = r - _dot_f32(lmats[i][i0:i0 + nb, :i0], jnp.concatenate(done[i], axis=0))
            xbs.append([r[s:s + SUBLANES, :] for s in range(0, nb, SUBLANES)])
            lbs.append([lmats[i][i0 + s:i0 + s + SUBLANES, :] for s in range(0, nb, SUBLANES)])
        for j in range(nb - 1):
            for i in range(n_sys):
                xb = xbs[i]
                xj = xb[j // SUBLANES][j % SUBLANES:j % SUBLANES + 1, :]
                for b in range((j + 1) // SUBLANES, len(xb)):
                    xb[b] = xb[b] - lbs[i][b][:, i0 + j:i0 + j + 1] * xj
        for i in range(n_sys):
            done[i].extend(xbs[i])
    return [jnp.concatenate(d, axis=0) if len(d) > 1 else d[0] for d in done]


def _gdn_masks(c):
    ri = lax.broadcasted_iota(jnp.int32, (c, c), 0)
    ci = lax.broadcasted_iota(jnp.int32, (c, c), 1)
    mr = lax.broadcasted_iota(jnp.int32, (c, LANES), 0)
    mc = lax.broadcasted_iota(jnp.int32, (c, LANES), 1)
    return ci <= ri, ci < ri, ((mc < mr) & (mc < c)) | (mc == c)


def _gdn_heads(hp_ref, qa, ka, va, ga, ab, nw, get_state, put_state, masks, valid):
    c = qa.shape[0]
    incl, strict, sel = masks
    hs = range(N_HEADS)
    lane = lax.broadcasted_iota(jnp.int32, ab.shape, 1)
    q = [_head(qa, h) for h in hs]
    k = [_head(ka, h) for h in hs]
    g, beta = [], []
    for h in hs:
        a_col = jnp.sum(jnp.where(lane == h, ab, 0.0), axis=-1, keepdims=True)
        b_col = jnp.sum(jnp.where(lane == h + N_HEADS, ab, 0.0), axis=-1, keepdims=True)
        g_h = -hp_ref[0, h] * _softplus(a_col + hp_ref[1, h])
        beta_h = _sigmoid(b_col)
        if valid is not None:
            g_h = jnp.where(valid, g_h, 0.0)
            beta_h = jnp.where(valid, beta_h, 0.0)
        g.append(g_h)
        beta.append(beta_h)
    incl_b = incl.astype(BF16)
    m = [_dot_exact_lhs(incl_b, jnp.where(sel, g[h], 0.0)) for h in hs]
    gcum = [m[h][:, c:c + 1] for h in hs]
    decay = [jnp.where(incl, jnp.exp(jnp.where(incl, m[h][:, :c], 0.0)), 0.0) for h in hs]
    kb = [k[h] * beta[h] for h in hs]
    lmat = [jnp.where(strict, _dot_nt(kb[h], k[h]) * decay[h], 0.0) for h in hs]
    e_g = [jnp.exp(gcum[h]) for h in hs]
    sol = _forward_substitute(lmat, [jnp.concatenate([_head(va, h) * beta[h], kb[h] * e_g[h]], axis=1) for h in hs])
    s = [get_state(h) for h in hs]
    v_new = [sol[h][:, :HEAD_W] - _dot(sol[h][:, HEAD_W:], s[h]) for h in hs]
    attn = [jnp.where(incl, _dot_nt(q[h], k[h]) * decay[h], 0.0) for h in hs]
    o = [_dot(q[h] * e_g[h], s[h]) + _dot(attn[h], v_new[h]) for h in hs]
    for h in hs:
        g_last = gcum[h][c - 1:c, :]
        put_state(h, s[h] * jnp.exp(g_last) + _dot_tn(k[h] * jnp.exp(g_last - gcum[h]), v_new[h]))
    return jnp.concatenate([_rms_norm(o[h], nw) * _silu(_head(ga, h)) for h in hs], axis=1)


def _gdn_prompt_body(hp_ref, q_ref, k_ref, v_ref, g_ref, ab_ref, nw_ref, o_ref, s_out_ref, s_ref, *, n_tiles):
    t = pl.program_id(1)

    @pl.when(t == 0)
    def _():
        s_ref[...] = jnp.zeros_like(s_ref)

    nw = nw_ref[...]
    masks = _gdn_masks(GDN_CHUNK)

    def put(h, s_new):
        s_ref[h] = s_new

    def chunk(c, carry):
        sl = pl.ds(pl.multiple_of(c * GDN_CHUNK, GDN_CHUNK), GDN_CHUNK)
        o_ref[sl, :] = _gdn_heads(hp_ref, q_ref[sl, :], k_ref[sl, :], v_ref[sl, :], g_ref[sl, :], ab_ref[sl, :],
                                  nw, lambda h: s_ref[h], put, masks, None)
        return carry

    lax.fori_loop(0, SEQ_TILE // GDN_CHUNK, chunk, 0)

    @pl.when(t == n_tiles - 1)
    def _():
        s_out_ref[0] = s_ref[...]


def _gdn_prompt(qkv, proj, ab, head_params, norm_w, batch, seq):
    n_tiles = seq // SEQ_TILE

    def col(blk):
        return pl.BlockSpec((SEQ_TILE, BR_W), lambda b, t: (b * n_tiles + t, blk))

    return pl.pallas_call(
        functools.partial(_gdn_prompt_body, n_tiles=n_tiles),
        grid=(batch, n_tiles),
        in_specs=[pl.BlockSpec(memory_space=pltpu.SMEM),
                  col(0), col(1), col(2), col(BLK_B + 3),
                  pl.BlockSpec((SEQ_TILE, LANES), lambda b, t: (b * n_tiles + t, 0)),
                  pl.BlockSpec((1, HEAD_W), lambda b, t: (0, 0))],
        out_specs=[pl.BlockSpec((SEQ_TILE, BR_W), lambda b, t: (b * n_tiles + t, 0)),
                   pl.BlockSpec((1, N_HEADS, HEAD_W, HEAD_W), lambda b, t: (b, 0, 0, 0))],
        out_shape=[jax.ShapeDtypeStruct((batch * seq, BR_W), F32),
                   jax.ShapeDtypeStruct((batch, N_HEADS, HEAD_W, HEAD_W), F32)],
        scratch_shapes=[pltpu.VMEM((N_HEADS, HEAD_W, HEAD_W), F32)],
        compiler_params=_cparams(2),
        name="gdn_prompt",
    )(head_params, qkv, qkv, qkv, proj, ab, norm_w)


def _gdn_sample_body(hp_ref, q_ref, k_ref, v_ref, g_ref, ab_ref, nw_ref, s0_ref, *refs, n_seq, n_real):
    o_ref, s_out_ref = refs[-2:]
    nw = nw_ref[...]
    masks = _gdn_masks(SAMPLE_PAD)
    valid = lax.broadcasted_iota(jnp.int32, (SAMPLE_PAD, 1), 0) >= SAMPLE_PAD - n_real

    def one(i, carry):
        sl = pl.ds(pl.multiple_of(i * SAMPLE_PAD, SAMPLE_PAD), SAMPLE_PAD)

        def put(h, s_new):
            s_out_ref[i, h] = s_new

        o_ref[sl, :] = _gdn_heads(hp_ref, q_ref[sl, :], k_ref[sl, :], v_ref[sl, :], g_ref[sl, :], ab_ref[sl, :],
                                  nw, lambda h: s0_ref[i, h], put, masks, valid)
        return carry

    lax.fori_loop(0, n_seq, one, 0)


def _gdn_sample(qkv, proj, ab, head_params, norm_w, states, layer, prev, n_real, seq_per_step=8):
    n_batch = states.shape[1]
    rows = seq_per_step * SAMPLE_PAD

    def col(blk):
        return pl.BlockSpec((rows, BR_W), lambda g: (g, blk))

    return pl.pallas_call(
        functools.partial(_gdn_sample_body, n_seq=seq_per_step, n_real=n_real),
        grid=(n_batch // seq_per_step,),
        in_specs=[pl.BlockSpec(memory_space=pltpu.SMEM),
                  col(0), col(1), col(2), col(BLK_B + 3),
                  pl.BlockSpec((rows, LANES), lambda g: (g, 0)),
                  pl.BlockSpec((1, HEAD_W), lambda g: (0, 0)),
                  _state_specs(layer, seq_per_step), ALIASED],
        out_specs=[pl.BlockSpec((rows, BR_W), lambda g: (g, 0)), _state_specs(layer, seq_per_step)],
        out_shape=[jax.ShapeDtypeStruct((n_batch * SAMPLE_PAD, BR_W), F32),
                   jax.ShapeDtypeStruct(states.shape, F32)],
        input_output_aliases={8: 1},
        compiler_params=_cparams(1),
        name="gdn_sample",
    )(head_params, qkv, qkv, qkv, proj, ab, norm_w, states, prev)


def t5_bucket(rel):
    n = jnp.maximum(-rel, 0)
    exact = N_BUCKETS // 2
    log_part = jnp.log(jnp.maximum(n, 1).astype(F32) / exact) / math.log(MAX_DISTANCE / exact)
    large = jnp.minimum(exact + (log_part * (N_BUCKETS - exact)).astype(jnp.int32), N_BUCKETS - 1)
    return jnp.where(n < exact, n, large)


def _bias_tile_body(thr_ref, rb_ref, o_ref):
    h = pl.program_id(0)
    blk = ATT_BLOCK
    row = lax.broadcasted_iota(jnp.int32, (blk, blk), 0)
    col = lax.broadcasted_iota(jnp.int32, (blk, blk), 1)
    for which in range(2):
        n = row - col + which * blk
        val = jnp.full((blk, blk), rb_ref[h, 0], F32)
        for b in range(1, N_BUCKETS):
            val = jnp.where(n >= thr_ref[b], rb_ref[h, b], val)
        if which == 0:
            val = jnp.where(n >= 0, val, NEG_INF)
        o_ref[which] = val


def _bias_tiles(thr, rel_bias_t):
    return pl.pallas_call(
        _bias_tile_body,
        grid=(N_HEADS,),
        in_specs=[pl.BlockSpec(memory_space=pltpu.SMEM), pl.BlockSpec(memory_space=pltpu.SMEM)],
        out_specs=pl.BlockSpec((None, 2, ATT_BLOCK, ATT_BLOCK), lambda h: (h, 0, 0, 0)),
        out_shape=jax.ShapeDtypeStruct((N_HEADS, 2, ATT_BLOCK, ATT_BLOCK), F32),
        compiler_params=_cparams(1),
        name="bias_tiles",
    )(thr, rel_bias_t)


def _stack_maps(q):
    lane = lax.broadcasted_iota(jnp.int32, q.shape, 1)
    return jnp.concatenate([jnp.where(lane < DH_C, q, 0.0), jnp.where(lane >= DH_C, q, 0.0)], axis=0)


def _diff_prompt_body(sc_ref, q_ref, k_ref, v_ref, d_ref, nw_ref, o_ref, k16_ref, v16_ref, m_ref, acc_ref):
    hp = pl.program_id(1)
    qi = pl.program_id(2)
    blk = ATT_BLOCK
    seq = k16_ref.shape[1]

    @pl.when(qi == 0)
    def _():
        for hh in range(ATT_HEADS):
            k16_ref[hh] = _head(k_ref[...], hh).astype(BF16)
            v16_ref[hh, :, :HEAD_W] = _head(v_ref[...], hh).astype(BF16)
            v16_ref[hh, :, HEAD_W:] = jnp.ones((seq, HEAD_W), BF16)

    lam = sc_ref[0, 0]
    out_scale = sc_ref[0, 1]
    qs = [_stack_maps(_head(q_ref[...], hh) * DH_C ** -0.5).astype(BF16) for hh in range(ATT_HEADS)]
    m_ref[...] = jnp.full_like(m_ref, -jnp.inf)
    acc_ref[...] = jnp.zeros_like(acc_ref)

    def block(kj, bias_of):
        sl = pl.ds(pl.multiple_of(kj * blk, blk), blk)
        for hh in range(ATT_HEADS):
            s = lax.dot_general(qs[hh], k16_ref[hh, sl, :], (((1,), (1,)), ((), ())), preferred_element_type=F32)
            bias = bias_of(hh)
            if bias.ndim == 2:
                s = (s.reshape(2, blk, blk) + bias[None]).reshape(2 * blk, blk)
            else:
                s = s + bias
            m_old = m_ref[hh]
            m_new = jnp.maximum(m_old, jnp.max(s, axis=-1, keepdims=True))
            alpha = jnp.exp(m_old - m_new)
            p = jnp.exp(s - jnp.concatenate([m_new] * (blk // LANES), axis=1))
            pv = jnp.dot(p.astype(BF16), v16_ref[hh, sl, :], preferred_element_type=F32)
            acc_ref[hh] = jnp.concatenate([alpha, alpha], axis=1) * acc_ref[hh] + pv
            m_ref[hh] = m_new

    def far(kj, carry):
        block(kj, lambda hh: sc_ref[1, hp * ATT_HEADS + hh])
        return carry

    lax.fori_loop(0, jnp.maximum(qi - 1, 0), far, 0)

    @pl.when(qi >= 1)
    def _():
        block(qi - 1, lambda hh: d_ref[hh, 1])

    block(qi, lambda hh: d_ref[hh, 0])
    outs = []
    for hh in range(ATT_HEADS):
        acc = acc_ref[hh]
        o = acc[:, :HEAD_W] / acc[:, HEAD_W:]
        o = o[:blk] - lam * o[blk:]
        outs.append(_rms_norm(o, nw_ref[...]) * out_scale)
    o_ref[...] = jnp.concatenate(outs, axis=1)


def _diff_prompt(proj, k_all, v_all, layer, scalars, tiles, norm_w, batch, seq):
    nq = seq // ATT_BLOCK
    width = ATT_HEADS * HEAD_W
    per_row = BR_W // width
    kv = pl.BlockSpec((None, seq, width), lambda b, h, q: (layer, b, h))
    return pl.pallas_call(
        _diff_prompt_body,
        grid=(batch, N_HEADS // ATT_HEADS, nq),
        in_specs=[pl.BlockSpec(memory_space=pltpu.SMEM),
                  pl.BlockSpec((ATT_BLOCK, width), lambda b, h, q: (b * nq + q, BLK_CQ * per_row + h)),
                  kv, kv,
                  pl.BlockSpec((ATT_HEADS, 2, ATT_BLOCK, ATT_BLOCK), lambda b, h, q: (h, 0, 0, 0)),
                  pl.BlockSpec((1, HEAD_W), lambda b, h, q: (0, 0))],
        out_specs=pl.BlockSpec((ATT_BLOCK, width), lambda b, h, q: (b * nq + q, h)),
        out_shape=jax.ShapeDtypeStruct((batch * seq, BR_W), F32),
        scratch_shapes=[pltpu.VMEM((ATT_HEADS, seq, HEAD_W), BF16), pltpu.VMEM((ATT_HEADS, seq, 2 * HEAD_W), BF16),
                        pltpu.VMEM((ATT_HEADS, 2 * ATT_BLOCK, LANES), F32),
                        pltpu.VMEM((ATT_HEADS, 2 * ATT_BLOCK, 2 * HEAD_W), F32)],
        compiler_params=_cparams(3),
        name="diff_prompt",
    )(scalars, proj, k_all, v_all, tiles, norm_w)


def _diff_sample_body(pt_ref, sc_ref, q_ref, kn_ref, vn_ref, ck_ref, cv_ref, bias_ref, nbias_ref, nw_ref,
                      o_ref, kbuf, vbuf, sem, *, layer, n_pages, n_real, n_batch):
    b = pl.program_id(0)
    slot = b % 2
    pad = SAMPLE_PAD - n_real

    def page_copies(seq_idx, sl):
        cps = []
        for j in range(n_pages):
            page = pt_ref[seq_idx, j]
            rows = pl.ds(j * PAGE_SIZE * N_HEADS, PAGE_SIZE * N_HEADS)
            cps.append(pltpu.make_async_copy(ck_ref.at[layer, page], kbuf.at[sl, rows], sem.at[sl, 0]))
            cps.append(pltpu.make_async_copy(cv_ref.at[layer, page], vbuf.at[sl, rows], sem.at[sl, 1]))
        return cps

    @pl.when(b == 0)
    def _():
        for cp in page_copies(0, 0):
            cp.start()

    @pl.when(b + 1 < n_batch)
    def _():
        for cp in page_copies(b + 1, 1 - slot):
            cp.start()

    for cp in page_copies(b, slot):
        cp.wait()

    lam = sc_ref[0, 0]
    out_scale = sc_ref[0, 1]
    hs = range(N_HEADS)
    head_rows = [pl.ds(h, n_pages * PAGE_SIZE, stride=N_HEADS) for h in hs]
    qs = [_stack_maps(q_ref[pad:, h * HEAD_W:(h + 1) * HEAD_W] * DH_C ** -0.5) for h in hs]
    s = [_dot_nt(qs[h], kbuf[slot, head_rows[h], :]) + bias_ref[h] for h in hs]
    kn = [kn_ref[pad:, h * HEAD_W:(h + 1) * HEAD_W] for h in hs]
    vn = [vn_ref[pad:, h * HEAD_W:(h + 1) * HEAD_W] for h in hs]
    s_new = [[jnp.sum(qs[h] * kn[h][t:t + 1, :], axis=-1, keepdims=True) + nbias_ref[h, t] for t in range(n_real)]
             for h in hs]
    m = [functools.reduce(jnp.maximum, s_new[h], jnp.max(s[h], axis=-1, keepdims=True)) for h in hs]
    p = [jnp.exp(s[h] - m[h]) for h in hs]
    p_new = [[jnp.exp(sn - m[h]) for sn in s_new[h]] for h in hs]
    l = [sum(p_new[h], jnp.sum(p[h], axis=-1, keepdims=True)) for h in hs]
    acc = [_dot(p[h], vbuf[slot, head_rows[h], :]) for h in hs]
    outs = []
    for h in hs:
        a = acc[h]
        for t in range(n_real):
            a = a + p_new[h][t] * vn[h][t:t + 1, :]
        o = a / l[h]
        o = o[:n_real] - lam * o[n_real:]
        outs.append(_rms_norm(o, nw_ref[...]) * out_scale)
    o_ref[0:pad, :] = jnp.zeros((pad, BR_W), F32)
    o_ref[pad:, :] = jnp.concatenate(outs, axis=1)


def _diff_sample(page_table, scalars, proj, k_new, v_new, cache_k, cache_v, layer, bias, nbias, norm_w, n_real):
    n_batch, n_pages = page_table.shape
    n_past = n_pages * PAGE_SIZE
    new_tok = pl.BlockSpec((None, SAMPLE_PAD, BR_W), lambda b, pt: (layer, b, 0))
    grid_spec = pltpu.PrefetchScalarGridSpec(
        num_scalar_prefetch=1,
        grid=(n_batch,),
        in_specs=[pl.BlockSpec(memory_space=pltpu.SMEM),
                  pl.BlockSpec((SAMPLE_PAD, BR_W), lambda b, pt: (b, BLK_CQ)),
                  new_tok, new_tok,
                  pl.BlockSpec(memory_space=pl.ANY), pl.BlockSpec(memory_space=pl.ANY),
                  pl.BlockSpec(bias.shape, lambda b, pt: (0, 0, 0)),
                  pl.BlockSpec(nbias.shape, lambda b, pt: (0, 0, 0, 0)),
                  pl.BlockSpec((1, HEAD_W), lambda b, pt: (0, 0))],
        out_specs=pl.BlockSpec((SAMPLE_PAD, BR_W), lambda b, pt: (b, 0)),
        scratch_shapes=[pltpu.VMEM((2, n_past * N_HEADS, HEAD_W), F32), pltpu.VMEM((2, n_past * N_HEADS, HEAD_W), F32),
                        pltpu.SemaphoreType.DMA((2, 2))],
    )
    return pl.pallas_call(
        functools.partial(_diff_sample_body, layer=layer, n_pages=n_pages, n_real=n_real, n_batch=n_batch),
        grid_spec=grid_spec,
        out_shape=jax.ShapeDtypeStruct((n_batch * SAMPLE_PAD, BR_W), F32),
        compiler_params=_cparams(1),
        name="diff_sample",
    )(page_table, scalars, proj, k_new, v_new, cache_k, cache_v, bias, nbias, norm_w)


def _merge_body(oa_ref, ob_ref, oc_ref, g0_ref, g1_ref, g2_ref, x_ref, wb_ref, wo_ref, ln_ref, o_ref):
    m = (_sigmoid(g0_ref[...]) * _dot(oa_ref[...], wb_ref[0])
         + _sigmoid(g1_ref[...]) * _dot(ob_ref[...], wb_ref[1])
         + _sigmoid(g2_ref[...]) * _dot(oc_ref[...], wb_ref[2]))
    y = _dot(m, wo_ref[...])
    o_ref[...] = _layer_norm(ALPHA * x_ref[...] + y, ln_ref[0:1, :], ln_ref[1:2, :])


def _merge(oa, ob, oc, proj, x, w_branch, w_out, ln):
    t = x.shape[0]
    tm = min(ROW_TILE, t)
    br = pl.BlockSpec((tm, BR_W), lambda i: (i, 0))

    def gate(n):
        return pl.BlockSpec((tm, D_MODEL), lambda i: (i, n))

    return pl.pallas_call(
        _merge_body,
        grid=(t // tm,),
        in_specs=[br, br, br, gate(0), gate(1), gate(2),
                  pl.BlockSpec((tm, D_MODEL), lambda i: (i, 0)),
                  pl.BlockSpec(w_branch.shape, lambda i: (0, 0, 0)),
                  pl.BlockSpec(w_out.shape, lambda i: (0, 0)),
                  pl.BlockSpec((SUBLANES, D_MODEL), lambda i: (0, 0))],
        out_specs=pl.BlockSpec((tm, D_MODEL), lambda i: (i, 0)),
        out_shape=jax.ShapeDtypeStruct((t, D_MODEL), F32),
        compiler_params=_cparams(1),
        name="merge",
    )(oa, ob, oc, proj, proj, proj, x, w_branch, w_out, ln)


def _attend_memory(qs, kvs):
    n = range(len(qs))
    s = [_dot_nt(qs[i], kvs[i][0]) * HEAD_W ** -0.5 for i in n]
    p = [jnp.exp(s[i] - jnp.max(s[i], axis=-1, keepdims=True)) for i in n]
    p = [p[i] / jnp.sum(p[i], axis=-1, keepdims=True) for i in n]
    return [_dot(p[i], kvs[i][1]) for i in n]


def _merge_xattn_prompt_body(oa_ref, ob_ref, oc_ref, g0_ref, g1_ref, g2_ref, x_ref, wb_ref, wo_ref, ln0_ref,
                             mk_ref, mv_ref, wq_ref, wxo_ref, ln1_ref, o_ref):
    m = (_sigmoid(g0_ref[...]) * _dot(oa_ref[...], wb_ref[0])
         + _sigmoid(g1_ref[...]) * _dot(ob_ref[...], wb_ref[1])
         + _sigmoid(g2_ref[...]) * _dot(oc_ref[...], wb_ref[2]))
    x = _layer_norm(ALPHA * x_ref[...] + _dot(m, wo_ref[...]), ln0_ref[0:1, :], ln0_ref[1:2, :])
    mk, mv = mk_ref[0], mv_ref[0]
    q = _dot(x, wq_ref[...])
    o = jnp.concatenate(_attend_memory([_head(q, h) for h in range(N_HEADS)],
                                       [(_head(mk, h), _head(mv, h)) for h in range(N_HEADS)]), axis=1)
    o_ref[...] = _layer_norm(ALPHA * x + _dot(o, wxo_ref[...]), ln1_ref[0:1, :], ln1_ref[1:2, :])


def _merge_xattn_prompt(oa, ob, oc, proj, x, w_branch, w_out, ln0, mkv, w_xq, w_xo, ln1, batch, seq):
    tm = min(ROW_TILE, seq)
    per = seq // tm
    n_mem = mkv.shape[1]

    def rows(width, col):
        return pl.BlockSpec((tm, width), lambda b, i: (b * per + i, col))

    def whole(a):
        return pl.BlockSpec(a.shape, lambda b, i: (0,) * a.ndim)

    ln = pl.BlockSpec((SUBLANES, D_MODEL), lambda b, i: (0, 0))
    return pl.pallas_call(
        _merge_xattn_prompt_body,
        grid=(batch, per),
        in_specs=[rows(BR_W, 0), rows(BR_W, 0), rows(BR_W, 0),
                  rows(D_MODEL, 0), rows(D_MODEL, 1), rows(D_MODEL, 2), rows(D_MODEL, 0),
                  whole(w_branch), whole(w_out), ln,
                  pl.BlockSpec((1, n_mem, BR_W), lambda b, i: (b, 0, 0)),
                  pl.BlockSpec((1, n_mem, BR_W), lambda b, i: (b, 0, 1)),
                  whole(w_xq), whole(w_xo), ln],
        out_specs=rows(D_MODEL, 0),
        out_shape=jax.ShapeDtypeStruct(x.shape, F32),
        compiler_params=_cparams(2),
        name="merge_xattn_prompt",
    )(oa, ob, oc, proj, proj, proj, x, w_branch, w_out, ln0, mkv, mkv, w_xq, w_xo, ln1)


def _xattn_sample_body(x_ref, mk_ref, mv_ref, wq_ref, wo_ref, ln_ref, o_ref, att_ref, *, n_seq):
    x = x_ref[...]
    q = _dot(x, wq_ref[...])
    n_mem = mk_ref.shape[1] // N_HEADS
    tiles = [(i, h) for i in range(n_seq) for h in range(N_HEADS)]
    qs, kvs = [], []
    for i, h in tiles:
        rows = pl.ds(h, n_mem, stride=N_HEADS)
        qs.append(_head(q[i * SAMPLE_PAD:(i + 1) * SAMPLE_PAD], h))
        kvs.append((mk_ref[i, rows, :], mv_ref[i, rows, :]))
    for (i, h), o in zip(tiles, _attend_memory(qs, kvs)):
        att_ref[i * SAMPLE_PAD:(i + 1) * SAMPLE_PAD, h * HEAD_W:(h + 1) * HEAD_W] = o
    o_ref[...] = _layer_norm(ALPHA * x + _dot(att_ref[...], wo_ref[...]), ln_ref[0:1, :], ln_ref[1:2, :])


def _xattn_sample(x, mem_k, mem_v, layer, w_xq, w_xo, ln, seq_per_step=8):
    n_batch, mem_rows = mem_k.shape[1:3]
    rows = seq_per_step * SAMPLE_PAD
    mem = pl.BlockSpec((None, seq_per_step, mem_rows, HEAD_W), lambda g: (layer, g, 0, 0))
    return pl.pallas_call(
        functools.partial(_xattn_sample_body, n_seq=seq_per_step),
        grid=(n_batch // seq_per_step,),
        in_specs=[pl.BlockSpec((rows, D_MODEL), lambda g: (g, 0)), mem, mem,
                  pl.BlockSpec(w_xq.shape, lambda g: (0, 0)),
                  pl.BlockSpec(w_xo.shape, lambda g: (0, 0)),
                  pl.BlockSpec((SUBLANES, D_MODEL), lambda g: (0, 0))],
        out_specs=pl.BlockSpec((rows, D_MODEL), lambda g: (g, 0)),
        out_shape=jax.ShapeDtypeStruct(x.shape, F32),
        scratch_shapes=[pltpu.VMEM((rows, BR_W), F32)],
        compiler_params=_cparams(1),
        name="xattn_sample",
    )(x, mem_k, mem_v, w_xq, w_xo, ln)


def _ffn_body(x_ref, wg_ref, wu_ref, w2_ref, ln_ref, o_ref, acc_ref, *, n_f):
    f = pl.program_id(1)
    x = x_ref[...]
    xb = x.astype(BF16)
    gate = jnp.dot(xb, wg_ref[...], preferred_element_type=F32)
    up = jnp.dot(xb, wu_ref[...], preferred_element_type=F32)
    part = _dot(_silu(gate) * up, w2_ref[...])

    @pl.when(f == 0)
    def _():
        acc_ref[...] = part

    @pl.when(f > 0)
    def _():
        acc_ref[...] += part

    @pl.when(f == n_f - 1)
    def _():
        o_ref[...] = _layer_norm(ALPHA * x + acc_ref[...], ln_ref[0:1, :], ln_ref[1:2, :])


def _ffn(x, w1, w2, ln, tf):
    t = x.shape[0]
    tm = min(ROW_TILE, t)
    n_f = D_FF // tf
    return pl.pallas_call(
        functools.partial(_ffn_body, n_f=n_f),
        grid=(t // tm, n_f),
        in_specs=[pl.BlockSpec((tm, D_MODEL), lambda i, f: (i, 0)),
                  pl.BlockSpec((D_MODEL, tf), lambda i, f: (0, f)),
                  pl.BlockSpec((D_MODEL, tf), lambda i, f: (0, n_f + f)),
                  pl.BlockSpec((tf, D_MODEL), lambda i, f: (f, 0)),
                  pl.BlockSpec((SUBLANES, D_MODEL), lambda i, f: (0, 0))],
        out_specs=pl.BlockSpec((tm, D_MODEL), lambda i, f: (i, 0)),
        out_shape=jax.ShapeDtypeStruct(x.shape, F32),
        scratch_shapes=[pltpu.VMEM((tm, D_MODEL), F32)],
        compiler_params=_cparams(2),
        name="ffn",
    )(x, w1, w1, w2, ln)


def _split_w_in(w):
    ab0 = 8 * BR_W
    c0 = ab0 + 2 * N_HEADS
    g0 = c0 + 3 * BR_W
    pad = jnp.zeros((w.shape[0], LANES - 2 * N_HEADS), w.dtype)
    main = jnp.concatenate([w[:, g0:], w[:, :ab0], w[:, c0:c0 + BR_W]], axis=1)
    kv = jnp.concatenate([w[:, c0 + BR_W:g0], w[:, ab0:c0], pad], axis=1)
    return main.astype(BF16), kv.astype(BF16)


def _ln_rows(g, b):
    return jnp.concatenate([g[None], b[None], jnp.zeros((SUBLANES - 2, g.shape[0]), F32)], axis=0)


def kernel(x_prompt, x_sample, mem_prompt, cache_attn_k, cache_attn_v, cache_mem_k, cache_mem_v, state_hgrn,
           state_gdn, state_gdn_conv, page_table, w_in, w_branch, w_out, lower_bounds, hgrn_norm, gdn_a_log,
           gdn_dt_bias, gdn_conv_w, gdn_norm, diff_lambda, diff_norm, rel_bias, w_xq, w_mem_kv, w_xo, ln_g, ln_b,
           w_ffn_in, w_ffn_out):
    bp, seq, _ = x_prompt.shape
    bs, ls, _ = x_sample.shape
    n_mem = mem_prompt.shape[1]
    n_pages = page_table.shape[1]
    n_past = n_pages * PAGE_SIZE
    n_pool = cache_attn_k.shape[1]
    pad = SAMPLE_PAD - ls

    lb_cum = jnp.cumsum(jax.nn.softmax(lower_bounds.astype(F32), axis=0), axis=0)
    lb = (lb_cum - lb_cum[0]).reshape(DEPTH, N_HEADS, 1, HEAD_W)
    lbp = jnp.concatenate([jnp.log(lb), jnp.log1p(-lb), 1.0 - lb,
                           jnp.zeros((DEPTH, N_HEADS, SUBLANES - 3, HEAD_W), F32)], axis=2)

    n_max = max(2 * ATT_BLOCK + 2, n_past + ls)
    bucket = t5_bucket(-jnp.arange(n_max, dtype=jnp.int32))
    thr = jnp.searchsorted(bucket, jnp.arange(N_BUCKETS, dtype=jnp.int32), side="left").astype(jnp.int32)
    rel_bias_t = rel_bias.astype(F32).T
    tiles = _bias_tiles(thr, rel_bias_t)
    dist = rel_bias_t[:, bucket]
    far_bias = dist[:, ATT_BLOCK + 1]
    q_pos = n_past + jnp.arange(ls)
    past = dist[:, q_pos[:, None] - jnp.arange(n_past)[None, :]]
    bias_s = jnp.concatenate([past, past], axis=1)
    tt = jnp.arange(ls)[:, None] - jnp.arange(ls)[None, :]
    new = jnp.where(tt >= 0, dist[:, jnp.maximum(tt, 0)], NEG_INF)
    nbias = jnp.concatenate([new, new], axis=1).transpose(0, 2, 1)[..., None]

    xp = x_prompt.reshape(bp * seq, D_MODEL)
    xs = jnp.concatenate([jnp.zeros((bs, pad, D_MODEL), F32), x_sample], axis=1).reshape(bs * SAMPLE_PAD, D_MODEL)
    mem = mem_prompt.reshape(bp * n_mem, D_MODEL)
    cache_k = cache_attn_k.reshape(DEPTH, n_pool, PAGE_SIZE * N_HEADS, HEAD_W)
    cache_v = cache_attn_v.reshape(DEPTH, n_pool, PAGE_SIZE * N_HEADS, HEAD_W)
    mem_k = cache_mem_k.reshape(DEPTH, bs, n_mem * N_HEADS, HEAD_W)
    mem_v = cache_mem_v.reshape(DEPTH, bs, n_mem * N_HEADS, HEAD_W)

    pk = pv = jnp.zeros((DEPTH, bp * seq, BR_W), F32)
    sk = sv = jnp.zeros((DEPTH, bs * SAMPLE_PAD, BR_W), F32)
    sh = sg = jnp.zeros(state_hgrn.shape, F32)
    pmk, pmv, ph, pg, pc, sc = [], [], [], [], [], []
    b_col0 = BLK_B * BR_W

    for l in range(DEPTH):
        w_main, w_kv = _split_w_in(w_in[l])
        wb_l = w_branch[l].astype(BF16)
        wo_l = w_out[l].astype(BF16)
        wq_l = w_xq[l].astype(BF16)
        wxo_l = w_xo[l].astype(BF16)
        wkv_l = w_mem_kv[l].astype(BF16)
        w1_l = w_ffn_in[l].astype(BF16)
        w2_l = w_ffn_out[l].astype(BF16)
        lns = [_ln_rows(ln_g[l, i], ln_b[l, i]) for i in range(3)]
        hn = hgrn_norm[l][None]
        gn = gdn_norm[l][None]
        dn = diff_norm[l][None]
        head_params = jnp.stack([jnp.exp(gdn_a_log[l].astype(F32)), gdn_dt_bias[l].astype(F32)])
        lam_init = 0.8 - 0.6 * math.exp(-0.3 * l)
        lq1, lk1, lq2, lk2 = diff_lambda[l].astype(F32)
        lam = jnp.exp(jnp.sum(lq1 * lk1)) - jnp.exp(jnp.sum(lq2 * lk2)) + lam_init
        scalars = jnp.stack([jnp.stack([lam, jnp.float32(1.0 - lam_init), jnp.float32(0), jnp.float32(0)]),
                             far_bias])
        conv_w_l = gdn_conv_w[l]

        proj = _matmul(xp, w_main, ROW_TILE, MAIN_W // 3)
        pk, pv, ab = _kv_proj(xp, w_kv, l, pk, pv)
        oa, h_new = _hgrn_prompt(proj, lbp[l], hn, bp, seq)
        qkv = _gdn_conv(proj, BLK_B, conv_w_l, SEQ_TILE, seq)
        ob, g_new = _gdn_prompt(qkv, proj, ab, head_params, gn, bp, seq)
        oc = _diff_prompt(proj, pk, pv, l, scalars, tiles, dn, bp, seq)
        mkv = _matmul(mem, wkv_l, ROW_TILE, 2 * BR_W).reshape(bp, n_mem, 2 * BR_W)
        x2 = _merge_xattn_prompt(oa, ob, oc, proj, xp, wb_l, wo_l, lns[0], mkv, wq_l, wxo_l, lns[1], bp, seq)
        xp = _ffn(x2, w1_l, w2_l, lns[2], D_FF // 2)
        pmk.append(mkv[:, :, :BR_W].reshape(bp, n_mem, N_HEADS, HEAD_W))
        pmv.append(mkv[:, :, BR_W:].reshape(bp, n_mem, N_HEADS, HEAD_W))
        ph.append(h_new)
        pg.append(g_new)
        pc.append(proj.reshape(bp, seq, MAIN_W)[:, seq - (CONV_W - 1):, b_col0:b_col0 + CONV_CH])

        proj_s = _matmul(xs, w_main, ROW_TILE, MAIN_W // 3)
        sk, sv, ab_s = _kv_proj(xs, w_kv, l, sk, sv)
        oa, sh = _hgrn_sample(proj_s, lbp[l], hn, state_hgrn, l, sh, ls)
        conv_tok = proj_s[:, b_col0:b_col0 + CONV_CH].reshape(bs, SAMPLE_PAD, CONV_CH)
        conv_in = jnp.concatenate([conv_tok[:, :pad - (CONV_W - 1)], state_gdn_conv[l], conv_tok[:, pad:]], axis=1)
        qkv = _gdn_conv(conv_in.reshape(bs * SAMPLE_PAD, CONV_CH), 0, conv_w_l, bs * SAMPLE_PAD, SAMPLE_PAD)
        ob, sg = _gdn_sample(qkv, proj_s, ab_s, head_params, gn, state_gdn, l, sg, ls)
        oc = _diff_sample(page_table, scalars, proj_s, sk, sv, cache_k, cache_v, l, bias_s, nbias, dn, ls)
        x1 = _merge(oa, ob, oc, proj_s, xs, wb_l, wo_l, lns[0])
        x2 = _xattn_sample(x1, mem_k, mem_v, l, wq_l, wxo_l, lns[1])
        xs = _ffn(x2, w1_l, w2_l, lns[2], D_FF // 2)
        sc.append(conv_in[:, SAMPLE_PAD - (CONV_W - 1):])

    y_prompt = xp.reshape(bp, seq, D_MODEL)
    y_sample = xs.reshape(bs, SAMPLE_PAD, D_MODEL)[:, pad:]
    new_k_p = pk.reshape(DEPTH, bp, seq, N_HEADS, HEAD_W)
    new_v_p = pv.reshape(DEPTH, bp, seq, N_HEADS, HEAD_W)
    new_k_s = sk.reshape(DEPTH, bs, SAMPLE_PAD, N_HEADS, HEAD_W)[:, :, pad:]
    new_v_s = sv.reshape(DEPTH, bs, SAMPLE_PAD, N_HEADS, HEAD_W)[:, :, pad:]
    return (y_prompt, y_sample, new_k_p, new_v_p, jnp.stack(pmk), jnp.stack(pmv), jnp.stack(ph), jnp.stack(pg),
            jnp.stack(pc), new_k_s, new_v_s, sh, sg, jnp.stack(sc))
```

```python
import functools
import math

import jax
import jax.numpy as jnp
from jax import lax
from jax.experimental import pallas as pl
from jax.experimental.pallas import tpu as pltpu

F32 = jnp.float32
BF16 = jnp.bfloat16

D_MODEL = 1024
DEPTH = 4
PAGE_SIZE = 128
BR_W = D_MODEL // 2
N_HEADS = 4
HEAD_W = BR_W // N_HEADS
DH_C = HEAD_W // 2
CONV_W = 4
CONV_CH = 3 * BR_W
N_BRANCH = 3
N_BUCKETS = 32
MAX_DISTANCE = 128
D_FF = -(-8 * D_MODEL // (3 * 256)) * 256
LN_EPS = 1e-5
NORM_EPS = 1e-6
NEG_INF = -1e30
ALPHA = (2 * DEPTH) ** 0.25

LANES = 128
SUBLANES = 8
VMEM_LIMIT_BYTES = 56 * 1024 * 1024

BLK_A = N_BRANCH * D_MODEL // BR_W
BLK_B = BLK_A + 4
BLK_CQ = BLK_B + 4
MAIN_W = (BLK_CQ + 1) * BR_W
KV_W = 2 * BR_W + LANES

HGRN_SUB = 16
HGRN_ROWS = 64
GDN_CHUNK = 64
SOLVE_BLOCK = 16
SAMPLE_PAD = 8
SEQ_TILE = 512
GDN_TILE = 1024
ATT_BLOCK = 512
ATT_HEADS = 2
ROW_TILE = 512


def _cparams(n_axes):
    return pltpu.CompilerParams(dimension_semantics=("arbitrary",) * n_axes,
                                vmem_limit_bytes=VMEM_LIMIT_BYTES)


def _dot(a, b):
    return jnp.dot(a.astype(BF16), b.astype(BF16), preferred_element_type=F32)


def _dot_nt(a, b):
    return lax.dot_general(a.astype(BF16), b.astype(BF16), (((1,), (1,)), ((), ())),
                           preferred_element_type=F32)


def _dot_tn(a, b):
    return lax.dot_general(a.astype(BF16), b.astype(BF16), (((0,), (0,)), ((), ())),
                           preferred_element_type=F32)


def _dot_exact_lhs(m01, x):
    hi = x.astype(BF16)
    r1 = x - hi.astype(F32)
    mid = r1.astype(BF16)
    lo = (r1 - mid.astype(F32)).astype(BF16)
    dot = functools.partial(jnp.dot, preferred_element_type=F32)
    return dot(m01, hi) + dot(m01, mid) + dot(m01, lo)


def _sigmoid(x):
    return 1.0 / (1.0 + jnp.exp(-x))


def _silu(x):
    return x * _sigmoid(x)


def _softplus(x):
    return jnp.maximum(x, 0.0) + jnp.log1p(jnp.exp(-jnp.abs(x)))


def _layer_norm(h, g, b):
    mu = jnp.mean(h, axis=-1, keepdims=True)
    c = h - mu
    var = jnp.mean(c * c, axis=-1, keepdims=True)
    return c * lax.rsqrt(var + LN_EPS) * g + b


def _rms_norm(x, g):
    return x * lax.rsqrt(jnp.mean(x * x, axis=-1, keepdims=True) + NORM_EPS) * g


def _bcast_rows(x, rows, reps):
    return jnp.concatenate([jnp.broadcast_to(x[r:r + 1, :], (reps, x.shape[1])) for r in rows], axis=0)


def _head(x, h):
    return x[:, h * HEAD_W:(h + 1) * HEAD_W]


ALIASED = pl.BlockSpec(memory_space=pl.ANY)


def _mm_body(x_ref, w_ref, o_ref):
    o_ref[...] = _dot(x_ref[...], w_ref[...])


def _matmul(x, w, tm, tn):
    t, k = x.shape
    n = w.shape[1]
    tm = min(tm, t)
    return pl.pallas_call(
        _mm_body,
        grid=(n // tn, t // tm),
        in_specs=[pl.BlockSpec((tm, k), lambda j, i: (i, 0)),
                  pl.BlockSpec((k, tn), lambda j, i: (0, j))],
        out_specs=pl.BlockSpec((tm, tn), lambda j, i: (i, j)),
        out_shape=jax.ShapeDtypeStruct((t, n), F32),
        compiler_params=_cparams(2),
        name="proj_matmul",
    )(x, w)


def _kv_body(x_ref, w_ref, *refs):
    k_ref, v_ref, ab_ref = refs[-3:]
    y = _dot(x_ref[...], w_ref[...])
    k_ref[...] = y[:, :BR_W]
    v_ref[...] = y[:, BR_W:2 * BR_W]
    ab_ref[...] = y[:, 2 * BR_W:]


def _kv_proj(x, w, layer, k_prev, v_prev):
    t = x.shape[0]
    tm = min(ROW_TILE, t)
    slab = pl.BlockSpec((None, tm, BR_W), lambda i: (layer, i, 0))
    stacked = jax.ShapeDtypeStruct((DEPTH, t, BR_W), F32)
    return pl.pallas_call(
        _kv_body,
        grid=(t // tm,),
        in_specs=[pl.BlockSpec((tm, D_MODEL), lambda i: (i, 0)),
                  pl.BlockSpec(w.shape, lambda i: (0, 0)), ALIASED, ALIASED],
        out_specs=[slab, slab, pl.BlockSpec((tm, LANES), lambda i: (i, 0))],
        out_shape=[stacked, stacked, jax.ShapeDtypeStruct((t, LANES), F32)],
        input_output_aliases={2: 0, 3: 1},
        compiler_params=_cparams(1),
        name="kv_proj",
    )(x, w, k_prev, v_prev)


def _hgrn_rows(q, z, v, lbp, sub, masks, valid):
    r = q.shape[0]
    n_sub = r // sub
    tri, pos = masks
    log_lb, log1m_lb, om_lb = lbp[0:1, :], lbp[1:2, :], lbp[2:3, :]
    log_sig = jnp.minimum(z, 0.0) - jnp.log(1.0 + jnp.exp(-jnp.abs(z)))
    t2 = log1m_lb + log_sig
    logf = jnp.maximum(log_lb, t2) + jnp.log(1.0 + jnp.exp(-jnp.abs(log_lb - t2)))
    k = om_lb * (1.0 / (1.0 + jnp.exp(z)))
    if valid is not None:
        logf = jnp.where(valid, logf, 0.0)
        k = jnp.where(valid, k, 0.0)
    bc = _dot_exact_lhs(tri, logf)
    last_rows = [i * sub + sub - 1 for i in range(n_sub)]
    b_last = _bcast_rows(bc, last_rows, sub)
    q_hat = q * jnp.exp(bc)
    k_hat = k * jnp.exp(b_last - bc)
    o = jnp.zeros_like(q)
    for s in range(sub):
        rows = [i * sub + s for i in range(n_sub)]
        ks = _bcast_rows(k, rows, sub)
        bs = _bcast_rows(bc, rows, sub)
        vs = _bcast_rows(v, rows, sub)
        a = jnp.where(pos >= s, q * ks * jnp.exp(bc - bs), 0.0)
        o = o + jnp.sum(a, axis=-1, keepdims=True) * vs
    return q_hat, k_hat, o, bc


def _sub_chunk_masks(rows, sub):
    ri = lax.broadcasted_iota(jnp.int32, (rows, rows), 0)
    ci = lax.broadcasted_iota(jnp.int32, (rows, rows), 1)
    tri = ((ri // sub == ci // sub) & (ci <= ri)).astype(BF16)
    pos = lax.broadcasted_iota(jnp.int32, (rows, 1), 0) % sub
    return tri, pos


def _hgrn_finish(o, g, norm_w):
    return _rms_norm(o, norm_w) * _silu(g)


def _hgrn_prompt_body(q_ref, f_ref, i_ref, g_ref, lbp_ref, nw_ref, o_ref, s_out_ref, st_ref, *, n_tiles):
    t = pl.program_id(1)

    @pl.when(t == 0)
    def _():
        st_ref[...] = jnp.zeros_like(st_ref)

    rows = HGRN_ROWS
    nw = nw_ref[...]
    masks = _sub_chunk_masks(rows, HGRN_SUB)

    def chunk(c, carry):
        sl = pl.ds(pl.multiple_of(c * rows, rows), rows)
        qa, za, va, ga = q_ref[sl, :], f_ref[sl, :], i_ref[sl, :], g_ref[sl, :]
        hs = range(N_HEADS)
        v = [_head(va, h) for h in hs]
        parts = [_hgrn_rows(_head(qa, h), _head(za, h), v[h], lbp_ref[h], HGRN_SUB, masks, None) for h in hs]
        st = [st_ref[h] for h in hs]
        outs = [[] for _ in hs]
        for i in range(rows // HGRN_SUB):
            lo, hi = i * HGRN_SUB, (i + 1) * HGRN_SUB
            for h in hs:
                q_hat, k_hat, o, bc = parts[h]
                outs[h].append(o[lo:hi] + _dot_nt(q_hat[lo:hi], st[h]))
                st[h] = st[h] * jnp.exp(bc[hi - 1:hi, :]) + _dot_tn(v[h][lo:hi], k_hat[lo:hi])
        for h in hs:
            st_ref[h] = st[h]
        o_ref[sl, :] = jnp.concatenate(
            [_hgrn_finish(jnp.concatenate(outs[h], axis=0), _head(ga, h), nw) for h in hs], axis=1)
        return carry

    lax.fori_loop(0, SEQ_TILE // rows, chunk, 0)

    @pl.when(t == n_tiles - 1)
    def _():
        for h in range(N_HEADS):
            s_out_ref[0, h] = st_ref[h].T


def _hgrn_prompt(proj, lbp, norm_w, batch, seq):
    n_tiles = seq // SEQ_TILE

    def col(blk):
        return pl.BlockSpec((SEQ_TILE, BR_W), lambda b, t: (b * n_tiles + t, blk))

    return pl.pallas_call(
        functools.partial(_hgrn_prompt_body, n_tiles=n_tiles),
        grid=(batch, n_tiles),
        in_specs=[col(BLK_A), col(BLK_A + 1), col(BLK_A + 2), col(BLK_A + 3),
                  pl.BlockSpec((N_HEADS, SUBLANES, HEAD_W), lambda b, t: (0, 0, 0)),
                  pl.BlockSpec((1, HEAD_W), lambda b, t: (0, 0))],
        out_specs=[pl.BlockSpec((SEQ_TILE, BR_W), lambda b, t: (b * n_tiles + t, 0)),
                   pl.BlockSpec((1, N_HEADS, HEAD_W, HEAD_W), lambda b, t: (b, 0, 0, 0))],
        out_shape=[jax.ShapeDtypeStruct((batch * seq, BR_W), F32),
                   jax.ShapeDtypeStruct((batch, N_HEADS, HEAD_W, HEAD_W), F32)],
        scratch_shapes=[pltpu.VMEM((N_HEADS, HEAD_W, HEAD_W), F32)],
        compiler_params=_cparams(2),
        name="hgrn_prompt",
    )(proj, proj, proj, proj, lbp, norm_w)


def _hgrn_sample_body(q_ref, f_ref, i_ref, g_ref, lbp_ref, nw_ref, s0_ref, *refs, n_seq, n_real):
    o_ref, s_out_ref = refs[-2:]
    nw = nw_ref[...]
    rows = n_seq * SAMPLE_PAD
    masks = _sub_chunk_masks(rows, SAMPLE_PAD)
    valid = masks[1] >= SAMPLE_PAD - n_real
    qa, za, va, ga = q_ref[...], f_ref[...], i_ref[...], g_ref[...]
    hs = range(N_HEADS)
    v = [_head(va, h) for h in hs]
    parts = [_hgrn_rows(_head(qa, h), _head(za, h), v[h], lbp_ref[h], SAMPLE_PAD, masks, valid) for h in hs]
    eye = (lax.broadcasted_iota(jnp.int32, (HEAD_W, HEAD_W), 0)
           == lax.broadcasted_iota(jnp.int32, (HEAD_W, HEAD_W), 1))
    outs = [[] for _ in hs]
    for i in range(n_seq):
        lo, hi = i * SAMPLE_PAD, (i + 1) * SAMPLE_PAD
        for h in hs:
            q_hat, k_hat, o, bc = parts[h]
            s0 = s0_ref[i, h]
            outs[h].append(o[lo:hi] + _dot(q_hat[lo:hi], s0))
            decay = jnp.exp(bc[hi - 1:hi, :])
            decay_col = jnp.sum(jnp.where(eye, decay, 0.0), axis=-1, keepdims=True)
            s_out_ref[i, h] = s0 * decay_col + _dot_tn(k_hat[lo:hi], v[h][lo:hi])
    o_ref[...] = jnp.concatenate(
        [_hgrn_finish(jnp.concatenate(outs[h], axis=0), _head(ga, h), nw) for h in hs], axis=1)


def _state_specs(layer, seq_per_step):
    return pl.BlockSpec((None, seq_per_step, N_HEADS, HEAD_W, HEAD_W), lambda g: (layer, g, 0, 0, 0))


def _hgrn_sample(proj, lbp, norm_w, states, layer, prev, n_real, seq_per_step=8):
    n_batch = states.shape[1]
    rows = seq_per_step * SAMPLE_PAD

    def col(blk):
        return pl.BlockSpec((rows, BR_W), lambda g: (g, blk))

    return pl.pallas_call(
        functools.partial(_hgrn_sample_body, n_seq=seq_per_step, n_real=n_real),
        grid=(n_batch // seq_per_step,),
        in_specs=[col(BLK_A), col(BLK_A + 1), col(BLK_A + 2), col(BLK_A + 3),
                  pl.BlockSpec((N_HEADS, SUBLANES, HEAD_W), lambda g: (0, 0, 0)),
                  pl.BlockSpec((1, HEAD_W), lambda g: (0, 0)),
                  _state_specs(layer, seq_per_step), ALIASED],
        out_specs=[pl.BlockSpec((rows, BR_W), lambda g: (g, 0)), _state_specs(layer, seq_per_step)],
        out_shape=[jax.ShapeDtypeStruct((n_batch * SAMPLE_PAD, BR_W), F32),
                   jax.ShapeDtypeStruct(states.shape, F32)],
        input_output_aliases={7: 1},
        compiler_params=_cparams(1),
        name="hgrn_sample",
    )(proj, proj, proj, proj, lbp, norm_w, states, prev)


def _conv_body(prev_ref, cur_ref, w_ref, o_ref, buf_ref, *, tile, seq_len):
    i = pl.program_id(0)
    c = pl.program_id(1)
    buf_ref[0:SUBLANES, :] = prev_ref[...]
    buf_ref[SUBLANES:, :] = cur_ref[...]
    pos = (i * tile + lax.broadcasted_iota(jnp.int32, (tile, 1), 0)) % seq_len
    w = w_ref[...]
    acc = cur_ref[...] * w[CONV_W - 1:CONV_W, :]
    for back in range(1, CONV_W):
        shifted = buf_ref[pl.ds(SUBLANES - back, tile), :]
        acc = acc + jnp.where(pos >= back, shifted, 0.0) * w[CONV_W - 1 - back:CONV_W - back, :]
    act = _silu(acc)
    q_scale = jnp.where(c == 0, HEAD_W ** -0.5, 1.0)
    heads = []
    for h in range(N_HEADS):
        a = _head(act, h)
        inv = lax.rsqrt(jnp.sum(a * a, axis=-1, keepdims=True) + NORM_EPS)
        heads.append(a * jnp.where(c < 2, inv * q_scale, 1.0))
    o_ref[...] = jnp.concatenate(heads, axis=1)


def _gdn_conv(conv_in, col0, conv_w, tile, seq_len):
    t = conv_in.shape[0]
    per = tile // SUBLANES
    return pl.pallas_call(
        functools.partial(_conv_body, tile=tile, seq_len=seq_len),
        grid=(t // tile, CONV_CH // BR_W),
        in_specs=[pl.BlockSpec((SUBLANES, BR_W), lambda i, c: (jnp.maximum(i * per - 1, 0), col0 + c)),
                  pl.BlockSpec((tile, BR_W), lambda i, c: (i, col0 + c)),
                  pl.BlockSpec((CONV_W, BR_W), lambda i, c: (0, c))],
        out_specs=pl.BlockSpec((tile, BR_W), lambda i, c: (i, c)),
        out_shape=jax.ShapeDtypeStruct((t, CONV_CH), F32),
        scratch_shapes=[pltpu.VMEM((tile + SUBLANES, BR_W), F32)],
        compiler_params=_cparams(2),
        name="gdn_conv",
    )(conv_in, conv_in, conv_w)


def _dot_f32(a, b):
    ah = a.astype(BF16)
    al = (a - ah.astype(F32)).astype(BF16)
    bh = b.astype(BF16)
    bl = (b - bh.astype(F32)).astype(BF16)
    dot = functools.partial(jnp.dot, preferred_element_type=F32)
    return dot(ah, bh) + dot(ah, bl) + dot(al, bh)


def _forward_substitute(lmats, rhss):
    n_sys = len(lmats)
    c = lmats[0].shape[0]
    nb = min(c, SOLVE_BLOCK)
    done = [[] for _ in range(n_sys)]
    for i0 in range(0, c, nb):
        xbs, lbs = [], []
        for i in range(n_sys):
            r = rhss[i][i0:i0 + nb, :]
            if i0:
                r = r - _dot_f32(lmats[i][i0:i0 + nb, :i0], jnp.concatenate(done[i], axis=0))
            xbs.append([r[s:s + SUBLANES, :] for s in range(0, nb, SUBLANES)])
            lbs.append([lmats[i][i0 + s:i0 + s + SUBLANES, :] for s in range(0, nb, SUBLANES)])
        for j in range(nb - 1):
            for i in range(n_sys):
                xb = xbs[i]
                xj = xb[j // SUBLANES][j % SUBLANES:j % SUBLANES + 1, :]
                for b in range((j + 1) // SUBLANES, len(xb)):
                    xb[b] = xb[b] - lbs[i][b][:, i0 + j:i0 + j + 1] * xj
        for i in range(n_sys):
            done[i].extend(xbs[i])
    return [jnp.concatenate(d, axis=0) if len(d) > 1 else d[0] for d in done]


def _gdn_masks(c):
    ri = lax.broadcasted_iota(jnp.int32, (c, c), 0)
    ci = lax.broadcasted_iota(jnp.int32, (c, c), 1)
    mr = lax.broadcasted_iota(jnp.int32, (c, LANES), 0)
    mc = lax.broadcasted_iota(jnp.int32, (c, LANES), 1)
    return ci <= ri, ci < ri, ((mc < mr) & (mc < c)) | (mc == c)


def _gdn_prepare(hp_ref, qa, ka, va, ab, masks, valid):
    c = qa.shape[0]
    incl, strict, sel = masks
    hs = range(N_HEADS)
    lane = lax.broadcasted_iota(jnp.int32, ab.shape, 1)
    q = [_head(qa, h) for h in hs]
    k = [_head(ka, h) for h in hs]
    g, beta = [], []
    for h in hs:
        a_col = jnp.sum(jnp.where(lane == h, ab, 0.0), axis=-1, keepdims=True)
        b_col = jnp.sum(jnp.where(lane == h + N_HEADS, ab, 0.0), axis=-1, keepdims=True)
        g_h = -hp_ref[0, h] * _softplus(a_col + hp_ref[1, h])
        beta_h = _sigmoid(b_col)
        if valid is not None:
            g_h = jnp.where(valid, g_h, 0.0)
            beta_h = jnp.where(valid, beta_h, 0.0)
        g.append(g_h)
        beta.append(beta_h)
    incl_b = incl.astype(BF16)
    m = [_dot_exact_lhs(incl_b, jnp.where(sel, g[h], 0.0)) for h in hs]
    gcum = [m[h][:, c:c + 1] for h in hs]
    decay = [jnp.where(incl, jnp.exp(jnp.where(incl, m[h][:, :c], 0.0)), 0.0) for h in hs]
    kb = [k[h] * beta[h] for h in hs]
    lmat = [jnp.where(strict, _dot_nt(kb[h], k[h]) * decay[h], 0.0) for h in hs]
    e_g = [jnp.exp(gcum[h]) for h in hs]
    sol = _forward_substitute(lmat, [jnp.concatenate([_head(va, h) * beta[h], kb[h] * e_g[h]], axis=1) for h in hs])
    attn = [jnp.where(incl, _dot_nt(q[h], k[h]) * decay[h], 0.0) for h in hs]
    out = []
    for h in hs:
        g_last = gcum[h][c - 1:c, :]
        out.append((sol[h][:, :HEAD_W], sol[h][:, HEAD_W:], q[h] * e_g[h], k[h] * jnp.exp(g_last - gcum[h]),
                    attn[h], jnp.exp(g_last)))
    return out


def _gdn_apply(prep, ga, nw, get_state, put_state):
    hs = range(N_HEADS)
    s = [get_state(h) for h in hs]
    v_new = [prep[h][0] - _dot(prep[h][1], s[h]) for h in hs]
    o = [_dot(prep[h][2], s[h]) + _dot(prep[h][4], v_new[h]) for h in hs]
    for h in hs:
        put_state(h, s[h] * prep[h][5] + _dot_tn(prep[h][3], v_new[h]))
    return jnp.concatenate([_rms_norm(o[h], nw) * _silu(_head(ga, h)) for h in hs], axis=1)


def _gdn_prompt_body(hp_ref, q_ref, k_ref, v_ref, g_ref, ab_ref, nw_ref, o_ref, s_out_ref, s_ref, pre_ref, last_ref,
                     *, n_tiles):
    t = pl.program_id(1)

    @pl.when(t == 0)
    def _():
        s_ref[...] = jnp.zeros_like(s_ref)

    nw = nw_ref[...]
    masks = _gdn_masks(GDN_CHUNK)
    n_chunks = q_ref.shape[0] // GDN_CHUNK

    def put(h, s_new):
        s_ref[h] = s_new

    def rows_of(c):
        return pl.ds(pl.multiple_of(c * GDN_CHUNK, GDN_CHUNK), GDN_CHUNK)

    def prepare_into(c, slot):
        sl = rows_of(c)
        prep = _gdn_prepare(hp_ref, q_ref[sl, :], k_ref[sl, :], v_ref[sl, :], ab_ref[sl, :], masks, None)
        for h, (u, w, qe, ke, attn, e_last) in enumerate(prep):
            for i, x in enumerate((u, w, qe, ke)):
                pre_ref[slot, h, i] = x
            pre_ref[slot, h, 4, :, :GDN_CHUNK] = attn
            last_ref[slot, h] = jnp.broadcast_to(e_last, (SUBLANES, LANES))

    prepare_into(0, 0)

    def chunk(c, carry):
        slot = c % 2
        prep = [(pre_ref[slot, h, 0], pre_ref[slot, h, 1], pre_ref[slot, h, 2], pre_ref[slot, h, 3],
                 pre_ref[slot, h, 4, :, :GDN_CHUNK], last_ref[slot, h, 0:1, 0:1]) for h in range(N_HEADS)]
        sl = rows_of(c)
        o_ref[sl, :] = _gdn_apply(prep, g_ref[sl, :], nw, lambda h: s_ref[h], put)
        prepare_into(jnp.minimum(c + 1, n_chunks - 1), 1 - slot)
        return carry

    lax.fori_loop(0, n_chunks, chunk, 0)

    @pl.when(t == n_tiles - 1)
    def _():
        s_out_ref[0] = s_ref[...]


def _gdn_prompt(qkv, proj, ab, head_params, norm_w, batch, seq):
    tile = min(GDN_TILE, seq)
    n_tiles = seq // tile

    def col(blk):
        return pl.BlockSpec((tile, BR_W), lambda b, t: (b * n_tiles + t, blk))

    return pl.pallas_call(
        functools.partial(_gdn_prompt_body, n_tiles=n_tiles),
        grid=(batch, n_tiles),
        in_specs=[pl.BlockSpec(memory_space=pltpu.SMEM),
                  col(0), col(1), col(2), col(BLK_B + 3),
                  pl.BlockSpec((tile, LANES), lambda b, t: (b * n_tiles + t, 0)),
                  pl.BlockSpec((1, HEAD_W), lambda b, t: (0, 0))],
        out_specs=[pl.BlockSpec((tile, BR_W), lambda b, t: (b * n_tiles + t, 0)),
                   pl.BlockSpec((1, N_HEADS, HEAD_W, HEAD_W), lambda b, t: (b, 0, 0, 0))],
        out_shape=[jax.ShapeDtypeStruct((batch * seq, BR_W), F32),
                   jax.ShapeDtypeStruct((batch, N_HEADS, HEAD_W, HEAD_W), F32)],
        scratch_shapes=[pltpu.VMEM((N_HEADS, HEAD_W, HEAD_W), F32),
                        pltpu.VMEM((2, N_HEADS, 5, GDN_CHUNK, HEAD_W), F32),
                        pltpu.VMEM((2, N_HEADS, SUBLANES, LANES), F32)],
        compiler_params=_cparams(2),
        name="gdn_prompt",
    )(head_params, qkv, qkv, qkv, proj, ab, norm_w)


def _gdn_sample_body(hp_ref, q_ref, k_ref, v_ref, g_ref, ab_ref, nw_ref, s0_ref, *refs, n_seq, n_real):
    o_ref, s_out_ref = refs[-2:]
    nw = nw_ref[...]
    masks = _gdn_masks(SAMPLE_PAD)
    valid = lax.broadcasted_iota(jnp.int32, (SAMPLE_PAD, 1), 0) >= SAMPLE_PAD - n_real

    def one(i, carry):
        sl = pl.ds(pl.multiple_of(i * SAMPLE_PAD, SAMPLE_PAD), SAMPLE_PAD)

        def put(h, s_new):
            s_out_ref[i, h] = s_new

        prep = _gdn_prepare(hp_ref, q_ref[sl, :], k_ref[sl, :], v_ref[sl, :], ab_ref[sl, :], masks, valid)
        o_ref[sl, :] = _gdn_apply(prep, g_ref[sl, :], nw, lambda h: s0_ref[i, h], put)
        return carry

    lax.fori_loop(0, n_seq, one, 0)


def _gdn_sample(qkv, proj, ab, head_params, norm_w, states, layer, prev, n_real, seq_per_step=8):
    n_batch = states.shape[1]
    rows = seq_per_step * SAMPLE_PAD

    def col(blk):
        return pl.BlockSpec((rows, BR_W), lambda g: (g, blk))

    return pl.pallas_call(
        functools.partial(_gdn_sample_body, n_seq=seq_per_step, n_real=n_real),
        grid=(n_batch // seq_per_step,),
        in_specs=[pl.BlockSpec(memory_space=pltpu.SMEM),
                  col(0), col(1), col(2), col(BLK_B + 3),
                  pl.BlockSpec((rows, LANES), lambda g: (g, 0)),
                  pl.BlockSpec((1, HEAD_W), lambda g: (0, 0)),
                  _state_specs(layer, seq_per_step), ALIASED],
        out_specs=[pl.BlockSpec((rows, BR_W), lambda g: (g, 0)), _state_specs(layer, seq_per_step)],
        out_shape=[jax.ShapeDtypeStruct((n_batch * SAMPLE_PAD, BR_W), F32),
                   jax.ShapeDtypeStruct(states.shape, F32)],
        input_output_aliases={8: 1},
        compiler_params=_cparams(1),
        name="gdn_sample",
    )(head_params, qkv, qkv, qkv, proj, ab, norm_w, states, prev)


def t5_bucket(rel):
    n = jnp.maximum(-rel, 0)
    exact = N_BUCKETS // 2
    log_part = jnp.log(jnp.maximum(n, 1).astype(F32) / exact) / math.log(MAX_DISTANCE / exact)
    large = jnp.minimum(exact + (log_part * (N_BUCKETS - exact)).astype(jnp.int32), N_BUCKETS - 1)
    return jnp.where(n < exact, n, large)


def _bias_tile_body(thr_ref, rb_ref, o_ref):
    h = pl.program_id(0)
    blk = ATT_BLOCK
    row = lax.broadcasted_iota(jnp.int32, (blk, blk), 0)
    col = lax.broadcasted_iota(jnp.int32, (blk, blk), 1)
    for which in range(2):
        n = row - col + which * blk
        val = jnp.full((blk, blk), rb_ref[h, 0], F32)
        for b in range(1, N_BUCKETS):
            val = jnp.where(n >= thr_ref[b], rb_ref[h, b], val)
        if which == 0:
            val = jnp.where(n >= 0, val, NEG_INF)
        o_ref[which] = val


def _bias_tiles(thr, rel_bias_t):
    return pl.pallas_call(
        _bias_tile_body,
        grid=(N_HEADS,),
        in_specs=[pl.BlockSpec(memory_space=pltpu.SMEM), pl.BlockSpec(memory_space=pltpu.SMEM)],
        out_specs=pl.BlockSpec((None, 2, ATT_BLOCK, ATT_BLOCK), lambda h: (h, 0, 0, 0)),
        out_shape=jax.ShapeDtypeStruct((N_HEADS, 2, ATT_BLOCK, ATT_BLOCK), F32),
        compiler_params=_cparams(1),
        name="bias_tiles",
    )(thr, rel_bias_t)


def _stack_maps(q):
    lane = lax.broadcasted_iota(jnp.int32, q.shape, 1)
    return jnp.concatenate([jnp.where(lane < DH_C, q, 0.0), jnp.where(lane >= DH_C, q, 0.0)], axis=0)


def _diff_prompt_body(sc_ref, q_ref, k_ref, v_ref, d_ref, nw_ref, o_ref, k16_ref, v16_ref, m_ref, acc_ref):
    hp = pl.program_id(1)
    qi = pl.program_id(2)
    blk = ATT_BLOCK
    seq = k16_ref.shape[1]

    @pl.when(qi == 0)
    def _():
        for hh in range(ATT_HEADS):
            k16_ref[hh] = _head(k_ref[...], hh).astype(BF16)
            v16_ref[hh, :, :HEAD_W] = _head(v_ref[...], hh).astype(BF16)
            v16_ref[hh, :, HEAD_W:] = jnp.ones((seq, HEAD_W), BF16)

    lam = sc_ref[0, 0]
    out_scale = sc_ref[0, 1]
    qs = [_stack_maps(_head(q_ref[...], hh) * DH_C ** -0.5).astype(BF16) for hh in range(ATT_HEADS)]
    m_ref[...] = jnp.full_like(m_ref, -jnp.inf)
    acc_ref[...] = jnp.zeros_like(acc_ref)

    def block(kj, bias_of):
        sl = pl.ds(pl.multiple_of(kj * blk, blk), blk)
        for hh in range(ATT_HEADS):
            s = lax.dot_general(qs[hh], k16_ref[hh, sl, :], (((1,), (1,)), ((), ())), preferred_element_type=F32)
            bias = bias_of(hh)
            if bias.ndim == 2:
                s = (s.reshape(2, blk, blk) + bias[None]).reshape(2 * blk, blk)
            else:
                s = s + bias
            m_old = m_ref[hh]
            m_new = jnp.maximum(m_old, jnp.max(s, axis=-1, keepdims=True))
            alpha = jnp.exp(m_old - m_new)
            p = jnp.exp(s - jnp.concatenate([m_new] * (blk // LANES), axis=1))
            pv = jnp.dot(p.astype(BF16), v16_ref[hh, sl, :], preferred_element_type=F32)
            acc_ref[hh] = jnp.concatenate([alpha, alpha], axis=1) * acc_ref[hh] + pv
            m_ref[hh] = m_new

    def far(kj, carry):
        block(kj, lambda hh: sc_ref[1, hp * ATT_HEADS + hh])
        return carry

    lax.fori_loop(0, jnp.maximum(qi - 1, 0), far, 0)

    @pl.when(qi >= 1)
    def _():
        block(qi - 1, lambda hh: d_ref[hh, 1])

    block(qi, lambda hh: d_ref[hh, 0])
    outs = []
    for hh in range(ATT_HEADS):
        acc = acc_ref[hh]
        o = acc[:, :HEAD_W] / acc[:, HEAD_W:]
        o = o[:blk] - lam * o[blk:]
        outs.append(_rms_norm(o, nw_ref[...]) * out_scale)
    o_ref[...] = jnp.concatenate(outs, axis=1)


def _diff_prompt(proj, k_all, v_all, layer, scalars, tiles, norm_w, batch, seq):
    nq = seq // ATT_BLOCK
    width = ATT_HEADS * HEAD_W
    per_row = BR_W // width
    kv = pl.BlockSpec((None, seq, width), lambda b, h, q: (layer, b, h))
    return pl.pallas_call(
        _diff_prompt_body,
        grid=(batch, N_HEADS // ATT_HEADS, nq),
        in_specs=[pl.BlockSpec(memory_space=pltpu.SMEM),
                  pl.BlockSpec((ATT_BLOCK, width), lambda b, h, q: (b * nq + q, BLK_CQ * per_row + h)),
                  kv, kv,
                  pl.BlockSpec((ATT_HEADS, 2, ATT_BLOCK, ATT_BLOCK), lambda b, h, q: (h, 0, 0, 0)),
                  pl.BlockSpec((1, HEAD_W), lambda b, h, q: (0, 0))],
        out_specs=pl.BlockSpec((ATT_BLOCK, width), lambda b, h, q: (b * nq + q, h)),
        out_shape=jax.ShapeDtypeStruct((batch * seq, BR_W), F32),
        scratch_shapes=[pltpu.VMEM((ATT_HEADS, seq, HEAD_W), BF16), pltpu.VMEM((ATT_HEADS, seq, 2 * HEAD_W), BF16),
                        pltpu.VMEM((ATT_HEADS, 2 * ATT_BLOCK, LANES), F32),
                        pltpu.VMEM((ATT_HEADS, 2 * ATT_BLOCK, 2 * HEAD_W), F32)],
        compiler_params=_cparams(3),
        name="diff_prompt",
    )(scalars, proj, k_all, v_all, tiles, norm_w)


def _diff_sample_body(pt_ref, sc_ref, q_ref, kn_ref, vn_ref, ck_ref, cv_ref, bias_ref, nbias_ref, nw_ref,
                      o_ref, kbuf, vbuf, sem, *, layer, n_pages, n_real, n_batch):
    b = pl.program_id(0)
    slot = b % 2
    pad = SAMPLE_PAD - n_real

    def page_copies(seq_idx, sl):
        cps = []
        for j in range(n_pages):
            page = pt_ref[seq_idx, j]
            rows = pl.ds(j * PAGE_SIZE * N_HEADS, PAGE_SIZE * N_HEADS)
            cps.append(pltpu.make_async_copy(ck_ref.at[layer, page], kbuf.at[sl, rows], sem.at[sl, 0]))
            cps.append(pltpu.make_async_copy(cv_ref.at[layer, page], vbuf.at[sl, rows], sem.at[sl, 1]))
        return cps

    @pl.when(b == 0)
    def _():
        for cp in page_copies(0, 0):
            cp.start()

    @pl.when(b + 1 < n_batch)
    def _():
        for cp in page_copies(b + 1, 1 - slot):
            cp.start()

    for cp in page_copies(b, slot):
        cp.wait()

    lam = sc_ref[0, 0]
    out_scale = sc_ref[0, 1]
    hs = range(N_HEADS)
    head_rows = [pl.ds(h, n_pages * PAGE_SIZE, stride=N_HEADS) for h in hs]
    qs = [_stack_maps(q_ref[pad:, h * HEAD_W:(h + 1) * HEAD_W] * DH_C ** -0.5) for h in hs]
    s = [_dot_nt(qs[h], kbuf[slot, head_rows[h], :]) + bias_ref[h] for h in hs]
    kn = [kn_ref[pad:, h * HEAD_W:(h + 1) * HEAD_W] for h in hs]
    vn = [vn_ref[pad:, h * HEAD_W:(h + 1) * HEAD_W] for h in hs]
    s_new = [[jnp.sum(qs[h] * kn[h][t:t + 1, :], axis=-1, keepdims=True) + nbias_ref[h, t] for t in range(n_real)]
             for h in hs]
    m = [functools.reduce(jnp.maximum, s_new[h], jnp.max(s[h], axis=-1, keepdims=True)) for h in hs]
    p = [jnp.exp(s[h] - m[h]) for h in hs]
    p_new = [[jnp.exp(sn - m[h]) for sn in s_new[h]] for h in hs]
    l = [sum(p_new[h], jnp.sum(p[h], axis=-1, keepdims=True)) for h in hs]
    acc = [_dot(p[h], vbuf[slot, head_rows[h], :]) for h in hs]
    outs = []
    for h in hs:
        a = acc[h]
        for t in range(n_real):
            a = a + p_new[h][t] * vn[h][t:t + 1, :]
        o = a / l[h]
        o = o[:n_real] - lam * o[n_real:]
        outs.append(_rms_norm(o, nw_ref[...]) * out_scale)
    o_ref[0:pad, :] = jnp.zeros((pad, BR_W), F32)
    o_ref[pad:, :] = jnp.concatenate(outs, axis=1)


def _diff_sample(page_table, scalars, proj, k_new, v_new, cache_k, cache_v, layer, bias, nbias, norm_w, n_real):
    n_batch, n_pages = page_table.shape
    n_past = n_pages * PAGE_SIZE
    new_tok = pl.BlockSpec((None, SAMPLE_PAD, BR_W), lambda b, pt: (layer, b, 0))
    grid_spec = pltpu.PrefetchScalarGridSpec(
        num_scalar_prefetch=1,
        grid=(n_batch,),
        in_specs=[pl.BlockSpec(memory_space=pltpu.SMEM),
                  pl.BlockSpec((SAMPLE_PAD, BR_W), lambda b, pt: (b, BLK_CQ)),
                  new_tok, new_tok,
                  pl.BlockSpec(memory_space=pl.ANY), pl.BlockSpec(memory_space=pl.ANY),
                  pl.BlockSpec(bias.shape, lambda b, pt: (0, 0, 0)),
                  pl.BlockSpec(nbias.shape, lambda b, pt: (0, 0, 0, 0)),
                  pl.BlockSpec((1, HEAD_W), lambda b, pt: (0, 0))],
        out_specs=pl.BlockSpec((SAMPLE_PAD, BR_W), lambda b, pt: (b, 0)),
        scratch_shapes=[pltpu.VMEM((2, n_past * N_HEADS, HEAD_W), F32), pltpu.VMEM((2, n_past * N_HEADS, HEAD_W), F32),
                        pltpu.SemaphoreType.DMA((2, 2))],
    )
    return pl.pallas_call(
        functools.partial(_diff_sample_body, layer=layer, n_pages=n_pages, n_real=n_real, n_batch=n_batch),
        grid_spec=grid_spec,
        out_shape=jax.ShapeDtypeStruct((n_batch * SAMPLE_PAD, BR_W), F32),
        compiler_params=_cparams(1),
        name="diff_sample",
    )(page_table, scalars, proj, k_new, v_new, cache_k, cache_v, bias, nbias, norm_w)


def _merge_body(oa_ref, ob_ref, oc_ref, g0_ref, g1_ref, g2_ref, x_ref, wb_ref, wo_ref, ln_ref, o_ref):
    m = (_sigmoid(g0_ref[...]) * _dot(oa_ref[...], wb_ref[0])
         + _sigmoid(g1_ref[...]) * _dot(ob_ref[...], wb_ref[1])
         + _sigmoid(g2_ref[...]) * _dot(oc_ref[...], wb_ref[2]))
    y = _dot(m, wo_ref[...])
    o_ref[...] = _layer_norm(ALPHA * x_ref[...] + y, ln_ref[0:1, :], ln_ref[1:2, :])


def _merge(oa, ob, oc, proj, x, w_branch, w_out, ln):
    t = x.shape[0]
    tm = min(ROW_TILE, t)
    br = pl.BlockSpec((tm, BR_W), lambda i: (i, 0))

    def gate(n):
        return pl.BlockSpec((tm, D_MODEL), lambda i: (i, n))

    return pl.pallas_call(
        _merge_body,
        grid=(t // tm,),
        in_specs=[br, br, br, gate(0), gate(1), gate(2),
                  pl.BlockSpec((tm, D_MODEL), lambda i: (i, 0)),
                  pl.BlockSpec(w_branch.shape, lambda i: (0, 0, 0)),
                  pl.BlockSpec(w_out.shape, lambda i: (0, 0)),
                  pl.BlockSpec((SUBLANES, D_MODEL), lambda i: (0, 0))],
        out_specs=pl.BlockSpec((tm, D_MODEL), lambda i: (i, 0)),
        out_shape=jax.ShapeDtypeStruct((t, D_MODEL), F32),
        compiler_params=_cparams(1),
        name="merge",
    )(oa, ob, oc, proj, proj, proj, x, w_branch, w_out, ln)


def _attend_memory(qs, kvs):
    n = range(len(qs))
    s = [_dot_nt(qs[i], kvs[i][0]) * HEAD_W ** -0.5 for i in n]
    p = [jnp.exp(s[i] - jnp.max(s[i], axis=-1, keepdims=True)) for i in n]
    p = [p[i] / jnp.sum(p[i], axis=-1, keepdims=True) for i in n]
    return [_dot(p[i], kvs[i][1]) for i in n]


def _merge_xattn_prompt_body(oa_ref, ob_ref, oc_ref, g0_ref, g1_ref, g2_ref, x_ref, wb_ref, wo_ref, ln0_ref,
                             mk_ref, mv_ref, wq_ref, wxo_ref, ln1_ref, o_ref):
    m = (_sigmoid(g0_ref[...]) * _dot(oa_ref[...], wb_ref[0])
         + _sigmoid(g1_ref[...]) * _dot(ob_ref[...], wb_ref[1])
         + _sigmoid(g2_ref[...]) * _dot(oc_ref[...], wb_ref[2]))
    x = _layer_norm(ALPHA * x_ref[...] + _dot(m, wo_ref[...]), ln0_ref[0:1, :], ln0_ref[1:2, :])
    mk, mv = mk_ref[0], mv_ref[0]
    q = _dot(x, wq_ref[...])
    o = jnp.concatenate(_attend_memory([_head(q, h) for h in range(N_HEADS)],
                                       [(_head(mk, h), _head(mv, h)) for h in range(N_HEADS)]), axis=1)
    o_ref[...] = _layer_norm(ALPHA * x + _dot(o, wxo_ref[...]), ln1_ref[0:1, :], ln1_ref[1:2, :])


def _merge_xattn_prompt(oa, ob, oc, proj, x, w_branch, w_out, ln0, mkv, w_xq, w_xo, ln1, batch, seq):
    tm = min(ROW_TILE, seq)
    per = seq // tm
    n_mem = mkv.shape[1]

    def rows(width, col):
        return pl.BlockSpec((tm, width), lambda b, i: (b * per + i, col))

    def whole(a):
        return pl.BlockSpec(a.shape, lambda b, i: (0,) * a.ndim)

    ln = pl.BlockSpec((SUBLANES, D_MODEL), lambda b, i: (0, 0))
    return pl.pallas_call(
        _merge_xattn_prompt_body,
        grid=(batch, per),
        in_specs=[rows(BR_W, 0), rows(BR_W, 0), rows(BR_W, 0),
                  rows(D_MODEL, 0), rows(D_MODEL, 1), rows(D_MODEL, 2), rows(D_MODEL, 0),
                  whole(w_branch), whole(w_out), ln,
                  pl.BlockSpec((1, n_mem, BR_W), lambda b, i: (b, 0, 0)),
                  pl.BlockSpec((1, n_mem, BR_W), lambda b, i: (b, 0, 1)),
                  whole(w_xq), whole(w_xo), ln],
        out_specs=rows(D_MODEL, 0),
        out_shape=jax.ShapeDtypeStruct(x.shape, F32),
        compiler_params=_cparams(2),
        name="merge_xattn_prompt",
    )(oa, ob, oc, proj, proj, proj, x, w_branch, w_out, ln0, mkv, mkv, w_xq, w_xo, ln1)


def _xattn_sample_body(x_ref, mk_ref, mv_ref, wq_ref, wo_ref, ln_ref, o_ref, att_ref, *, n_seq):
    x = x_ref[...]
    q = _dot(x, wq_ref[...])
    n_mem = mk_ref.shape[1] // N_HEADS
    tiles = [(i, h) for i in range(n_seq) for h in range(N_HEADS)]
    qs, kvs = [], []
    for i, h in tiles:
        rows = pl.ds(h, n_mem, stride=N_HEADS)
        qs.append(_head(q[i * SAMPLE_PAD:(i + 1) * SAMPLE_PAD], h))
        kvs.append((mk_ref[i, rows, :], mv_ref[i, rows, :]))
    for (i, h), o in zip(tiles, _attend_memory(qs, kvs)):
        att_ref[i * SAMPLE_PAD:(i + 1) * SAMPLE_PAD, h * HEAD_W:(h + 1) * HEAD_W] = o
    o_ref[...] = _layer_norm(ALPHA * x + _dot(att_ref[...], wo_ref[...]), ln_ref[0:1, :], ln_ref[1:2, :])


def _xattn_sample(x, mem_k, mem_v, layer, w_xq, w_xo, ln, seq_per_step=8):
    n_batch, mem_rows = mem_k.shape[1:3]
    rows = seq_per_step * SAMPLE_PAD
    mem = pl.BlockSpec((None, seq_per_step, mem_rows, HEAD_W), lambda g: (layer, g, 0, 0))
    return pl.pallas_call(
        functools.partial(_xattn_sample_body, n_seq=seq_per_step),
        grid=(n_batch // seq_per_step,),
        in_specs=[pl.BlockSpec((rows, D_MODEL), lambda g: (g, 0)), mem, mem,
                  pl.BlockSpec(w_xq.shape, lambda g: (0, 0)),
                  pl.BlockSpec(w_xo.shape, lambda g: (0, 0)),
                  pl.BlockSpec((SUBLANES, D_MODEL), lambda g: (0, 0))],
        out_specs=pl.BlockSpec((rows, D_MODEL), lambda g: (g, 0)),
        out_shape=jax.ShapeDtypeStruct(x.shape, F32),
        scratch_shapes=[pltpu.VMEM((rows, BR_W), F32)],
        compiler_params=_cparams(1),
        name="xattn_sample",
    )(x, mem_k, mem_v, w_xq, w_xo, ln)


def _ffn_body(x_ref, wg_ref, wu_ref, w2_ref, ln_ref, o_ref):
    x = x_ref[...]
    xb = x.astype(BF16)
    gate = jnp.dot(xb, wg_ref[...], preferred_element_type=F32)
    up = jnp.dot(xb, wu_ref[...], preferred_element_type=F32)
    y = _dot(_silu(gate) * up, w2_ref[...])
    o_ref[...] = _layer_norm(ALPHA * x + y, ln_ref[0:1, :], ln_ref[1:2, :])


def _ffn(x, w1, w2, ln):
    t = x.shape[0]
    tm = min(ROW_TILE, t)
    resident = pl.Buffered(1)
    return pl.pallas_call(
        _ffn_body,
        grid=(t // tm,),
        in_specs=[pl.BlockSpec((tm, D_MODEL), lambda i: (i, 0)),
                  pl.BlockSpec((D_MODEL, D_FF), lambda i: (0, 0), pipeline_mode=resident),
                  pl.BlockSpec((D_MODEL, D_FF), lambda i: (0, 1), pipeline_mode=resident),
                  pl.BlockSpec((D_FF, D_MODEL), lambda i: (0, 0), pipeline_mode=resident),
                  pl.BlockSpec((SUBLANES, D_MODEL), lambda i: (0, 0))],
        out_specs=pl.BlockSpec((tm, D_MODEL), lambda i: (i, 0)),
        out_shape=jax.ShapeDtypeStruct(x.shape, F32),
        compiler_params=_cparams(1),
        name="ffn",
    )(x, w1, w1, w2, ln)


def _split_w_in(w):
    ab0 = 8 * BR_W
    c0 = ab0 + 2 * N_HEADS
    g0 = c0 + 3 * BR_W
    pad = jnp.zeros((w.shape[0], LANES - 2 * N_HEADS), w.dtype)
    main = jnp.concatenate([w[:, g0:], w[:, :ab0], w[:, c0:c0 + BR_W]], axis=1)
    kv = jnp.concatenate([w[:, c0 + BR_W:g0], w[:, ab0:c0], pad], axis=1)
    return main.astype(BF16), kv.astype(BF16)


def _ln_rows(g, b):
    return jnp.concatenate([g[None], b[None], jnp.zeros((SUBLANES - 2, g.shape[0]), F32)], axis=0)


def kernel(x_prompt, x_sample, mem_prompt, cache_attn_k, cache_attn_v, cache_mem_k, cache_mem_v, state_hgrn,
           state_gdn, state_gdn_conv, page_table, w_in, w_branch, w_out, lower_bounds, hgrn_norm, gdn_a_log,
           gdn_dt_bias, gdn_conv_w, gdn_norm, diff_lambda, diff_norm, rel_bias, w_xq, w_mem_kv, w_xo, ln_g, ln_b,
           w_ffn_in, w_ffn_out):
    bp, seq, _ = x_prompt.shape
    bs, ls, _ = x_sample.shape
    n_mem = mem_prompt.shape[1]
    n_pages = page_table.shape[1]
    n_past = n_pages * PAGE_SIZE
    n_pool = cache_attn_k.shape[1]
    pad = SAMPLE_PAD - ls

    lb_cum = jnp.cumsum(jax.nn.softmax(lower_bounds.astype(F32), axis=0), axis=0)
    lb = (lb_cum - lb_cum[0]).reshape(DEPTH, N_HEADS, 1, HEAD_W)
    lbp = jnp.concatenate([jnp.log(lb), jnp.log1p(-lb), 1.0 - lb,
                           jnp.zeros((DEPTH, N_HEADS, SUBLANES - 3, HEAD_W), F32)], axis=2)

    n_max = max(2 * ATT_BLOCK + 2, n_past + ls)
    bucket = t5_bucket(-jnp.arange(n_max, dtype=jnp.int32))
    thr = jnp.searchsorted(bucket, jnp.arange(N_BUCKETS, dtype=jnp.int32), side="left").astype(jnp.int32)
    rel_bias_t = rel_bias.astype(F32).T
    tiles = _bias_tiles(thr, rel_bias_t)
    dist = rel_bias_t[:, bucket]
    far_bias = dist[:, ATT_BLOCK + 1]
    q_pos = n_past + jnp.arange(ls)
    past = dist[:, q_pos[:, None] - jnp.arange(n_past)[None, :]]
    bias_s = jnp.concatenate([past, past], axis=1)
    tt = jnp.arange(ls)[:, None] - jnp.arange(ls)[None, :]
    new = jnp.where(tt >= 0, dist[:, jnp.maximum(tt, 0)], NEG_INF)
    nbias = jnp.concatenate([new, new], axis=1).transpose(0, 2, 1)[..., None]

    xp = x_prompt.reshape(bp * seq, D_MODEL)
    xs = jnp.concatenate([jnp.zeros((bs, pad, D_MODEL), F32), x_sample], axis=1).reshape(bs * SAMPLE_PAD, D_MODEL)
    mem = mem_prompt.reshape(bp * n_mem, D_MODEL)
    cache_k = cache_attn_k.reshape(DEPTH, n_pool, PAGE_SIZE * N_HEADS, HEAD_W)
    cache_v = cache_attn_v.reshape(DEPTH, n_pool, PAGE_SIZE * N_HEADS, HEAD_W)
    mem_k = cache_mem_k.reshape(DEPTH, bs, n_mem * N_HEADS, HEAD_W)
    mem_v = cache_mem_v.reshape(DEPTH, bs, n_mem * N_HEADS, HEAD_W)

    pk = pv = jnp.zeros((DEPTH, bp * seq, BR_W), F32)
    sk = sv = jnp.zeros((DEPTH, bs * SAMPLE_PAD, BR_W), F32)
    sh = sg = jnp.zeros(state_hgrn.shape, F32)
    pmk, pmv, ph, pg, pc, sc = [], [], [], [], [], []
    b_col0 = BLK_B * BR_W

    for l in range(DEPTH):
        w_main, w_kv = _split_w_in(w_in[l])
        wb_l = w_branch[l].astype(BF16)
        wo_l = w_out[l].astype(BF16)
        wq_l = w_xq[l].astype(BF16)
        wxo_l = w_xo[l].astype(BF16)
        wkv_l = w_mem_kv[l].astype(BF16)
        w1_l = w_ffn_in[l].astype(BF16)
        w2_l = w_ffn_out[l].astype(BF16)
        lns = [_ln_rows(ln_g[l, i], ln_b[l, i]) for i in range(3)]
        hn = hgrn_norm[l][None]
        gn = gdn_norm[l][None]
        dn = diff_norm[l][None]
        head_params = jnp.stack([jnp.exp(gdn_a_log[l].astype(F32)), gdn_dt_bias[l].astype(F32)])
        lam_init = 0.8 - 0.6 * math.exp(-0.3 * l)
        lq1, lk1, lq2, lk2 = diff_lambda[l].astype(F32)
        lam = jnp.exp(jnp.sum(lq1 * lk1)) - jnp.exp(jnp.sum(lq2 * lk2)) + lam_init
        scalars = jnp.stack([jnp.stack([lam, jnp.float32(1.0 - lam_init), jnp.float32(0), jnp.float32(0)]),
                             far_bias])
        conv_w_l = gdn_conv_w[l]

        proj = _matmul(xp, w_main, 2 * ROW_TILE, MAIN_W // 3)
        pk, pv, ab = _kv_proj(xp, w_kv, l, pk, pv)
        oa, h_new = _hgrn_prompt(proj, lbp[l], hn, bp, seq)
        qkv = _gdn_conv(proj, BLK_B, conv_w_l, SEQ_TILE, seq)
        ob, g_new = _gdn_prompt(qkv, proj, ab, head_params, gn, bp, seq)
        oc = _diff_prompt(proj, pk, pv, l, scalars, tiles, dn, bp, seq)
        mkv = _matmul(mem, wkv_l, ROW_TILE, 2 * BR_W).reshape(bp, n_mem, 2 * BR_W)
        x2 = _merge_xattn_prompt(oa, ob, oc, proj, xp, wb_l, wo_l, lns[0], mkv, wq_l, wxo_l, lns[1], bp, seq)
        xp = _ffn(x2, w1_l, w2_l, lns[2])
        pmk.append(mkv[:, :, :BR_W].reshape(bp, n_mem, N_HEADS, HEAD_W))
        pmv.append(mkv[:, :, BR_W:].reshape(bp, n_mem, N_HEADS, HEAD_W))
        ph.append(h_new)
        pg.append(g_new)
        pc.append(proj.reshape(bp, seq, MAIN_W)[:, seq - (CONV_W - 1):, b_col0:b_col0 + CONV_CH])

        proj_s = _matmul(xs, w_main, ROW_TILE, MAIN_W // 3)
        sk, sv, ab_s = _kv_proj(xs, w_kv, l, sk, sv)
        oa, sh = _hgrn_sample(proj_s, lbp[l], hn, state_hgrn, l, sh, ls)
        conv_tok = proj_s[:, b_col0:b_col0 + CONV_CH].reshape(bs, SAMPLE_PAD, CONV_CH)
        conv_in = jnp.concatenate([conv_tok[:, :pad - (CONV_W - 1)], state_gdn_conv[l], conv_tok[:, pad:]], axis=1)
        qkv = _gdn_conv(conv_in.reshape(bs * SAMPLE_PAD, CONV_CH), 0, conv_w_l, bs * SAMPLE_PAD, SAMPLE_PAD)
        ob, sg = _gdn_sample(qkv, proj_s, ab_s, head_params, gn, state_gdn, l, sg, ls)
        oc = _diff_sample(page_table, scalars, proj_s, sk, sv, cache_k, cache_v, l, bias_s, nbias, dn, ls)
        x1 = _merge(oa, ob, oc, proj_s, xs, wb_l, wo_l, lns[0])
        x2 = _xattn_sample(x1, mem_k, mem_v, l, wq_l, wxo_l, lns[1])
        xs = _ffn(x2, w1_l, w2_l, lns[2])
        sc.append(conv_in[:, SAMPLE_PAD - (CONV_W - 1):])

    y_prompt = xp.reshape(bp, seq, D_MODEL)
    y_sample = xs.reshape(bs, SAMPLE_PAD, D_MODEL)[:, pad:]
    new_k_p = pk.reshape(DEPTH, bp, seq, N_HEADS, HEAD_W)
    new_v_p = pv.reshape(DEPTH, bp, seq, N_HEADS, HEAD_W)
    new_k_s = sk.reshape(DEPTH, bs, SAMPLE_PAD, N_HEADS, HEAD_W)[:, :, pad:]
    new_v_s = sv.reshape(DEPTH, bs, SAMPLE_PAD, N_HEADS, HEAD_W)[:, :, pad:]
    return (y_prompt, y_sample, new_k_p, new_v_p, jnp.stack(pmk), jnp.stack(pmv), jnp.stack(ph), jnp.stack(pg),
            jnp.stack(pc), new_k_s, new_v_s, sh, sg, jnp.stack(sc))
```

```python
import functools
import math

import jax
import jax.numpy as jnp
from jax import lax
from jax.experimental import pallas as pl
from jax.experimental.pallas import tpu as pltpu

F32 = jnp.float32
BF16 = jnp.bfloat16

D_MODEL = 1024
DEPTH = 4
PAGE_SIZE = 128
BR_W = D_MODEL // 2
N_HEADS = 4
HEAD_W = BR_W // N_HEADS
DH_C = HEAD_W // 2
CONV_W = 4
CONV_CH = 3 * BR_W
N_BRANCH = 3
N_BUCKETS = 32
MAX_DISTANCE = 128
D_FF = -(-8 * D_MODEL // (3 * 256)) * 256
LN_EPS = 1e-5
NORM_EPS = 1e-6
NEG_INF = -1e30
ALPHA = (2 * DEPTH) ** 0.25

LANES = 128
SUBLANES = 8
VMEM_LIMIT_BYTES = 56 * 1024 * 1024

BLK_A = N_BRANCH * D_MODEL // BR_W
BLK_B = BLK_A + 4
BLK_CQ = BLK_B + 4
MAIN_W = (BLK_CQ + 1) * BR_W
KV_W = 2 * BR_W + LANES

HGRN_SUB = 16
HGRN_ROWS = 64
GDN_CHUNK = 64
SOLVE_BLOCK = 16
SAMPLE_PAD = 8
SEQ_TILE = 512
GDN_TILE = 1024
ATT_BLOCK = 512
ATT_HEADS = 4
ROW_TILE = 512


def _cparams(n_axes):
    return pltpu.CompilerParams(dimension_semantics=("arbitrary",) * n_axes,
                                vmem_limit_bytes=VMEM_LIMIT_BYTES)


def _dot(a, b):
    return jnp.dot(a.astype(BF16), b.astype(BF16), preferred_element_type=F32)


def _dot_nt(a, b):
    return lax.dot_general(a.astype(BF16), b.astype(BF16), (((1,), (1,)), ((), ())),
                           preferred_element_type=F32)


def _dot_tn(a, b):
    return lax.dot_general(a.astype(BF16), b.astype(BF16), (((0,), (0,)), ((), ())),
                           preferred_element_type=F32)


def _dot_exact_lhs(m01, x):
    hi = x.astype(BF16)
    r1 = x - hi.astype(F32)
    mid = r1.astype(BF16)
    lo = (r1 - mid.astype(F32)).astype(BF16)
    dot = functools.partial(jnp.dot, preferred_element_type=F32)
    return dot(m01, hi) + dot(m01, mid) + dot(m01, lo)


def _sigmoid(x):
    return 1.0 / (1.0 + jnp.exp(-x))


def _silu(x):
    return x * _sigmoid(x)


def _softplus(x):
    return jnp.maximum(x, 0.0) + jnp.log1p(jnp.exp(-jnp.abs(x)))


def _layer_norm(h, g, b):
    mu = jnp.mean(h, axis=-1, keepdims=True)
    c = h - mu
    var = jnp.mean(c * c, axis=-1, keepdims=True)
    return c * lax.rsqrt(var + LN_EPS) * g + b


def _rms_norm(x, g):
    return x * lax.rsqrt(jnp.mean(x * x, axis=-1, keepdims=True) + NORM_EPS) * g


def _bcast_rows(x, rows, reps):
    return jnp.concatenate([jnp.broadcast_to(x[r:r + 1, :], (reps, x.shape[1])) for r in rows], axis=0)


def _head(x, h):
    return x[:, h * HEAD_W:(h + 1) * HEAD_W]


ALIASED = pl.BlockSpec(memory_space=pl.ANY)


def _mm_body(x_ref, w_ref, o_ref):
    o_ref[...] = _dot(x_ref[...], w_ref[...])


def _matmul(x, w, tm, tn):
    t, k = x.shape
    n = w.shape[1]
    tm = min(tm, t)
    return pl.pallas_call(
        _mm_body,
        grid=(n // tn, t // tm),
        in_specs=[pl.BlockSpec((tm, k), lambda j, i: (i, 0)),
                  pl.BlockSpec((k, tn), lambda j, i: (0, j))],
        out_specs=pl.BlockSpec((tm, tn), lambda j, i: (i, j)),
        out_shape=jax.ShapeDtypeStruct((t, n), F32),
        compiler_params=_cparams(2),
        name="proj_matmul",
    )(x, w)


def _kv_body(x_ref, w_ref, kp_ref, vp_ref, k_ref, v_ref, ab_ref, k16_ref, v16_ref):
    y = _dot(x_ref[...], w_ref[...])
    k = y[:, :BR_W]
    v = y[:, BR_W:2 * BR_W]
    k_ref[...] = k
    v_ref[...] = v
    ab_ref[...] = y[:, 2 * BR_W:]
    k16_ref[...] = k.astype(BF16)
    ones = jnp.ones((v.shape[0], HEAD_W), BF16)
    v16_ref[...] = jnp.concatenate([piece for h in range(N_HEADS) for piece in (_head(v, h).astype(BF16), ones)],
                                   axis=1)


def _kv_proj(x, w, layer, k_prev, v_prev):
    t = x.shape[0]
    tm = min(ROW_TILE, t)
    slab = pl.BlockSpec((None, tm, BR_W), lambda i: (layer, i, 0))
    stacked = jax.ShapeDtypeStruct((DEPTH, t, BR_W), F32)
    return pl.pallas_call(
        _kv_body,
        grid=(t // tm,),
        in_specs=[pl.BlockSpec((tm, D_MODEL), lambda i: (i, 0)),
                  pl.BlockSpec(w.shape, lambda i: (0, 0)), ALIASED, ALIASED],
        out_specs=[slab, slab, pl.BlockSpec((tm, LANES), lambda i: (i, 0)),
                   pl.BlockSpec((tm, BR_W), lambda i: (i, 0)), pl.BlockSpec((tm, 2 * BR_W), lambda i: (i, 0))],
        out_shape=[stacked, stacked, jax.ShapeDtypeStruct((t, LANES), F32),
                   jax.ShapeDtypeStruct((t, BR_W), BF16), jax.ShapeDtypeStruct((t, 2 * BR_W), BF16)],
        input_output_aliases={2: 0, 3: 1},
        compiler_params=_cparams(1),
        name="kv_proj",
    )(x, w, k_prev, v_prev)


def _hgrn_rows(q, z, v, lbp, sub, masks, valid):
    r = q.shape[0]
    n_sub = r // sub
    tri, pos = masks
    log_lb, log1m_lb, om_lb = lbp[0:1, :], lbp[1:2, :], lbp[2:3, :]
    log_sig = jnp.minimum(z, 0.0) - jnp.log(1.0 + jnp.exp(-jnp.abs(z)))
    t2 = log1m_lb + log_sig
    logf = jnp.maximum(log_lb, t2) + jnp.log(1.0 + jnp.exp(-jnp.abs(log_lb - t2)))
    k = om_lb * (1.0 / (1.0 + jnp.exp(z)))
    if valid is not None:
        logf = jnp.where(valid, logf, 0.0)
        k = jnp.where(valid, k, 0.0)
    bc = _dot_exact_lhs(tri, logf)
    last_rows = [i * sub + sub - 1 for i in range(n_sub)]
    b_last = _bcast_rows(bc, last_rows, sub)
    q_hat = q * jnp.exp(bc)
    k_hat = k * jnp.exp(b_last - bc)
    o = jnp.zeros_like(q)
    for s in range(sub):
        rows = [i * sub + s for i in range(n_sub)]
        ks = _bcast_rows(k, rows, sub)
        bs = _bcast_rows(bc, rows, sub)
        vs = _bcast_rows(v, rows, sub)
        a = jnp.where(pos >= s, q * ks * jnp.exp(bc - bs), 0.0)
        o = o + jnp.sum(a, axis=-1, keepdims=True) * vs
    return q_hat, k_hat, o, bc


def _sub_chunk_masks(rows, sub):
    ri = lax.broadcasted_iota(jnp.int32, (rows, rows), 0)
    ci = lax.broadcasted_iota(jnp.int32, (rows, rows), 1)
    tri = ((ri // sub == ci // sub) & (ci <= ri)).astype(BF16)
    pos = lax.broadcasted_iota(jnp.int32, (rows, 1), 0) % sub
    return tri, pos


def _hgrn_finish(o, g, norm_w):
    return _rms_norm(o, norm_w) * _silu(g)


def _hgrn_prompt_body(q_ref, f_ref, i_ref, g_ref, lbp_ref, nw_ref, o_ref, s_out_ref, st_ref, *, n_tiles):
    t = pl.program_id(1)

    @pl.when(t == 0)
    def _():
        st_ref[...] = jnp.zeros_like(st_ref)

    rows = HGRN_ROWS
    nw = nw_ref[...]
    masks = _sub_chunk_masks(rows, HGRN_SUB)

    def chunk(c, carry):
        sl = pl.ds(pl.multiple_of(c * rows, rows), rows)
        qa, za, va, ga = q_ref[sl, :], f_ref[sl, :], i_ref[sl, :], g_ref[sl, :]
        hs = range(N_HEADS)
        v = [_head(va, h) for h in hs]
        parts = [_hgrn_rows(_head(qa, h), _head(za, h), v[h], lbp_ref[h], HGRN_SUB, masks, None) for h in hs]
        st = [st_ref[h] for h in hs]
        outs = [[] for _ in hs]
        for i in range(rows // HGRN_SUB):
            lo, hi = i * HGRN_SUB, (i + 1) * HGRN_SUB
            for h in hs:
                q_hat, k_hat, o, bc = parts[h]
                outs[h].append(o[lo:hi] + _dot_nt(q_hat[lo:hi], st[h]))
                st[h] = st[h] * jnp.exp(bc[hi - 1:hi, :]) + _dot_tn(v[h][lo:hi], k_hat[lo:hi])
        for h in hs:
            st_ref[h] = st[h]
        o_ref[sl, :] = jnp.concatenate(
            [_hgrn_finish(jnp.concatenate(outs[h], axis=0), _head(ga, h), nw) for h in hs], axis=1)
        return carry

    lax.fori_loop(0, SEQ_TILE // rows, chunk, 0)

    @pl.when(t == n_tiles - 1)
    def _():
        for h in range(N_HEADS):
            s_out_ref[0, h] = st_ref[h].T


def _hgrn_prompt(proj, lbp, norm_w, batch, seq):
    n_tiles = seq // SEQ_TILE

    def col(blk):
        return pl.BlockSpec((SEQ_TILE, BR_W), lambda b, t: (b * n_tiles + t, blk))

    return pl.pallas_call(
        functools.partial(_hgrn_prompt_body, n_tiles=n_tiles),
        grid=(batch, n_tiles),
        in_specs=[col(BLK_A), col(BLK_A + 1), col(BLK_A + 2), col(BLK_A + 3),
                  pl.BlockSpec((N_HEADS, SUBLANES, HEAD_W), lambda b, t: (0, 0, 0)),
                  pl.BlockSpec((1, HEAD_W), lambda b, t: (0, 0))],
        out_specs=[pl.BlockSpec((SEQ_TILE, BR_W), lambda b, t: (b * n_tiles + t, 0)),
                   pl.BlockSpec((1, N_HEADS, HEAD_W, HEAD_W), lambda b, t: (b, 0, 0, 0))],
        out_shape=[jax.ShapeDtypeStruct((batch * seq, BR_W), F32),
                   jax.ShapeDtypeStruct((batch, N_HEADS, HEAD_W, HEAD_W), F32)],
        scratch_shapes=[pltpu.VMEM((N_HEADS, HEAD_W, HEAD_W), F32)],
        compiler_params=_cparams(2),
        name="hgrn_prompt",
    )(proj, proj, proj, proj, lbp, norm_w)


def _hgrn_sample_body(q_ref, f_ref, i_ref, g_ref, lbp_ref, nw_ref, s0_ref, *refs, n_seq, n_real):
    o_ref, s_out_ref = refs[-2:]
    nw = nw_ref[...]
    rows = n_seq * SAMPLE_PAD
    masks = _sub_chunk_masks(rows, SAMPLE_PAD)
    valid = masks[1] >= SAMPLE_PAD - n_real
    qa, za, va, ga = q_ref[...], f_ref[...], i_ref[...], g_ref[...]
    hs = range(N_HEADS)
    v = [_head(va, h) for h in hs]
    parts = [_hgrn_rows(_head(qa, h), _head(za, h), v[h], lbp_ref[h], SAMPLE_PAD, masks, valid) for h in hs]
    eye = (lax.broadcasted_iota(jnp.int32, (HEAD_W, HEAD_W), 0)
           == lax.broadcasted_iota(jnp.int32, (HEAD_W, HEAD_W), 1))
    outs = [[] for _ in hs]
    for i in range(n_seq):
        lo, hi = i * SAMPLE_PAD, (i + 1) * SAMPLE_PAD
        for h in hs:
            q_hat, k_hat, o, bc = parts[h]
            s0 = s0_ref[i, h]
            outs[h].append(o[lo:hi] + _dot(q_hat[lo:hi], s0))
            decay = jnp.exp(bc[hi - 1:hi, :])
            decay_col = jnp.sum(jnp.where(eye, decay, 0.0), axis=-1, keepdims=True)
            s_out_ref[i, h] = s0 * decay_col + _dot_tn(k_hat[lo:hi], v[h][lo:hi])
    o_ref[...] = jnp.concatenate(
        [_hgrn_finish(jnp.concatenate(outs[h], axis=0), _head(ga, h), nw) for h in hs], axis=1)


def _state_specs(layer, seq_per_step):
    return pl.BlockSpec((None, seq_per_step, N_HEADS, HEAD_W, HEAD_W), lambda g: (layer, g, 0, 0, 0))


def _hgrn_sample(proj, lbp, norm_w, states, layer, prev, n_real, seq_per_step=8):
    n_batch = states.shape[1]
    rows = seq_per_step * SAMPLE_PAD

    def col(blk):
        return pl.BlockSpec((rows, BR_W), lambda g: (g, blk))

    return pl.pallas_call(
        functools.partial(_hgrn_sample_body, n_seq=seq_per_step, n_real=n_real),
        grid=(n_batch // seq_per_step,),
        in_specs=[col(BLK_A), col(BLK_A + 1), col(BLK_A + 2), col(BLK_A + 3),
                  pl.BlockSpec((N_HEADS, SUBLANES, HEAD_W), lambda g: (0, 0, 0)),
                  pl.BlockSpec((1, HEAD_W), lambda g: (0, 0)),
                  _state_specs(layer, seq_per_step), ALIASED],
        out_specs=[pl.BlockSpec((rows, BR_W), lambda g: (g, 0)), _state_specs(layer, seq_per_step)],
        out_shape=[jax.ShapeDtypeStruct((n_batch * SAMPLE_PAD, BR_W), F32),
                   jax.ShapeDtypeStruct(states.shape, F32)],
        input_output_aliases={7: 1},
        compiler_params=_cparams(1),
        name="hgrn_sample",
    )(proj, proj, proj, proj, lbp, norm_w, states, prev)


def _conv_body(prev_ref, cur_ref, w_ref, o_ref, buf_ref, *, tile, seq_len):
    i = pl.program_id(0)
    c = pl.program_id(1)
    buf_ref[0:SUBLANES, :] = prev_ref[...]
    buf_ref[SUBLANES:, :] = cur_ref[...]
    pos = (i * tile + lax.broadcasted_iota(jnp.int32, (tile, 1), 0)) % seq_len
    w = w_ref[...]
    acc = cur_ref[...] * w[CONV_W - 1:CONV_W, :]
    for back in range(1, CONV_W):
        shifted = buf_ref[pl.ds(SUBLANES - back, tile), :]
        acc = acc + jnp.where(pos >= back, shifted, 0.0) * w[CONV_W - 1 - back:CONV_W - back, :]
    act = _silu(acc)
    q_scale = jnp.where(c == 0, HEAD_W ** -0.5, 1.0)
    heads = []
    for h in range(N_HEADS):
        a = _head(act, h)
        inv = lax.rsqrt(jnp.sum(a * a, axis=-1, keepdims=True) + NORM_EPS)
        heads.append(a * jnp.where(c < 2, inv * q_scale, 1.0))
    o_ref[...] = jnp.concatenate(heads, axis=1)


def _gdn_conv(conv_in, col0, conv_w, tile, seq_len):
    t = conv_in.shape[0]
    per = tile // SUBLANES
    return pl.pallas_call(
        functools.partial(_conv_body, tile=tile, seq_len=seq_len),
        grid=(t // tile, CONV_CH // BR_W),
        in_specs=[pl.BlockSpec((SUBLANES, BR_W), lambda i, c: (jnp.maximum(i * per - 1, 0), col0 + c)),
                  pl.BlockSpec((tile, BR_W), lambda i, c: (i, col0 + c)),
                  pl.BlockSpec((CONV_W, BR_W), lambda i, c: (0, c))],
        out_specs=pl.BlockSpec((tile, BR_W), lambda i, c: (i, c)),
        out_shape=jax.ShapeDtypeStruct((t, CONV_CH), F32),
        scratch_shapes=[pltpu.VMEM((tile + SUBLANES, BR_W), F32)],
        compiler_params=_cparams(2),
        name="gdn_conv",
    )(conv_in, conv_in, conv_w)


def _dot_f32(a, b):
    ah = a.astype(BF16)
    al = (a - ah.astype(F32)).astype(BF16)
    bh = b.astype(BF16)
    bl = (b - bh.astype(F32)).astype(BF16)
    dot = functools.partial(jnp.dot, preferred_element_type=F32)
    return dot(ah, bh) + dot(ah, bl) + dot(al, bh)


def _forward_substitute(lmats, rhss):
    n_sys = len(lmats)
    c = lmats[0].shape[0]
    nb = min(c, SOLVE_BLOCK)
    done = [[] for _ in range(n_sys)]
    for i0 in range(0, c, nb):
        xbs, lbs = [], []
        for i in range(n_sys):
            r = rhss[i][i0:i0 + nb, :]
            if i0:
                r = r - _dot_f32(lmats[i][i0:i0 + nb, :i0], jnp.concatenate(done[i], axis=0))
            xbs.append([r[s:s + SUBLANES, :] for s in range(0, nb, SUBLANES)])
            lbs.append([lmats[i][i0 + s:i0 + s + SUBLANES, :] for s in range(0, nb, SUBLANES)])
        for j in range(nb - 1):
            for i in range(n_sys):
                xb = xbs[i]
                xj = xb[j // SUBLANES][j % SUBLANES:j % SUBLANES + 1, :]
                for b in range((j + 1) // SUBLANES, len(xb)):
                    xb[b] = xb[b] - lbs[i][b][:, i0 + j:i0 + j + 1] * xj
        for i in range(n_sys):
            done[i].extend(xbs[i])
    return [jnp.concatenate(d, axis=0) if len(d) > 1 else d[0] for d in done]


def _gdn_masks(c):
    ri = lax.broadcasted_iota(jnp.int32, (c, c), 0)
    ci = lax.broadcasted_iota(jnp.int32, (c, c), 1)
    mr = lax.broadcasted_iota(jnp.int32, (c, LANES), 0)
    mc = lax.broadcasted_iota(jnp.int32, (c, LANES), 1)
    return ci <= ri, ci < ri, ((mc < mr) & (mc < c)) | (mc == c)


def _gdn_prepare(hp_ref, qa, ka, va, ab, masks, valid):
    c = qa.shape[0]
    incl, strict, sel = masks
    hs = range(N_HEADS)
    lane = lax.broadcasted_iota(jnp.int32, ab.shape, 1)
    q = [_head(qa, h) for h in hs]
    k = [_head(ka, h) for h in hs]
    g, beta = [], []
    for h in hs:
        a_col = jnp.sum(jnp.where(lane == h, ab, 0.0), axis=-1, keepdims=True)
        b_col = jnp.sum(jnp.where(lane == h + N_HEADS, ab, 0.0), axis=-1, keepdims=True)
        g_h = -hp_ref[0, h] * _softplus(a_col + hp_ref[1, h])
        beta_h = _sigmoid(b_col)
        if valid is not None:
            g_h = jnp.where(valid, g_h, 0.0)
            beta_h = jnp.where(valid, beta_h, 0.0)
        g.append(g_h)
        beta.append(beta_h)
    incl_b = incl.astype(BF16)
    m = [_dot_exact_lhs(incl_b, jnp.where(sel, g[h], 0.0)) for h in hs]
    gcum = [m[h][:, c:c + 1] for h in hs]
    decay = [jnp.where(incl, jnp.exp(jnp.where(incl, m[h][:, :c], 0.0)), 0.0) for h in hs]
    kb = [k[h] * beta[h] for h in hs]
    lmat = [jnp.where(strict, _dot_nt(kb[h], k[h]) * decay[h], 0.0) for h in hs]
    e_g = [jnp.exp(gcum[h]) for h in hs]
    sol = _forward_substitute(lmat, [jnp.concatenate([_head(va, h) * beta[h], kb[h] * e_g[h]], axis=1) for h in hs])
    attn = [jnp.where(incl, _dot_nt(q[h], k[h]) * decay[h], 0.0) for h in hs]
    out = []
    for h in hs:
        g_last = gcum[h][c - 1:c, :]
        out.append((sol[h][:, :HEAD_W], sol[h][:, HEAD_W:], q[h] * e_g[h], k[h] * jnp.exp(g_last - gcum[h]),
                    attn[h], jnp.exp(g_last)))
    return out


def _gdn_apply(prep, ga, nw, get_state, put_state):
    hs = range(N_HEADS)
    s = [get_state(h) for h in hs]
    v_new = [prep[h][0] - _dot(prep[h][1], s[h]) for h in hs]
    o = [_dot(prep[h][2], s[h]) + _dot(prep[h][4], v_new[h]) for h in hs]
    for h in hs:
        put_state(h, s[h] * prep[h][5] + _dot_tn(prep[h][3], v_new[h]))
    return jnp.concatenate([_rms_norm(o[h], nw) * _silu(_head(ga, h)) for h in hs], axis=1)


def _gdn_prompt_body(hp_ref, q_ref, k_ref, v_ref, g_ref, ab_ref, nw_ref, o_ref, s_out_ref, s_ref, pre_ref, last_ref,
                     *, n_tiles):
    t = pl.program_id(1)

    @pl.when(t == 0)
    def _():
        s_ref[...] = jnp.zeros_like(s_ref)

    nw = nw_ref[...]
    masks = _gdn_masks(GDN_CHUNK)
    n_chunks = q_ref.shape[0] // GDN_CHUNK

    def put(h, s_new):
        s_ref[h] = s_new

    def rows_of(c):
        return pl.ds(pl.multiple_of(c * GDN_CHUNK, GDN_CHUNK), GDN_CHUNK)

    def prepare_into(c, slot):
        sl = rows_of(c)
        prep = _gdn_prepare(hp_ref, q_ref[sl, :], k_ref[sl, :], v_ref[sl, :], ab_ref[sl, :], masks, None)
        for h, (u, w, qe, ke, attn, e_last) in enumerate(prep):
            for i, x in enumerate((u, w, qe, ke)):
                pre_ref[slot, h, i] = x
            pre_ref[slot, h, 4, :, :GDN_CHUNK] = attn
            last_ref[slot, h] = jnp.broadcast_to(e_last, (SUBLANES, LANES))

    prepare_into(0, 0)

    def chunk(c, carry):
        slot = c % 2
        prep = [(pre_ref[slot, h, 0], pre_ref[slot, h, 1], pre_ref[slot, h, 2], pre_ref[slot, h, 3],
                 pre_ref[slot, h, 4, :, :GDN_CHUNK], last_ref[slot, h, 0:1, 0:1]) for h in range(N_HEADS)]
        sl = rows_of(c)
        o_ref[sl, :] = _gdn_apply(prep, g_ref[sl, :], nw, lambda h: s_ref[h], put)
        prepare_into(jnp.minimum(c + 1, n_chunks - 1), 1 - slot)
        return carry

    lax.fori_loop(0, n_chunks, chunk, 0)

    @pl.when(t == n_tiles - 1)
    def _():
        s_out_ref[0] = s_ref[...]


def _gdn_prompt(qkv, proj, ab, head_params, norm_w, batch, seq):
    tile = min(GDN_TILE, seq)
    n_tiles = seq // tile

    def col(blk):
        return pl.BlockSpec((tile, BR_W), lambda b, t: (b * n_tiles + t, blk))

    return pl.pallas_call(
        functools.partial(_gdn_prompt_body, n_tiles=n_tiles),
        grid=(batch, n_tiles),
        in_specs=[pl.BlockSpec(memory_space=pltpu.SMEM),
                  col(0), col(1), col(2), col(BLK_B + 3),
                  pl.BlockSpec((tile, LANES), lambda b, t: (b * n_tiles + t, 0)),
                  pl.BlockSpec((1, HEAD_W), lambda b, t: (0, 0))],
        out_specs=[pl.BlockSpec((tile, BR_W), lambda b, t: (b * n_tiles + t, 0)),
                   pl.BlockSpec((1, N_HEADS, HEAD_W, HEAD_W), lambda b, t: (b, 0, 0, 0))],
        out_shape=[jax.ShapeDtypeStruct((batch * seq, BR_W), F32),
                   jax.ShapeDtypeStruct((batch, N_HEADS, HEAD_W, HEAD_W), F32)],
        scratch_shapes=[pltpu.VMEM((N_HEADS, HEAD_W, HEAD_W), F32),
                        pltpu.VMEM((2, N_HEADS, 5, GDN_CHUNK, HEAD_W), F32),
                        pltpu.VMEM((2, N_HEADS, SUBLANES, LANES), F32)],
        compiler_params=_cparams(2),
        name="gdn_prompt",
    )(head_params, qkv, qkv, qkv, proj, ab, norm_w)


def _gdn_sample_body(hp_ref, q_ref, k_ref, v_ref, g_ref, ab_ref, nw_ref, s0_ref, *refs, n_seq, n_real):
    o_ref, s_out_ref = refs[-2:]
    nw = nw_ref[...]
    masks = _gdn_masks(SAMPLE_PAD)
    valid = lax.broadcasted_iota(jnp.int32, (SAMPLE_PAD, 1), 0) >= SAMPLE_PAD - n_real

    def one(i, carry):
        sl = pl.ds(pl.multiple_of(i * SAMPLE_PAD, SAMPLE_PAD), SAMPLE_PAD)

        def put(h, s_new):
            s_out_ref[i, h] = s_new

        prep = _gdn_prepare(hp_ref, q_ref[sl, :], k_ref[sl, :], v_ref[sl, :], ab_ref[sl, :], masks, valid)
        o_ref[sl, :] = _gdn_apply(prep, g_ref[sl, :], nw, lambda h: s0_ref[i, h], put)
        return carry

    lax.fori_loop(0, n_seq, one, 0)


def _gdn_sample(qkv, proj, ab, head_params, norm_w, states, layer, prev, n_real, seq_per_step=8):
    n_batch = states.shape[1]
    rows = seq_per_step * SAMPLE_PAD

    def col(blk):
        return pl.BlockSpec((rows, BR_W), lambda g: (g, blk))

    return pl.pallas_call(
        functools.partial(_gdn_sample_body, n_seq=seq_per_step, n_real=n_real),
        grid=(n_batch // seq_per_step,),
        in_specs=[pl.BlockSpec(memory_space=pltpu.SMEM),
                  col(0), col(1), col(2), col(BLK_B + 3),
                  pl.BlockSpec((rows, LANES), lambda g: (g, 0)),
                  pl.BlockSpec((1, HEAD_W), lambda g: (0, 0)),
                  _state_specs(layer, seq_per_step), ALIASED],
        out_specs=[pl.BlockSpec((rows, BR_W), lambda g: (g, 0)), _state_specs(layer, seq_per_step)],
        out_shape=[jax.ShapeDtypeStruct((n_batch * SAMPLE_PAD, BR_W), F32),
                   jax.ShapeDtypeStruct(states.shape, F32)],
        input_output_aliases={8: 1},
        compiler_params=_cparams(1),
        name="gdn_sample",
    )(head_params, qkv, qkv, qkv, proj, ab, norm_w, states, prev)


def t5_bucket(rel):
    n = jnp.maximum(-rel, 0)
    exact = N_BUCKETS // 2
    log_part = jnp.log(jnp.maximum(n, 1).astype(F32) / exact) / math.log(MAX_DISTANCE / exact)
    large = jnp.minimum(exact + (log_part * (N_BUCKETS - exact)).astype(jnp.int32), N_BUCKETS - 1)
    return jnp.where(n < exact, n, large)


def _bias_tile_body(thr_ref, rb_ref, o_ref):
    h = pl.program_id(0)
    blk = ATT_BLOCK
    row = lax.broadcasted_iota(jnp.int32, (blk, blk), 0)
    col = lax.broadcasted_iota(jnp.int32, (blk, blk), 1)
    for which in range(2):
        n = row - col + which * blk
        val = jnp.full((blk, blk), rb_ref[h, 0], F32)
        for b in range(1, N_BUCKETS):
            val = jnp.where(n >= thr_ref[b], rb_ref[h, b], val)
        if which == 0:
            val = jnp.where(n >= 0, val, NEG_INF)
        o_ref[which] = val


def _bias_tiles(thr, rel_bias_t):
    return pl.pallas_call(
        _bias_tile_body,
        grid=(N_HEADS,),
        in_specs=[pl.BlockSpec(memory_space=pltpu.SMEM), pl.BlockSpec(memory_space=pltpu.SMEM)],
        out_specs=pl.BlockSpec((None, 2, ATT_BLOCK, ATT_BLOCK), lambda h: (h, 0, 0, 0)),
        out_shape=jax.ShapeDtypeStruct((N_HEADS, 2, ATT_BLOCK, ATT_BLOCK), F32),
        compiler_params=_cparams(1),
        name="bias_tiles",
    )(thr, rel_bias_t)


def _stack_maps(q):
    lane = lax.broadcasted_iota(jnp.int32, q.shape, 1)
    return jnp.concatenate([jnp.where(lane < DH_C, q, 0.0), jnp.where(lane >= DH_C, q, 0.0)], axis=0)


def _diff_prompt_body(sc_ref, q_ref, k_ref, v_ref, d_ref, nw_ref, o_ref, m_ref, acc_ref):
    hp = pl.program_id(1)
    qi = pl.program_id(2)
    blk = ATT_BLOCK
    lam = sc_ref[0, 0]
    out_scale = sc_ref[0, 1]
    qs = [_stack_maps(_head(q_ref[...], hh) * DH_C ** -0.5).astype(BF16) for hh in range(ATT_HEADS)]
    m_ref[...] = jnp.full_like(m_ref, -jnp.inf)
    acc_ref[...] = jnp.zeros_like(acc_ref)

    def block(kj, bias_of):
        sl = pl.ds(pl.multiple_of(kj * blk, blk), blk)
        for hh in range(ATT_HEADS):
            s = lax.dot_general(qs[hh], k_ref[sl, hh * HEAD_W:(hh + 1) * HEAD_W], (((1,), (1,)), ((), ())),
                                preferred_element_type=F32)
            bias = bias_of(hh)
            if bias.ndim == 2:
                s = (s.reshape(2, blk, blk) + bias[None]).reshape(2 * blk, blk)
            else:
                s = s + bias
            m_old = m_ref[hh]
            m_new = jnp.maximum(m_old, jnp.max(s, axis=-1, keepdims=True))
            alpha = jnp.exp(m_old - m_new)
            p = jnp.exp(s - jnp.concatenate([m_new] * (blk // LANES), axis=1))
            pv = jnp.dot(p.astype(BF16), v_ref[sl, hh * 2 * HEAD_W:(hh + 1) * 2 * HEAD_W],
                         preferred_element_type=F32)
            acc_ref[hh] = jnp.concatenate([alpha, alpha], axis=1) * acc_ref[hh] + pv
            m_ref[hh] = m_new

    def far(kj, carry):
        block(kj, lambda hh: sc_ref[1, hp * ATT_HEADS + hh])
        return carry

    lax.fori_loop(0, jnp.maximum(qi - 1, 0), far, 0)

    @pl.when(qi >= 1)
    def _():
        block(qi - 1, lambda hh: d_ref[hh, 1])

    block(qi, lambda hh: d_ref[hh, 0])
    outs = []
    for hh in range(ATT_HEADS):
        acc = acc_ref[hh]
        o = acc[:, :HEAD_W] / acc[:, HEAD_W:]
        o = o[:blk] - lam * o[blk:]
        outs.append(_rms_norm(o, nw_ref[...]) * out_scale)
    o_ref[...] = jnp.concatenate(outs, axis=1)


def _diff_prompt(proj, k16, v16, scalars, tiles, norm_w, batch, seq):
    nq = seq // ATT_BLOCK
    width = ATT_HEADS * HEAD_W
    per_row = BR_W // width
    return pl.pallas_call(
        _diff_prompt_body,
        grid=(batch, N_HEADS // ATT_HEADS, nq),
        in_specs=[pl.BlockSpec(memory_space=pltpu.SMEM),
                  pl.BlockSpec((ATT_BLOCK, width), lambda b, h, q: (b * nq + q, BLK_CQ * per_row + h)),
                  pl.BlockSpec((seq, width), lambda b, h, q: (b, h)),
                  pl.BlockSpec((seq, 2 * width), lambda b, h, q: (b, h)),
                  pl.BlockSpec((ATT_HEADS, 2, ATT_BLOCK, ATT_BLOCK), lambda b, h, q: (h, 0, 0, 0),
                               pipeline_mode=pl.Buffered(1) if ATT_HEADS == N_HEADS else None),
                  pl.BlockSpec((1, HEAD_W), lambda b, h, q: (0, 0))],
        out_specs=pl.BlockSpec((ATT_BLOCK, width), lambda b, h, q: (b * nq + q, h)),
        out_shape=jax.ShapeDtypeStruct((batch * seq, BR_W), F32),
        scratch_shapes=[pltpu.VMEM((ATT_HEADS, 2 * ATT_BLOCK, LANES), F32),
                        pltpu.VMEM((ATT_HEADS, 2 * ATT_BLOCK, 2 * HEAD_W), F32)],
        compiler_params=_cparams(3),
        name="diff_prompt",
    )(scalars, proj, k16, v16, tiles, norm_w)


def _diff_sample_body(pt_ref, sc_ref, q_ref, kn_ref, vn_ref, ck_ref, cv_ref, bias_ref, nbias_ref, nw_ref,
                      o_ref, kbuf, vbuf, sem, *, layer, n_pages, n_real, n_batch):
    b = pl.program_id(0)
    slot = b % 2
    pad = SAMPLE_PAD - n_real

    def page_copies(seq_idx, sl):
        cps = []
        for j in range(n_pages):
            page = pt_ref[seq_idx, j]
            rows = pl.ds(j * PAGE_SIZE * N_HEADS, PAGE_SIZE * N_HEADS)
            cps.append(pltpu.make_async_copy(ck_ref.at[layer, page], kbuf.at[sl, rows], sem.at[sl, 0]))
            cps.append(pltpu.make_async_copy(cv_ref.at[layer, page], vbuf.at[sl, rows], sem.at[sl, 1]))
        return cps

    @pl.when(b == 0)
    def _():
        for cp in page_copies(0, 0):
            cp.start()

    @pl.when(b + 1 < n_batch)
    def _():
        for cp in page_copies(b + 1, 1 - slot):
            cp.start()

    for cp in page_copies(b, slot):
        cp.wait()

    lam = sc_ref[0, 0]
    out_scale = sc_ref[0, 1]
    hs = range(N_HEADS)
    head_rows = [pl.ds(h, n_pages * PAGE_SIZE, stride=N_HEADS) for h in hs]
    qs = [_stack_maps(q_ref[pad:, h * HEAD_W:(h + 1) * HEAD_W] * DH_C ** -0.5) for h in hs]
    s = [_dot_nt(qs[h], kbuf[slot, head_rows[h], :]) + bias_ref[h] for h in hs]
    kn = [kn_ref[pad:, h * HEAD_W:(h + 1) * HEAD_W] for h in hs]
    vn = [vn_ref[pad:, h * HEAD_W:(h + 1) * HEAD_W] for h in hs]
    s_new = [[jnp.sum(qs[h] * kn[h][t:t + 1, :], axis=-1, keepdims=True) + nbias_ref[h, t] for t in range(n_real)]
             for h in hs]
    m = [functools.reduce(jnp.maximum, s_new[h], jnp.max(s[h], axis=-1, keepdims=True)) for h in hs]
    p = [jnp.exp(s[h] - m[h]) for h in hs]
    p_new = [[jnp.exp(sn - m[h]) for sn in s_new[h]] for h in hs]
    l = [sum(p_new[h], jnp.sum(p[h], axis=-1, keepdims=True)) for h in hs]
    acc = [_dot(p[h], vbuf[slot, head_rows[h], :]) for h in hs]
    outs = []
    for h in hs:
        a = acc[h]
        for t in range(n_real):
            a = a + p_new[h][t] * vn[h][t:t + 1, :]
        o = a / l[h]
        o = o[:n_real] - lam * o[n_real:]
        outs.append(_rms_norm(o, nw_ref[...]) * out_scale)
    o_ref[0:pad, :] = jnp.zeros((pad, BR_W), F32)
    o_ref[pad:, :] = jnp.concatenate(outs, axis=1)


def _diff_sample(page_table, scalars, proj, k_new, v_new, cache_k, cache_v, layer, bias, nbias, norm_w, n_real):
    n_batch, n_pages = page_table.shape
    n_past = n_pages * PAGE_SIZE
    new_tok = pl.BlockSpec((None, SAMPLE_PAD, BR_W), lambda b, pt: (layer, b, 0))
    grid_spec = pltpu.PrefetchScalarGridSpec(
        num_scalar_prefetch=1,
        grid=(n_batch,),
        in_specs=[pl.BlockSpec(memory_space=pltpu.SMEM),
                  pl.BlockSpec((SAMPLE_PAD, BR_W), lambda b, pt: (b, BLK_CQ)),
                  new_tok, new_tok,
                  pl.BlockSpec(memory_space=pl.ANY), pl.BlockSpec(memory_space=pl.ANY),
                  pl.BlockSpec(bias.shape, lambda b, pt: (0, 0, 0)),
                  pl.BlockSpec(nbias.shape, lambda b, pt: (0, 0, 0, 0)),
                  pl.BlockSpec((1, HEAD_W), lambda b, pt: (0, 0))],
        out_specs=pl.BlockSpec((SAMPLE_PAD, BR_W), lambda b, pt: (b, 0)),
        scratch_shapes=[pltpu.VMEM((2, n_past * N_HEADS, HEAD_W), F32), pltpu.VMEM((2, n_past * N_HEADS, HEAD_W), F32),
                        pltpu.SemaphoreType.DMA((2, 2))],
    )
    return pl.pallas_call(
        functools.partial(_diff_sample_body, layer=layer, n_pages=n_pages, n_real=n_real, n_batch=n_batch),
        grid_spec=grid_spec,
        out_shape=jax.ShapeDtypeStruct((n_batch * SAMPLE_PAD, BR_W), F32),
        compiler_params=_cparams(1),
        name="diff_sample",
    )(page_table, scalars, proj, k_new, v_new, cache_k, cache_v, bias, nbias, norm_w)


def _merge_body(oa_ref, ob_ref, oc_ref, g0_ref, g1_ref, g2_ref, x_ref, wb_ref, wo_ref, ln_ref, o_ref):
    m = (_sigmoid(g0_ref[...]) * _dot(oa_ref[...], wb_ref[0])
         + _sigmoid(g1_ref[...]) * _dot(ob_ref[...], wb_ref[1])
         + _sigmoid(g2_ref[...]) * _dot(oc_ref[...], wb_ref[2]))
    y = _dot(m, wo_ref[...])
    o_ref[...] = _layer_norm(ALPHA * x_ref[...] + y, ln_ref[0:1, :], ln_ref[1:2, :])


def _merge(oa, ob, oc, proj, x, w_branch, w_out, ln):
    t = x.shape[0]
    tm = min(ROW_TILE, t)
    br = pl.BlockSpec((tm, BR_W), lambda i: (i, 0))

    def gate(n):
        return pl.BlockSpec((tm, D_MODEL), lambda i: (i, n))

    return pl.pallas_call(
        _merge_body,
        grid=(t // tm,),
        in_specs=[br, br, br, gate(0), gate(1), gate(2),
                  pl.BlockSpec((tm, D_MODEL), lambda i: (i, 0)),
                  pl.BlockSpec(w_branch.shape, lambda i: (0, 0, 0)),
                  pl.BlockSpec(w_out.shape, lambda i: (0, 0)),
                  pl.BlockSpec((SUBLANES, D_MODEL), lambda i: (0, 0))],
        out_specs=pl.BlockSpec((tm, D_MODEL), lambda i: (i, 0)),
        out_shape=jax.ShapeDtypeStruct((t, D_MODEL), F32),
        compiler_params=_cparams(1),
        name="merge",
    )(oa, ob, oc, proj, proj, proj, x, w_branch, w_out, ln)


def _attend_memory(qs, kvs):
    n = range(len(qs))
    s = [_dot_nt(qs[i], kvs[i][0]) * HEAD_W ** -0.5 for i in n]
    p = [jnp.exp(s[i] - jnp.max(s[i], axis=-1, keepdims=True)) for i in n]
    p = [p[i] / jnp.sum(p[i], axis=-1, keepdims=True) for i in n]
    return [_dot(p[i], kvs[i][1]) for i in n]


def _merge_xattn_prompt_body(oa_ref, ob_ref, oc_ref, g0_ref, g1_ref, g2_ref, x_ref, wb_ref, wo_ref, ln0_ref,
                             mk_ref, mv_ref, wq_ref, wxo_ref, ln1_ref, o_ref):
    m = (_sigmoid(g0_ref[...]) * _dot(oa_ref[...], wb_ref[0])
         + _sigmoid(g1_ref[...]) * _dot(ob_ref[...], wb_ref[1])
         + _sigmoid(g2_ref[...]) * _dot(oc_ref[...], wb_ref[2]))
    x = _layer_norm(ALPHA * x_ref[...] + _dot(m, wo_ref[...]), ln0_ref[0:1, :], ln0_ref[1:2, :])
    mk, mv = mk_ref[0], mv_ref[0]
    q = _dot(x, wq_ref[...])
    o = jnp.concatenate(_attend_memory([_head(q, h) for h in range(N_HEADS)],
                                       [(_head(mk, h), _head(mv, h)) for h in range(N_HEADS)]), axis=1)
    o_ref[...] = _layer_norm(ALPHA * x + _dot(o, wxo_ref[...]), ln1_ref[0:1, :], ln1_ref[1:2, :])


def _merge_xattn_prompt(oa, ob, oc, proj, x, w_branch, w_out, ln0, mkv, w_xq, w_xo, ln1, batch, seq):
    tm = min(ROW_TILE, seq)
    per = seq // tm
    n_mem = mkv.shape[1]

    def rows(width, col):
        return pl.BlockSpec((tm, width), lambda b, i: (b * per + i, col))

    def whole(a):
        return pl.BlockSpec(a.shape, lambda b, i: (0,) * a.ndim)

    ln = pl.BlockSpec((SUBLANES, D_MODEL), lambda b, i: (0, 0))
    return pl.pallas_call(
        _merge_xattn_prompt_body,
        grid=(batch, per),
        in_specs=[rows(BR_W, 0), rows(BR_W, 0), rows(BR_W, 0),
                  rows(D_MODEL, 0), rows(D_MODEL, 1), rows(D_MODEL, 2), rows(D_MODEL, 0),
                  whole(w_branch), whole(w_out), ln,
                  pl.BlockSpec((1, n_mem, BR_W), lambda b, i: (b, 0, 0)),
                  pl.BlockSpec((1, n_mem, BR_W), lambda b, i: (b, 0, 1)),
                  whole(w_xq), whole(w_xo), ln],
        out_specs=rows(D_MODEL, 0),
        out_shape=jax.ShapeDtypeStruct(x.shape, F32),
        compiler_params=_cparams(2),
        name="merge_xattn_prompt",
    )(oa, ob, oc, proj, proj, proj, x, w_branch, w_out, ln0, mkv, mkv, w_xq, w_xo, ln1)


def _xattn_sample_body(x_ref, mk_ref, mv_ref, wq_ref, wo_ref, ln_ref, o_ref, att_ref, *, n_seq):
    x = x_ref[...]
    q = _dot(x, wq_ref[...])
    n_mem = mk_ref.shape[1] // N_HEADS
    tiles = [(i, h) for i in range(n_seq) for h in range(N_HEADS)]
    qs, kvs = [], []
    for i, h in tiles:
        rows = pl.ds(h, n_mem, stride=N_HEADS)
        qs.append(_head(q[i * SAMPLE_PAD:(i + 1) * SAMPLE_PAD], h))
        kvs.append((mk_ref[i, rows, :], mv_ref[i, rows, :]))
    for (i, h), o in zip(tiles, _attend_memory(qs, kvs)):
        att_ref[i * SAMPLE_PAD:(i + 1) * SAMPLE_PAD, h * HEAD_W:(h + 1) * HEAD_W] = o
    o_ref[...] = _layer_norm(ALPHA * x + _dot(att_ref[...], wo_ref[...]), ln_ref[0:1, :], ln_ref[1:2, :])


def _xattn_sample(x, mem_k, mem_v, layer, w_xq, w_xo, ln, seq_per_step=8):
    n_batch, mem_rows = mem_k.shape[1:3]
    rows = seq_per_step * SAMPLE_PAD
    mem = pl.BlockSpec((None, seq_per_step, mem_rows, HEAD_W), lambda g: (layer, g, 0, 0))
    return pl.pallas_call(
        functools.partial(_xattn_sample_body, n_seq=seq_per_step),
        grid=(n_batch // seq_per_step,),
        in_specs=[pl.BlockSpec((rows, D_MODEL), lambda g: (g, 0)), mem, mem,
                  pl.BlockSpec(w_xq.shape, lambda g: (0, 0)),
                  pl.BlockSpec(w_xo.shape, lambda g: (0, 0)),
                  pl.BlockSpec((SUBLANES, D_MODEL), lambda g: (0, 0))],
        out_specs=pl.BlockSpec((rows, D_MODEL), lambda g: (g, 0)),
        out_shape=jax.ShapeDtypeStruct(x.shape, F32),
        scratch_shapes=[pltpu.VMEM((rows, BR_W), F32)],
        compiler_params=_cparams(1),
        name="xattn_sample",
    )(x, mem_k, mem_v, w_xq, w_xo, ln)


def _ffn_body(x_ref, wg_ref, wu_ref, w2_ref, ln_ref, o_ref):
    x = x_ref[...]
    xb = x.astype(BF16)
    gate = jnp.dot(xb, wg_ref[...], preferred_element_type=F32)
    up = jnp.dot(xb, wu_ref[...], preferred_element_type=F32)
    y = _dot(_silu(gate) * up, w2_ref[...])
    o_ref[...] = _layer_norm(ALPHA * x + y, ln_ref[0:1, :], ln_ref[1:2, :])


def _ffn(x, w1, w2, ln):
    t = x.shape[0]
    tm = min(ROW_TILE, t)
    resident = pl.Buffered(1)
    return pl.pallas_call(
        _ffn_body,
        grid=(t // tm,),
        in_specs=[pl.BlockSpec((tm, D_MODEL), lambda i: (i, 0)),
                  pl.BlockSpec((D_MODEL, D_FF), lambda i: (0, 0), pipeline_mode=resident),
                  pl.BlockSpec((D_MODEL, D_FF), lambda i: (0, 1), pipeline_mode=resident),
                  pl.BlockSpec((D_FF, D_MODEL), lambda i: (0, 0), pipeline_mode=resident),
                  pl.BlockSpec((SUBLANES, D_MODEL), lambda i: (0, 0))],
        out_specs=pl.BlockSpec((tm, D_MODEL), lambda i: (i, 0)),
        out_shape=jax.ShapeDtypeStruct(x.shape, F32),
        compiler_params=_cparams(1),
        name="ffn",
    )(x, w1, w1, w2, ln)


def _split_w_in(w):
    ab0 = 8 * BR_W
    c0 = ab0 + 2 * N_HEADS
    g0 = c0 + 3 * BR_W
    pad = jnp.zeros((w.shape[0], LANES - 2 * N_HEADS), w.dtype)
    main = jnp.concatenate([w[:, g0:], w[:, :ab0], w[:, c0:c0 + BR_W]], axis=1)
    kv = jnp.concatenate([w[:, c0 + BR_W:g0], w[:, ab0:c0], pad], axis=1)
    return main.astype(BF16), kv.astype(BF16)


def _ln_rows(g, b):
    return jnp.concatenate([g[None], b[None], jnp.zeros((SUBLANES - 2, g.shape[0]), F32)], axis=0)


def kernel(x_prompt, x_sample, mem_prompt, cache_attn_k, cache_attn_v, cache_mem_k, cache_mem_v, state_hgrn,
           state_gdn, state_gdn_conv, page_table, w_in, w_branch, w_out, lower_bounds, hgrn_norm, gdn_a_log,
           gdn_dt_bias, gdn_conv_w, gdn_norm, diff_lambda, diff_norm, rel_bias, w_xq, w_mem_kv, w_xo, ln_g, ln_b,
           w_ffn_in, w_ffn_out):
    bp, seq, _ = x_prompt.shape
    bs, ls, _ = x_sample.shape
    n_mem = mem_prompt.shape[1]
    n_pages = page_table.shape[1]
    n_past = n_pages * PAGE_SIZE
    n_pool = cache_attn_k.shape[1]
    pad = SAMPLE_PAD - ls

    lb_cum = jnp.cumsum(jax.nn.softmax(lower_bounds.astype(F32), axis=0), axis=0)
    lb = (lb_cum - lb_cum[0]).reshape(DEPTH, N_HEADS, 1, HEAD_W)
    lbp = jnp.concatenate([jnp.log(lb), jnp.log1p(-lb), 1.0 - lb,
                           jnp.zeros((DEPTH, N_HEADS, SUBLANES - 3, HEAD_W), F32)], axis=2)

    n_max = max(2 * ATT_BLOCK + 2, n_past + ls)
    bucket = t5_bucket(-jnp.arange(n_max, dtype=jnp.int32))
    thr = jnp.searchsorted(bucket, jnp.arange(N_BUCKETS, dtype=jnp.int32), side="left").astype(jnp.int32)
    rel_bias_t = rel_bias.astype(F32).T
    tiles = _bias_tiles(thr, rel_bias_t)
    dist = rel_bias_t[:, bucket]
    far_bias = dist[:, ATT_BLOCK + 1]
    q_pos = n_past + jnp.arange(ls)
    past = dist[:, q_pos[:, None] - jnp.arange(n_past)[None, :]]
    bias_s = jnp.concatenate([past, past], axis=1)
    tt = jnp.arange(ls)[:, None] - jnp.arange(ls)[None, :]
    new = jnp.where(tt >= 0, dist[:, jnp.maximum(tt, 0)], NEG_INF)
    nbias = jnp.concatenate([new, new], axis=1).transpose(0, 2, 1)[..., None]

    xp = x_prompt.reshape(bp * seq, D_MODEL)
    xs = jnp.concatenate([jnp.zeros((bs, pad, D_MODEL), F32), x_sample], axis=1).reshape(bs * SAMPLE_PAD, D_MODEL)
    mem = mem_prompt.reshape(bp * n_mem, D_MODEL)
    cache_k = cache_attn_k.reshape(DEPTH, n_pool, PAGE_SIZE * N_HEADS, HEAD_W)
    cache_v = cache_attn_v.reshape(DEPTH, n_pool, PAGE_SIZE * N_HEADS, HEAD_W)
    mem_k = cache_mem_k.reshape(DEPTH, bs, n_mem * N_HEADS, HEAD_W)
    mem_v = cache_mem_v.reshape(DEPTH, bs, n_mem * N_HEADS, HEAD_W)

    pk = pv = jnp.zeros((DEPTH, bp * seq, BR_W), F32)
    sk = sv = jnp.zeros((DEPTH, bs * SAMPLE_PAD, BR_W), F32)
    sh = sg = jnp.zeros(state_hgrn.shape, F32)
    pmk, pmv, ph, pg, pc, sc = [], [], [], [], [], []
    b_col0 = BLK_B * BR_W

    for l in range(DEPTH):
        w_main, w_kv = _split_w_in(w_in[l])
        wb_l = w_branch[l].astype(BF16)
        wo_l = w_out[l].astype(BF16)
        wq_l = w_xq[l].astype(BF16)
        wxo_l = w_xo[l].astype(BF16)
        wkv_l = w_mem_kv[l].astype(BF16)
        w1_l = w_ffn_in[l].astype(BF16)
        w2_l = w_ffn_out[l].astype(BF16)
        lns = [_ln_rows(ln_g[l, i], ln_b[l, i]) for i in range(3)]
        hn = hgrn_norm[l][None]
        gn = gdn_norm[l][None]
        dn = diff_norm[l][None]
        head_params = jnp.stack([jnp.exp(gdn_a_log[l].astype(F32)), gdn_dt_bias[l].astype(F32)])
        lam_init = 0.8 - 0.6 * math.exp(-0.3 * l)
        lq1, lk1, lq2, lk2 = diff_lambda[l].astype(F32)
        lam = jnp.exp(jnp.sum(lq1 * lk1)) - jnp.exp(jnp.sum(lq2 * lk2)) + lam_init
        scalars = jnp.stack([jnp.stack([lam, jnp.float32(1.0 - lam_init), jnp.float32(0), jnp.float32(0)]),
                             far_bias])
        conv_w_l = gdn_conv_w[l]

        proj = _matmul(xp, w_main, 2 * ROW_TILE, MAIN_W // 3)
        pk, pv, ab, k16, v16 = _kv_proj(xp, w_kv, l, pk, pv)
        oa, h_new = _hgrn_prompt(proj, lbp[l], hn, bp, seq)
        qkv = _gdn_conv(proj, BLK_B, conv_w_l, SEQ_TILE, seq)
        ob, g_new = _gdn_prompt(qkv, proj, ab, head_params, gn, bp, seq)
        oc = _diff_prompt(proj, k16, v16, scalars, tiles, dn, bp, seq)
        mkv = _matmul(mem, wkv_l, ROW_TILE, 2 * BR_W).reshape(bp, n_mem, 2 * BR_W)
        x2 = _merge_xattn_prompt(oa, ob, oc, proj, xp, wb_l, wo_l, lns[0], mkv, wq_l, wxo_l, lns[1], bp, seq)
        xp = _ffn(x2, w1_l, w2_l, lns[2])
        pmk.append(mkv[:, :, :BR_W].reshape(bp, n_mem, N_HEADS, HEAD_W))
        pmv.append(mkv[:, :, BR_W:].reshape(bp, n_mem, N_HEADS, HEAD_W))
        ph.append(h_new)
        pg.append(g_new)
        pc.append(proj.reshape(bp, seq, MAIN_W)[:, seq - (CONV_W - 1):, b_col0:b_col0 + CONV_CH])

        proj_s = _matmul(xs, w_main, ROW_TILE, MAIN_W // 3)
        sk, sv, ab_s, _, _ = _kv_proj(xs, w_kv, l, sk, sv)
        oa, sh = _hgrn_sample(proj_s, lbp[l], hn, state_hgrn, l, sh, ls)
        conv_tok = proj_s[:, b_col0:b_col0 + CONV_CH].reshape(bs, SAMPLE_PAD, CONV_CH)
        conv_in = jnp.concatenate([conv_tok[:, :pad - (CONV_W - 1)], state_gdn_conv[l], conv_tok[:, pad:]], axis=1)
        qkv = _gdn_conv(conv_in.reshape(bs * SAMPLE_PAD, CONV_CH), 0, conv_w_l, bs * SAMPLE_PAD, SAMPLE_PAD)
        ob, sg = _gdn_sample(qkv, proj_s, ab_s, head_params, gn, state_gdn, l, sg, ls)
        oc = _diff_sample(page_table, scalars, proj_s, sk, sv, cache_k, cache_v, l, bias_s, nbias, dn, ls)
        x1 = _merge(oa, ob, oc, proj_s, xs, wb_l, wo_l, lns[0])
        x2 = _xattn_sample(x1, mem_k, mem_v, l, wq_l, wxo_l, lns[1])
        xs = _ffn(x2, w1_l, w2_l, lns[2])
        sc.append(conv_in[:, SAMPLE_PAD - (CONV_W - 1):])

    y_prompt = xp.reshape(bp, seq, D_MODEL)
    y_sample = xs.reshape(bs, SAMPLE_PAD, D_MODEL)[:, pad:]
    new_k_p = pk.reshape(DEPTH, bp, seq, N_HEADS, HEAD_W)
    new_v_p = pv.reshape(DEPTH, bp, seq, N_HEADS, HEAD_W)
    new_k_s = sk.reshape(DEPTH, bs, SAMPLE_PAD, N_HEADS, HEAD_W)[:, :, pad:]
    new_v_s = sv.reshape(DEPTH, bs, SAMPLE_PAD, N_HEADS, HEAD_W)[:, :, pad:]
    return (y_prompt, y_sample, new_k_p, new_v_p, jnp.stack(pmk), jnp.stack(pmv), jnp.stack(ph), jnp.stack(pg),
            jnp.stack(pc), new_k_s, new_v_s, sh, sg, jnp.stack(sc))
```

```python
import functools
import math

import jax
import jax.numpy as jnp
from jax import lax
from jax.experimental import pallas as pl
from jax.experimental.pallas import tpu as pltpu

F32 = jnp.float32
BF16 = jnp.bfloat16

D_MODEL = 1024
DEPTH = 4
PAGE_SIZE = 128
BR_W = D_MODEL // 2
N_HEADS = 4
HEAD_W = BR_W // N_HEADS
DH_C = HEAD_W // 2
CONV_W = 4
CONV_CH = 3 * BR_W
N_BRANCH = 3
N_BUCKETS = 32
MAX_DISTANCE = 128
D_FF = -(-8 * D_MODEL // (3 * 256)) * 256
LN_EPS = 1e-5
NORM_EPS = 1e-6
NEG_INF = -1e30
ALPHA = (2 * DEPTH) ** 0.25

LANES = 128
SUBLANES = 8
VMEM_LIMIT_BYTES = 56 * 1024 * 1024

BLK_A = N_BRANCH * D_MODEL // BR_W
BLK_B = BLK_A + 4
BLK_CQ = BLK_B + 4
MAIN_W = (BLK_CQ + 1) * BR_W
KV_W = 2 * BR_W + LANES

HGRN_SUB = 16
HGRN_ROWS = 64
GDN_CHUNK = 64
SOLVE_BLOCK = 16
SAMPLE_PAD = 8
SEQ_TILE = 512
GDN_TILE = 1024
ATT_BLOCK = 512
ATT_HEADS = 4
ROW_TILE = 512


def _cparams(n_axes):
    return pltpu.CompilerParams(dimension_semantics=("arbitrary",) * n_axes,
                                vmem_limit_bytes=VMEM_LIMIT_BYTES)


def _dot(a, b):
    return jnp.dot(a.astype(BF16), b.astype(BF16), preferred_element_type=F32)


def _dot_nt(a, b):
    return lax.dot_general(a.astype(BF16), b.astype(BF16), (((1,), (1,)), ((), ())),
                           preferred_element_type=F32)


def _dot_tn(a, b):
    return lax.dot_general(a.astype(BF16), b.astype(BF16), (((0,), (0,)), ((), ())),
                           preferred_element_type=F32)


def _dot_exact_lhs(m01, x):
    hi = x.astype(BF16)
    r1 = x - hi.astype(F32)
    mid = r1.astype(BF16)
    lo = (r1 - mid.astype(F32)).astype(BF16)
    dot = functools.partial(jnp.dot, preferred_element_type=F32)
    return dot(m01, hi) + dot(m01, mid) + dot(m01, lo)


def _sigmoid(x):
    return 1.0 / (1.0 + jnp.exp(-x))


def _silu(x):
    return x * _sigmoid(x)


def _softplus(x):
    return jnp.maximum(x, 0.0) + jnp.log1p(jnp.exp(-jnp.abs(x)))


def _layer_norm(h, g, b):
    mu = jnp.mean(h, axis=-1, keepdims=True)
    c = h - mu
    var = jnp.mean(c * c, axis=-1, keepdims=True)
    return c * lax.rsqrt(var + LN_EPS) * g + b


def _rms_norm(x, g):
    return x * lax.rsqrt(jnp.mean(x * x, axis=-1, keepdims=True) + NORM_EPS) * g


def _bcast_rows(x, rows, reps):
    return jnp.concatenate([jnp.broadcast_to(x[r:r + 1, :], (reps, x.shape[1])) for r in rows], axis=0)


def _head(x, h):
    return x[:, h * HEAD_W:(h + 1) * HEAD_W]


ALIASED = pl.BlockSpec(memory_space=pl.ANY)


def _mm_body(x_ref, w_ref, o_ref):
    o_ref[...] = _dot(x_ref[...], w_ref[...])


def _matmul(x, w, tm, tn):
    t, k = x.shape
    n = w.shape[1]
    tm = min(tm, t)
    return pl.pallas_call(
        _mm_body,
        grid=(n // tn, t // tm),
        in_specs=[pl.BlockSpec((tm, k), lambda j, i: (i, 0)),
                  pl.BlockSpec((k, tn), lambda j, i: (0, j))],
        out_specs=pl.BlockSpec((tm, tn), lambda j, i: (i, j)),
        out_shape=jax.ShapeDtypeStruct((t, n), F32),
        compiler_params=_cparams(2),
        name="proj_matmul",
    )(x, w)


def _kv_body(x_ref, w_ref, kp_ref, vp_ref, k_ref, v_ref, ab_ref, k16_ref, v16_ref, *, head_rows):
    y = _dot(x_ref[...], w_ref[...])
    k = y[:, :BR_W]
    v = y[:, BR_W:2 * BR_W]
    if head_rows:
        for h in range(N_HEADS):
            rows = pl.ds(h, k.shape[0], stride=N_HEADS)
            k_ref[rows, :] = _head(k, h)
            v_ref[rows, :] = _head(v, h)
    else:
        k_ref[...] = k
        v_ref[...] = v
    ab_ref[...] = y[:, 2 * BR_W:]
    k16_ref[...] = k.astype(BF16)
    ones = jnp.ones((v.shape[0], HEAD_W), BF16)
    v16_ref[...] = jnp.concatenate([piece for h in range(N_HEADS) for piece in (_head(v, h).astype(BF16), ones)],
                                   axis=1)


def _kv_proj(x, w, layer, k_prev, v_prev, head_rows):
    t = x.shape[0]
    tm = min(ROW_TILE, t)
    slab_shape = (tm * N_HEADS, HEAD_W) if head_rows else (tm, BR_W)
    slab = pl.BlockSpec((None,) + slab_shape, lambda i: (layer, i, 0))
    stacked = jax.ShapeDtypeStruct(k_prev.shape, F32)
    return pl.pallas_call(
        functools.partial(_kv_body, head_rows=head_rows),
        grid=(t // tm,),
        in_specs=[pl.BlockSpec((tm, D_MODEL), lambda i: (i, 0)),
                  pl.BlockSpec(w.shape, lambda i: (0, 0)), ALIASED, ALIASED],
        out_specs=[slab, slab, pl.BlockSpec((tm, LANES), lambda i: (i, 0)),
                   pl.BlockSpec((tm, BR_W), lambda i: (i, 0)), pl.BlockSpec((tm, 2 * BR_W), lambda i: (i, 0))],
        out_shape=[stacked, stacked, jax.ShapeDtypeStruct((t, LANES), F32),
                   jax.ShapeDtypeStruct((t, BR_W), BF16), jax.ShapeDtypeStruct((t, 2 * BR_W), BF16)],
        input_output_aliases={2: 0, 3: 1},
        compiler_params=_cparams(1),
        name="kv_proj",
    )(x, w, k_prev, v_prev)


def _hgrn_rows(q, z, v, lbp, sub, masks, valid):
    r = q.shape[0]
    n_sub = r // sub
    tri, pos = masks
    log_lb, log1m_lb, om_lb = lbp[0:1, :], lbp[1:2, :], lbp[2:3, :]
    log_sig = jnp.minimum(z, 0.0) - jnp.log(1.0 + jnp.exp(-jnp.abs(z)))
    t2 = log1m_lb + log_sig
    logf = jnp.maximum(log_lb, t2) + jnp.log(1.0 + jnp.exp(-jnp.abs(log_lb - t2)))
    k = om_lb * (1.0 / (1.0 + jnp.exp(z)))
    if valid is not None:
        logf = jnp.where(valid, logf, 0.0)
        k = jnp.where(valid, k, 0.0)
    bc = _dot_exact_lhs(tri, logf)
    last_rows = [i * sub + sub - 1 for i in range(n_sub)]
    b_last = _bcast_rows(bc, last_rows, sub)
    q_hat = q * jnp.exp(bc)
    k_hat = k * jnp.exp(b_last - bc)
    o = jnp.zeros_like(q)
    for s in range(sub):
        rows = [i * sub + s for i in range(n_sub)]
        ks = _bcast_rows(k, rows, sub)
        bs = _bcast_rows(bc, rows, sub)
        vs = _bcast_rows(v, rows, sub)
        a = jnp.where(pos >= s, q * ks * jnp.exp(bc - bs), 0.0)
        o = o + jnp.sum(a, axis=-1, keepdims=True) * vs
    return q_hat, k_hat, o, bc


def _sub_chunk_masks(rows, sub):
    ri = lax.broadcasted_iota(jnp.int32, (rows, rows), 0)
    ci = lax.broadcasted_iota(jnp.int32, (rows, rows), 1)
    tri = ((ri // sub == ci // sub) & (ci <= ri)).astype(BF16)
    pos = lax.broadcasted_iota(jnp.int32, (rows, 1), 0) % sub
    return tri, pos


def _hgrn_finish(o, g, norm_w):
    return _rms_norm(o, norm_w) * _silu(g)


def _hgrn_prompt_body(q_ref, f_ref, i_ref, g_ref, lbp_ref, nw_ref, o_ref, s_out_ref, st_ref, *, n_tiles):
    t = pl.program_id(1)

    @pl.when(t == 0)
    def _():
        st_ref[...] = jnp.zeros_like(st_ref)

    rows = HGRN_ROWS
    nw = nw_ref[...]
    masks = _sub_chunk_masks(rows, HGRN_SUB)

    def chunk(c, carry):
        sl = pl.ds(pl.multiple_of(c * rows, rows), rows)
        qa, za, va, ga = q_ref[sl, :], f_ref[sl, :], i_ref[sl, :], g_ref[sl, :]
        hs = range(N_HEADS)
        v = [_head(va, h) for h in hs]
        parts = [_hgrn_rows(_head(qa, h), _head(za, h), v[h], lbp_ref[h], HGRN_SUB, masks, None) for h in hs]
        st = [st_ref[h] for h in hs]
        outs = [[] for _ in hs]
        for i in range(rows // HGRN_SUB):
            lo, hi = i * HGRN_SUB, (i + 1) * HGRN_SUB
            for h in hs:
                q_hat, k_hat, o, bc = parts[h]
                outs[h].append(o[lo:hi] + _dot_nt(q_hat[lo:hi], st[h]))
                st[h] = st[h] * jnp.exp(bc[hi - 1:hi, :]) + _dot_tn(v[h][lo:hi], k_hat[lo:hi])
        for h in hs:
            st_ref[h] = st[h]
        o_ref[sl, :] = jnp.concatenate(
            [_hgrn_finish(jnp.concatenate(outs[h], axis=0), _head(ga, h), nw) for h in hs], axis=1)
        return carry

    lax.fori_loop(0, SEQ_TILE // rows, chunk, 0)

    @pl.when(t == n_tiles - 1)
    def _():
        for h in range(N_HEADS):
            s_out_ref[0, h] = st_ref[h].T


def _hgrn_prompt(proj, lbp, norm_w, batch, seq):
    n_tiles = seq // SEQ_TILE

    def col(blk):
        return pl.BlockSpec((SEQ_TILE, BR_W), lambda b, t: (b * n_tiles + t, blk))

    return pl.pallas_call(
        functools.partial(_hgrn_prompt_body, n_tiles=n_tiles),
        grid=(batch, n_tiles),
        in_specs=[col(BLK_A), col(BLK_A + 1), col(BLK_A + 2), col(BLK_A + 3),
                  pl.BlockSpec((N_HEADS, SUBLANES, HEAD_W), lambda b, t: (0, 0, 0)),
                  pl.BlockSpec((1, HEAD_W), lambda b, t: (0, 0))],
        out_specs=[pl.BlockSpec((SEQ_TILE, BR_W), lambda b, t: (b * n_tiles + t, 0)),
                   pl.BlockSpec((1, N_HEADS, HEAD_W, HEAD_W), lambda b, t: (b, 0, 0, 0))],
        out_shape=[jax.ShapeDtypeStruct((batch * seq, BR_W), F32),
                   jax.ShapeDtypeStruct((batch, N_HEADS, HEAD_W, HEAD_W), F32)],
        scratch_shapes=[pltpu.VMEM((N_HEADS, HEAD_W, HEAD_W), F32)],
        compiler_params=_cparams(2),
        name="hgrn_prompt",
    )(proj, proj, proj, proj, lbp, norm_w)


def _hgrn_sample_body(q_ref, f_ref, i_ref, g_ref, lbp_ref, nw_ref, s0_ref, *refs, n_seq, n_real):
    o_ref, s_out_ref = refs[-2:]
    nw = nw_ref[...]
    rows = n_seq * SAMPLE_PAD
    masks = _sub_chunk_masks(rows, SAMPLE_PAD)
    valid = masks[1] >= SAMPLE_PAD - n_real
    qa, za, va, ga = q_ref[...], f_ref[...], i_ref[...], g_ref[...]
    hs = range(N_HEADS)
    v = [_head(va, h) for h in hs]
    parts = [_hgrn_rows(_head(qa, h), _head(za, h), v[h], lbp_ref[h], SAMPLE_PAD, masks, valid) for h in hs]
    eye = (lax.broadcasted_iota(jnp.int32, (HEAD_W, HEAD_W), 0)
           == lax.broadcasted_iota(jnp.int32, (HEAD_W, HEAD_W), 1))
    outs = [[] for _ in hs]
    for i in range(n_seq):
        lo, hi = i * SAMPLE_PAD, (i + 1) * SAMPLE_PAD
        for h in hs:
            q_hat, k_hat, o, bc = parts[h]
            s0 = s0_ref[i, h]
            outs[h].append(o[lo:hi] + _dot(q_hat[lo:hi], s0))
            decay = jnp.exp(bc[hi - 1:hi, :])
            decay_col = jnp.sum(jnp.where(eye, decay, 0.0), axis=-1, keepdims=True)
            s_out_ref[i, h] = s0 * decay_col + _dot_tn(k_hat[lo:hi], v[h][lo:hi])
    o_ref[...] = jnp.concatenate(
        [_hgrn_finish(jnp.concatenate(outs[h], axis=0), _head(ga, h), nw) for h in hs], axis=1)


def _state_specs(layer, seq_per_step):
    return pl.BlockSpec((None, seq_per_step, N_HEADS, HEAD_W, HEAD_W), lambda g: (layer, g, 0, 0, 0))


def _hgrn_sample(proj, lbp, norm_w, states, layer, prev, n_real, seq_per_step=8):
    n_batch = states.shape[1]
    rows = seq_per_step * SAMPLE_PAD

    def col(blk):
        return pl.BlockSpec((rows, BR_W), lambda g: (g, blk))

    return pl.pallas_call(
        functools.partial(_hgrn_sample_body, n_seq=seq_per_step, n_real=n_real),
        grid=(n_batch // seq_per_step,),
        in_specs=[col(BLK_A), col(BLK_A + 1), col(BLK_A + 2), col(BLK_A + 3),
                  pl.BlockSpec((N_HEADS, SUBLANES, HEAD_W), lambda g: (0, 0, 0)),
                  pl.BlockSpec((1, HEAD_W), lambda g: (0, 0)),
                  _state_specs(layer, seq_per_step), ALIASED],
        out_specs=[pl.BlockSpec((rows, BR_W), lambda g: (g, 0)), _state_specs(layer, seq_per_step)],
        out_shape=[jax.ShapeDtypeStruct((n_batch * SAMPLE_PAD, BR_W), F32),
                   jax.ShapeDtypeStruct(states.shape, F32)],
        input_output_aliases={7: 1},
        compiler_params=_cparams(1),
        name="hgrn_sample",
    )(proj, proj, proj, proj, lbp, norm_w, states, prev)


def _conv_body(prev_ref, cur_ref, w_ref, o_ref, buf_ref, *, tile, seq_len):
    i = pl.program_id(0)
    c = pl.program_id(1)
    buf_ref[0:SUBLANES, :] = prev_ref[...]
    buf_ref[SUBLANES:, :] = cur_ref[...]
    pos = (i * tile + lax.broadcasted_iota(jnp.int32, (tile, 1), 0)) % seq_len
    w = w_ref[...]
    acc = cur_ref[...] * w[CONV_W - 1:CONV_W, :]
    for back in range(1, CONV_W):
        shifted = buf_ref[pl.ds(SUBLANES - back, tile), :]
        acc = acc + jnp.where(pos >= back, shifted, 0.0) * w[CONV_W - 1 - back:CONV_W - back, :]
    act = _silu(acc)
    q_scale = jnp.where(c == 0, HEAD_W ** -0.5, 1.0)
    heads = []
    for h in range(N_HEADS):
        a = _head(act, h)
        inv = lax.rsqrt(jnp.sum(a * a, axis=-1, keepdims=True) + NORM_EPS)
        heads.append(a * jnp.where(c < 2, inv * q_scale, 1.0))
    o_ref[...] = jnp.concatenate(heads, axis=1)


def _gdn_conv(conv_in, col0, conv_w, tile, seq_len):
    t = conv_in.shape[0]
    per = tile // SUBLANES
    return pl.pallas_call(
        functools.partial(_conv_body, tile=tile, seq_len=seq_len),
        grid=(t // tile, CONV_CH // BR_W),
        in_specs=[pl.BlockSpec((SUBLANES, BR_W), lambda i, c: (jnp.maximum(i * per - 1, 0), col0 + c)),
                  pl.BlockSpec((tile, BR_W), lambda i, c: (i, col0 + c)),
                  pl.BlockSpec((CONV_W, BR_W), lambda i, c: (0, c))],
        out_specs=pl.BlockSpec((tile, BR_W), lambda i, c: (i, c)),
        out_shape=jax.ShapeDtypeStruct((t, CONV_CH), F32),
        scratch_shapes=[pltpu.VMEM((tile + SUBLANES, BR_W), F32)],
        compiler_params=_cparams(2),
        name="gdn_conv",
    )(conv_in, conv_in, conv_w)


def _dot_f32(a, b):
    ah = a.astype(BF16)
    al = (a - ah.astype(F32)).astype(BF16)
    bh = b.astype(BF16)
    bl = (b - bh.astype(F32)).astype(BF16)
    dot = functools.partial(jnp.dot, preferred_element_type=F32)
    return dot(ah, bh) + dot(ah, bl) + dot(al, bh)


def _forward_substitute(lmats, rhss):
    n_sys = len(lmats)
    c = lmats[0].shape[0]
    nb = min(c, SOLVE_BLOCK)
    done = [[] for _ in range(n_sys)]
    for i0 in range(0, c, nb):
        xbs, lbs = [], []
        for i in range(n_sys):
            r = rhss[i][i0:i0 + nb, :]
            if i0:
                r = r - _dot_f32(lmats[i][i0:i0 + nb, :i0], jnp.concatenate(done[i], axis=0))
            xbs.append([r[s:s + SUBLANES, :] for s in range(0, nb, SUBLANES)])
            lbs.append([lmats[i][i0 + s:i0 + s + SUBLANES, :] for s in range(0, nb, SUBLANES)])
        for j in range(nb - 1):
            for i in range(n_sys):
                xb = xbs[i]
                xj = xb[j // SUBLANES][j % SUBLANES:j % SUBLANES + 1, :]
                for b in range((j + 1) // SUBLANES, len(xb)):
                    xb[b] = xb[b] - lbs[i][b][:, i0 + j:i0 + j + 1] * xj
        for i in range(n_sys):
            done[i].extend(xbs[i])
    return [jnp.concatenate(d, axis=0) if len(d) > 1 else d[0] for d in done]


def _gdn_masks(c):
    ri = lax.broadcasted_iota(jnp.int32, (c, c), 0)
    ci = lax.broadcasted_iota(jnp.int32, (c, c), 1)
    mr = lax.broadcasted_iota(jnp.int32, (c, LANES), 0)
    mc = lax.broadcasted_iota(jnp.int32, (c, LANES), 1)
    return ci <= ri, ci < ri, ((mc < mr) & (mc < c)) | (mc == c)


def _gdn_prepare(hp_ref, qa, ka, va, ab, masks, valid):
    c = qa.shape[0]
    incl, strict, sel = masks
    hs = range(N_HEADS)
    lane = lax.broadcasted_iota(jnp.int32, ab.shape, 1)
    q = [_head(qa, h) for h in hs]
    k = [_head(ka, h) for h in hs]
    g, beta = [], []
    for h in hs:
        a_col = jnp.sum(jnp.where(lane == h, ab, 0.0), axis=-1, keepdims=True)
        b_col = jnp.sum(jnp.where(lane == h + N_HEADS, ab, 0.0), axis=-1, keepdims=True)
        g_h = -hp_ref[0, h] * _softplus(a_col + hp_ref[1, h])
        beta_h = _sigmoid(b_col)
        if valid is not None:
            g_h = jnp.where(valid, g_h, 0.0)
            beta_h = jnp.where(valid, beta_h, 0.0)
        g.append(g_h)
        beta.append(beta_h)
    incl_b = incl.astype(BF16)
    m = [_dot_exact_lhs(incl_b, jnp.where(sel, g[h], 0.0)) for h in hs]
    gcum = [m[h][:, c:c + 1] for h in hs]
    decay = [jnp.where(incl, jnp.exp(jnp.where(incl, m[h][:, :c], 0.0)), 0.0) for h in hs]
    kb = [k[h] * beta[h] for h in hs]
    lmat = [jnp.where(strict, _dot_nt(kb[h], k[h]) * decay[h], 0.0) for h in hs]
    e_g = [jnp.exp(gcum[h]) for h in hs]
    sol = _forward_substitute(lmat, [jnp.concatenate([_head(va, h) * beta[h], kb[h] * e_g[h]], axis=1) for h in hs])
    attn = [jnp.where(incl, _dot_nt(q[h], k[h]) * decay[h], 0.0) for h in hs]
    out = []
    for h in hs:
        g_last = gcum[h][c - 1:c, :]
        out.append((sol[h][:, :HEAD_W], sol[h][:, HEAD_W:], q[h] * e_g[h], k[h] * jnp.exp(g_last - gcum[h]),
                    attn[h], jnp.exp(g_last)))
    return out


def _gdn_apply(prep, ga, nw, get_state, put_state):
    hs = range(N_HEADS)
    s = [get_state(h) for h in hs]
    v_new = [prep[h][0] - _dot(prep[h][1], s[h]) for h in hs]
    o = [_dot(prep[h][2], s[h]) + _dot(prep[h][4], v_new[h]) for h in hs]
    for h in hs:
        put_state(h, s[h] * prep[h][5] + _dot_tn(prep[h][3], v_new[h]))
    return jnp.concatenate([_rms_norm(o[h], nw) * _silu(_head(ga, h)) for h in hs], axis=1)


def _gdn_prompt_body(hp_ref, q_ref, k_ref, v_ref, g_ref, ab_ref, nw_ref, o_ref, s_out_ref, s_ref, pre_ref, last_ref,
                     *, n_tiles):
    t = pl.program_id(1)

    @pl.when(t == 0)
    def _():
        s_ref[...] = jnp.zeros_like(s_ref)

    nw = nw_ref[...]
    masks = _gdn_masks(GDN_CHUNK)
    n_chunks = q_ref.shape[0] // GDN_CHUNK

    def put(h, s_new):
        s_ref[h] = s_new

    def rows_of(c):
        return pl.ds(pl.multiple_of(c * GDN_CHUNK, GDN_CHUNK), GDN_CHUNK)

    def prepare_into(c, slot):
        sl = rows_of(c)
        prep = _gdn_prepare(hp_ref, q_ref[sl, :], k_ref[sl, :], v_ref[sl, :], ab_ref[sl, :], masks, None)
        for h, (u, w, qe, ke, attn, e_last) in enumerate(prep):
            for i, x in enumerate((u, w, qe, ke)):
                pre_ref[slot, h, i] = x
            pre_ref[slot, h, 4, :, :GDN_CHUNK] = attn
            last_ref[slot, h] = jnp.broadcast_to(e_last, (SUBLANES, LANES))

    prepare_into(0, 0)

    def chunk(c, carry):
        slot = c % 2
        prep = [(pre_ref[slot, h, 0], pre_ref[slot, h, 1], pre_ref[slot, h, 2], pre_ref[slot, h, 3],
                 pre_ref[slot, h, 4, :, :GDN_CHUNK], last_ref[slot, h, 0:1, 0:1]) for h in range(N_HEADS)]
        sl = rows_of(c)
        o_ref[sl, :] = _gdn_apply(prep, g_ref[sl, :], nw, lambda h: s_ref[h], put)
        prepare_into(jnp.minimum(c + 1, n_chunks - 1), 1 - slot)
        return carry

    lax.fori_loop(0, n_chunks, chunk, 0)

    @pl.when(t == n_tiles - 1)
    def _():
        s_out_ref[0] = s_ref[...]


def _gdn_prompt(qkv, proj, ab, head_params, norm_w, batch, seq):
    tile = min(GDN_TILE, seq)
    n_tiles = seq // tile

    def col(blk):
        return pl.BlockSpec((tile, BR_W), lambda b, t: (b * n_tiles + t, blk))

    return pl.pallas_call(
        functools.partial(_gdn_prompt_body, n_tiles=n_tiles),
        grid=(batch, n_tiles),
        in_specs=[pl.BlockSpec(memory_space=pltpu.SMEM),
                  col(0), col(1), col(2), col(BLK_B + 3),
                  pl.BlockSpec((tile, LANES), lambda b, t: (b * n_tiles + t, 0)),
                  pl.BlockSpec((1, HEAD_W), lambda b, t: (0, 0))],
        out_specs=[pl.BlockSpec((tile, BR_W), lambda b, t: (b * n_tiles + t, 0)),
                   pl.BlockSpec((1, N_HEADS, HEAD_W, HEAD_W), lambda b, t: (b, 0, 0, 0))],
        out_shape=[jax.ShapeDtypeStruct((batch * seq, BR_W), F32),
                   jax.ShapeDtypeStruct((batch, N_HEADS, HEAD_W, HEAD_W), F32)],
        scratch_shapes=[pltpu.VMEM((N_HEADS, HEAD_W, HEAD_W), F32),
                        pltpu.VMEM((2, N_HEADS, 5, GDN_CHUNK, HEAD_W), F32),
                        pltpu.VMEM((2, N_HEADS, SUBLANES, LANES), F32)],
        compiler_params=_cparams(2),
        name="gdn_prompt",
    )(head_params, qkv, qkv, qkv, proj, ab, norm_w)


def _gdn_sample_body(hp_ref, q_ref, k_ref, v_ref, g_ref, ab_ref, nw_ref, s0_ref, *refs, n_seq, n_real):
    o_ref, s_out_ref = refs[-2:]
    nw = nw_ref[...]
    masks = _gdn_masks(SAMPLE_PAD)
    valid = lax.broadcasted_iota(jnp.int32, (SAMPLE_PAD, 1), 0) >= SAMPLE_PAD - n_real

    def one(i, carry):
        sl = pl.ds(pl.multiple_of(i * SAMPLE_PAD, SAMPLE_PAD), SAMPLE_PAD)

        def put(h, s_new):
            s_out_ref[i, h] = s_new

        prep = _gdn_prepare(hp_ref, q_ref[sl, :], k_ref[sl, :], v_ref[sl, :], ab_ref[sl, :], masks, valid)
        o_ref[sl, :] = _gdn_apply(prep, g_ref[sl, :], nw, lambda h: s0_ref[i, h], put)
        return carry

    lax.fori_loop(0, n_seq, one, 0)


def _gdn_sample(qkv, proj, ab, head_params, norm_w, states, layer, prev, n_real, seq_per_step=8):
    n_batch = states.shape[1]
    rows = seq_per_step * SAMPLE_PAD

    def col(blk):
        return pl.BlockSpec((rows, BR_W), lambda g: (g, blk))

    return pl.pallas_call(
        functools.partial(_gdn_sample_body, n_seq=seq_per_step, n_real=n_real),
        grid=(n_batch // seq_per_step,),
        in_specs=[pl.BlockSpec(memory_space=pltpu.SMEM),
                  col(0), col(1), col(2), col(BLK_B + 3),
                  pl.BlockSpec((rows, LANES), lambda g: (g, 0)),
                  pl.BlockSpec((1, HEAD_W), lambda g: (0, 0)),
                  _state_specs(layer, seq_per_step), ALIASED],
        out_specs=[pl.BlockSpec((rows, BR_W), lambda g: (g, 0)), _state_specs(layer, seq_per_step)],
        out_shape=[jax.ShapeDtypeStruct((n_batch * SAMPLE_PAD, BR_W), F32),
                   jax.ShapeDtypeStruct(states.shape, F32)],
        input_output_aliases={8: 1},
        compiler_params=_cparams(1),
        name="gdn_sample",
    )(head_params, qkv, qkv, qkv, proj, ab, norm_w, states, prev)


def t5_bucket(rel):
    n = jnp.maximum(-rel, 0)
    exact = N_BUCKETS // 2
    log_part = jnp.log(jnp.maximum(n, 1).astype(F32) / exact) / math.log(MAX_DISTANCE / exact)
    large = jnp.minimum(exact + (log_part * (N_BUCKETS - exact)).astype(jnp.int32), N_BUCKETS - 1)
    return jnp.where(n < exact, n, large)


def _bias_tile_body(thr_ref, rb_ref, o_ref):
    h = pl.program_id(0)
    blk = ATT_BLOCK
    row = lax.broadcasted_iota(jnp.int32, (blk, blk), 0)
    col = lax.broadcasted_iota(jnp.int32, (blk, blk), 1)
    for which in range(2):
        n = row - col + which * blk
        val = jnp.full((blk, blk), rb_ref[h, 0], F32)
        for b in range(1, N_BUCKETS):
            val = jnp.where(n >= thr_ref[b], rb_ref[h, b], val)
        if which == 0:
            val = jnp.where(n >= 0, val, NEG_INF)
        o_ref[which] = val


def _bias_tiles(thr, rel_bias_t):
    return pl.pallas_call(
        _bias_tile_body,
        grid=(N_HEADS,),
        in_specs=[pl.BlockSpec(memory_space=pltpu.SMEM), pl.BlockSpec(memory_space=pltpu.SMEM)],
        out_specs=pl.BlockSpec((None, 2, ATT_BLOCK, ATT_BLOCK), lambda h: (h, 0, 0, 0)),
        out_shape=jax.ShapeDtypeStruct((N_HEADS, 2, ATT_BLOCK, ATT_BLOCK), F32),
        compiler_params=_cparams(1),
        name="bias_tiles",
    )(thr, rel_bias_t)


def _stack_maps(q):
    lane = lax.broadcasted_iota(jnp.int32, q.shape, 1)
    return jnp.concatenate([jnp.where(lane < DH_C, q, 0.0), jnp.where(lane >= DH_C, q, 0.0)], axis=0)


def _diff_prompt_body(sc_ref, q_ref, k_ref, v_ref, d_ref, nw_ref, o_ref, m_ref, acc_ref):
    hp = pl.program_id(1)
    qi = pl.program_id(2)
    blk = ATT_BLOCK
    lam = sc_ref[0, 0]
    out_scale = sc_ref[0, 1]
    qs = [_stack_maps(_head(q_ref[...], hh) * DH_C ** -0.5).astype(BF16) for hh in range(ATT_HEADS)]
    m_ref[...] = jnp.full_like(m_ref, -jnp.inf)
    acc_ref[...] = jnp.zeros_like(acc_ref)

    def block(kj, bias_of):
        sl = pl.ds(pl.multiple_of(kj * blk, blk), blk)
        for hh in range(ATT_HEADS):
            s = lax.dot_general(qs[hh], k_ref[sl, hh * HEAD_W:(hh + 1) * HEAD_W], (((1,), (1,)), ((), ())),
                                preferred_element_type=F32)
            bias = bias_of(hh)
            if bias.ndim == 2:
                s = (s.reshape(2, blk, blk) + bias[None]).reshape(2 * blk, blk)
            else:
                s = s + bias
            m_old = m_ref[hh]
            m_new = jnp.maximum(m_old, jnp.max(s, axis=-1, keepdims=True))
            alpha = jnp.exp(m_old - m_new)
            p = jnp.exp(s - jnp.concatenate([m_new] * (blk // LANES), axis=1))
            pv = jnp.dot(p.astype(BF16), v_ref[sl, hh * 2 * HEAD_W:(hh + 1) * 2 * HEAD_W],
                         preferred_element_type=F32)
            acc_ref[hh] = jnp.concatenate([alpha, alpha], axis=1) * acc_ref[hh] + pv
            m_ref[hh] = m_new

    def far(kj, carry):
        block(kj, lambda hh: sc_ref[1, hp * ATT_HEADS + hh])
        return carry

    lax.fori_loop(0, jnp.maximum(qi - 1, 0), far, 0)

    @pl.when(qi >= 1)
    def _():
        block(qi - 1, lambda hh: d_ref[hh, 1])

    block(qi, lambda hh: d_ref[hh, 0])
    outs = []
    for hh in range(ATT_HEADS):
        acc = acc_ref[hh]
        o = acc[:, :HEAD_W] / acc[:, HEAD_W:]
        o = o[:blk] - lam * o[blk:]
        outs.append(_rms_norm(o, nw_ref[...]) * out_scale)
    o_ref[...] = jnp.concatenate(outs, axis=1)


def _diff_prompt(proj, k16, v16, scalars, tiles, norm_w, batch, seq):
    nq = seq // ATT_BLOCK
    width = ATT_HEADS * HEAD_W
    per_row = BR_W // width
    return pl.pallas_call(
        _diff_prompt_body,
        grid=(batch, N_HEADS // ATT_HEADS, nq),
        in_specs=[pl.BlockSpec(memory_space=pltpu.SMEM),
                  pl.BlockSpec((ATT_BLOCK, width), lambda b, h, q: (b * nq + q, BLK_CQ * per_row + h)),
                  pl.BlockSpec((seq, width), lambda b, h, q: (b, h)),
                  pl.BlockSpec((seq, 2 * width), lambda b, h, q: (b, h)),
                  pl.BlockSpec((ATT_HEADS, 2, ATT_BLOCK, ATT_BLOCK), lambda b, h, q: (h, 0, 0, 0),
                               pipeline_mode=pl.Buffered(1) if ATT_HEADS == N_HEADS else None),
                  pl.BlockSpec((1, HEAD_W), lambda b, h, q: (0, 0))],
        out_specs=pl.BlockSpec((ATT_BLOCK, width), lambda b, h, q: (b * nq + q, h)),
        out_shape=jax.ShapeDtypeStruct((batch * seq, BR_W), F32),
        scratch_shapes=[pltpu.VMEM((ATT_HEADS, 2 * ATT_BLOCK, LANES), F32),
                        pltpu.VMEM((ATT_HEADS, 2 * ATT_BLOCK, 2 * HEAD_W), F32)],
        compiler_params=_cparams(3),
        name="diff_prompt",
    )(scalars, proj, k16, v16, tiles, norm_w)


def _diff_sample_body(pt_ref, sc_ref, q_ref, kn_ref, vn_ref, ck_ref, cv_ref, bias_ref, nbias_ref, nw_ref,
                      o_ref, kbuf, vbuf, sem, *, layer, n_pages, n_real, n_batch):
    b = pl.program_id(0)
    slot = b % 2
    pad = SAMPLE_PAD - n_real

    def page_copies(seq_idx, sl):
        cps = []
        for j in range(n_pages):
            page = pt_ref[seq_idx, j]
            rows = pl.ds(j * PAGE_SIZE * N_HEADS, PAGE_SIZE * N_HEADS)
            cps.append(pltpu.make_async_copy(ck_ref.at[layer, page], kbuf.at[sl, rows], sem.at[sl, 0]))
            cps.append(pltpu.make_async_copy(cv_ref.at[layer, page], vbuf.at[sl, rows], sem.at[sl, 1]))
        return cps

    @pl.when(b == 0)
    def _():
        for cp in page_copies(0, 0):
            cp.start()

    @pl.when(b + 1 < n_batch)
    def _():
        for cp in page_copies(b + 1, 1 - slot):
            cp.start()

    for cp in page_copies(b, slot):
        cp.wait()

    lam = sc_ref[0, 0]
    out_scale = sc_ref[0, 1]
    hs = range(N_HEADS)
    head_rows = [pl.ds(h, n_pages * PAGE_SIZE, stride=N_HEADS) for h in hs]
    qs = [_stack_maps(q_ref[pad:, h * HEAD_W:(h + 1) * HEAD_W] * DH_C ** -0.5) for h in hs]
    s = [_dot_nt(qs[h], kbuf[slot, head_rows[h], :]) + bias_ref[h] for h in hs]
    kn = [kn_ref[pad:, h * HEAD_W:(h + 1) * HEAD_W] for h in hs]
    vn = [vn_ref[pad:, h * HEAD_W:(h + 1) * HEAD_W] for h in hs]
    s_new = [[jnp.sum(qs[h] * kn[h][t:t + 1, :], axis=-1, keepdims=True) + nbias_ref[h, t] for t in range(n_real)]
             for h in hs]
    m = [functools.reduce(jnp.maximum, s_new[h], jnp.max(s[h], axis=-1, keepdims=True)) for h in hs]
    p = [jnp.exp(s[h] - m[h]) for h in hs]
    p_new = [[jnp.exp(sn - m[h]) for sn in s_new[h]] for h in hs]
    l = [sum(p_new[h], jnp.sum(p[h], axis=-1, keepdims=True)) for h in hs]
    acc = [_dot(p[h], vbuf[slot, head_rows[h], :]) for h in hs]
    outs = []
    for h in hs:
        a = acc[h]
        for t in range(n_real):
            a = a + p_new[h][t] * vn[h][t:t + 1, :]
        o = a / l[h]
        o = o[:n_real] - lam * o[n_real:]
        outs.append(_rms_norm(o, nw_ref[...]) * out_scale)
    o_ref[0:pad, :] = jnp.zeros((pad, BR_W), F32)
    o_ref[pad:, :] = jnp.concatenate(outs, axis=1)


def _diff_sample(page_table, scalars, proj, k_new, v_new, cache_k, cache_v, layer, bias, nbias, norm_w, n_real):
    n_batch, n_pages = page_table.shape
    n_past = n_pages * PAGE_SIZE
    new_tok = pl.BlockSpec((None, SAMPLE_PAD, BR_W), lambda b, pt: (layer, b, 0))
    grid_spec = pltpu.PrefetchScalarGridSpec(
        num_scalar_prefetch=1,
        grid=(n_batch,),
        in_specs=[pl.BlockSpec(memory_space=pltpu.SMEM),
                  pl.BlockSpec((SAMPLE_PAD, BR_W), lambda b, pt: (b, BLK_CQ)),
                  new_tok, new_tok,
                  pl.BlockSpec(memory_space=pl.ANY), pl.BlockSpec(memory_space=pl.ANY),
                  pl.BlockSpec(bias.shape, lambda b, pt: (0, 0, 0)),
                  pl.BlockSpec(nbias.shape, lambda b, pt: (0, 0, 0, 0)),
                  pl.BlockSpec((1, HEAD_W), lambda b, pt: (0, 0))],
        out_specs=pl.BlockSpec((SAMPLE_PAD, BR_W), lambda b, pt: (b, 0)),
        scratch_shapes=[pltpu.VMEM((2, n_past * N_HEADS, HEAD_W), F32), pltpu.VMEM((2, n_past * N_HEADS, HEAD_W), F32),
                        pltpu.SemaphoreType.DMA((2, 2))],
    )
    return pl.pallas_call(
        functools.partial(_diff_sample_body, layer=layer, n_pages=n_pages, n_real=n_real, n_batch=n_batch),
        grid_spec=grid_spec,
        out_shape=jax.ShapeDtypeStruct((n_batch * SAMPLE_PAD, BR_W), F32),
        compiler_params=_cparams(1),
        name="diff_sample",
    )(page_table, scalars, proj, k_new, v_new, cache_k, cache_v, bias, nbias, norm_w)


def _merge_body(oa_ref, ob_ref, oc_ref, g0_ref, g1_ref, g2_ref, x_ref, wb_ref, wo_ref, ln_ref, o_ref):
    m = (_sigmoid(g0_ref[...]) * _dot(oa_ref[...], wb_ref[0])
         + _sigmoid(g1_ref[...]) * _dot(ob_ref[...], wb_ref[1])
         + _sigmoid(g2_ref[...]) * _dot(oc_ref[...], wb_ref[2]))
    y = _dot(m, wo_ref[...])
    o_ref[...] = _layer_norm(ALPHA * x_ref[...] + y, ln_ref[0:1, :], ln_ref[1:2, :])


def _merge(oa, ob, oc, proj, x, w_branch, w_out, ln):
    t = x.shape[0]
    tm = min(ROW_TILE, t)
    br = pl.BlockSpec((tm, BR_W), lambda i: (i, 0))

    def gate(n):
        return pl.BlockSpec((tm, D_MODEL), lambda i: (i, n))

    return pl.pallas_call(
        _merge_body,
        grid=(t // tm,),
        in_specs=[br, br, br, gate(0), gate(1), gate(2),
                  pl.BlockSpec((tm, D_MODEL), lambda i: (i, 0)),
                  pl.BlockSpec(w_branch.shape, lambda i: (0, 0, 0)),
                  pl.BlockSpec(w_out.shape, lambda i: (0, 0)),
                  pl.BlockSpec((SUBLANES, D_MODEL), lambda i: (0, 0))],
        out_specs=pl.BlockSpec((tm, D_MODEL), lambda i: (i, 0)),
        out_shape=jax.ShapeDtypeStruct((t, D_MODEL), F32),
        compiler_params=_cparams(1),
        name="merge",
    )(oa, ob, oc, proj, proj, proj, x, w_branch, w_out, ln)


def _attend_memory(qs, kvs):
    n = range(len(qs))
    s = [_dot_nt(qs[i], kvs[i][0]) * HEAD_W ** -0.5 for i in n]
    p = [jnp.exp(s[i] - jnp.max(s[i], axis=-1, keepdims=True)) for i in n]
    p = [p[i] / jnp.sum(p[i], axis=-1, keepdims=True) for i in n]
    return [_dot(p[i], kvs[i][1]) for i in n]


def _merge_xattn_prompt_body(oa_ref, ob_ref, oc_ref, g0_ref, g1_ref, g2_ref, x_ref, wb_ref, wo_ref, ln0_ref,
                             mk_ref, mv_ref, wq_ref, wxo_ref, ln1_ref, o_ref):
    m = (_sigmoid(g0_ref[...]) * _dot(oa_ref[...], wb_ref[0])
         + _sigmoid(g1_ref[...]) * _dot(ob_ref[...], wb_ref[1])
         + _sigmoid(g2_ref[...]) * _dot(oc_ref[...], wb_ref[2]))
    x = _layer_norm(ALPHA * x_ref[...] + _dot(m, wo_ref[...]), ln0_ref[0:1, :], ln0_ref[1:2, :])
    mk, mv = mk_ref[0], mv_ref[0]
    q = _dot(x, wq_ref[...])
    o = jnp.concatenate(_attend_memory([_head(q, h) for h in range(N_HEADS)],
                                       [(_head(mk, h), _head(mv, h)) for h in range(N_HEADS)]), axis=1)
    o_ref[...] = _layer_norm(ALPHA * x + _dot(o, wxo_ref[...]), ln1_ref[0:1, :], ln1_ref[1:2, :])


def _merge_xattn_prompt(oa, ob, oc, proj, x, w_branch, w_out, ln0, mkv, w_xq, w_xo, ln1, batch, seq):
    tm = min(ROW_TILE, seq)
    per = seq // tm
    n_mem = mkv.shape[1]

    def rows(width, col):
        return pl.BlockSpec((tm, width), lambda b, i: (b * per + i, col))

    def whole(a):
        return pl.BlockSpec(a.shape, lambda b, i: (0,) * a.ndim)

    ln = pl.BlockSpec((SUBLANES, D_MODEL), lambda b, i: (0, 0))
    return pl.pallas_call(
        _merge_xattn_prompt_body,
        grid=(batch, per),
        in_specs=[rows(BR_W, 0), rows(BR_W, 0), rows(BR_W, 0),
                  rows(D_MODEL, 0), rows(D_MODEL, 1), rows(D_MODEL, 2), rows(D_MODEL, 0),
                  whole(w_branch), whole(w_out), ln,
                  pl.BlockSpec((1, n_mem, BR_W), lambda b, i: (b, 0, 0)),
                  pl.BlockSpec((1, n_mem, BR_W), lambda b, i: (b, 0, 1)),
                  whole(w_xq), whole(w_xo), ln],
        out_specs=rows(D_MODEL, 0),
        out_shape=jax.ShapeDtypeStruct(x.shape, F32),
        compiler_params=_cparams(2),
        name="merge_xattn_prompt",
    )(oa, ob, oc, proj, proj, proj, x, w_branch, w_out, ln0, mkv, mkv, w_xq, w_xo, ln1)


def _xattn_sample_body(x_ref, mk_ref, mv_ref, wq_ref, wo_ref, ln_ref, o_ref, att_ref, *, n_seq):
    x = x_ref[...]
    q = _dot(x, wq_ref[...])
    n_mem = mk_ref.shape[1] // N_HEADS
    tiles = [(i, h) for i in range(n_seq) for h in range(N_HEADS)]
    qs, kvs = [], []
    for i, h in tiles:
        rows = pl.ds(h, n_mem, stride=N_HEADS)
        qs.append(_head(q[i * SAMPLE_PAD:(i + 1) * SAMPLE_PAD], h))
        kvs.append((mk_ref[i, rows, :], mv_ref[i, rows, :]))
    for (i, h), o in zip(tiles, _attend_memory(qs, kvs)):
        att_ref[i * SAMPLE_PAD:(i + 1) * SAMPLE_PAD, h * HEAD_W:(h + 1) * HEAD_W] = o
    o_ref[...] = _layer_norm(ALPHA * x + _dot(att_ref[...], wo_ref[...]), ln_ref[0:1, :], ln_ref[1:2, :])


def _xattn_sample(x, mem_k, mem_v, layer, w_xq, w_xo, ln, seq_per_step=8):
    n_batch, mem_rows = mem_k.shape[1:3]
    rows = seq_per_step * SAMPLE_PAD
    mem = pl.BlockSpec((None, seq_per_step, mem_rows, HEAD_W), lambda g: (layer, g, 0, 0))
    return pl.pallas_call(
        functools.partial(_xattn_sample_body, n_seq=seq_per_step),
        grid=(n_batch // seq_per_step,),
        in_specs=[pl.BlockSpec((rows, D_MODEL), lambda g: (g, 0)), mem, mem,
                  pl.BlockSpec(w_xq.shape, lambda g: (0, 0)),
                  pl.BlockSpec(w_xo.shape, lambda g: (0, 0)),
                  pl.BlockSpec((SUBLANES, D_MODEL), lambda g: (0, 0))],
        out_specs=pl.BlockSpec((rows, D_MODEL), lambda g: (g, 0)),
        out_shape=jax.ShapeDtypeStruct(x.shape, F32),
        scratch_shapes=[pltpu.VMEM((rows, BR_W), F32)],
        compiler_params=_cparams(1),
        name="xattn_sample",
    )(x, mem_k, mem_v, w_xq, w_xo, ln)


def _ffn_body(x_ref, wg_ref, wu_ref, w2_ref, ln_ref, o_ref):
    x = x_ref[...]
    xb = x.astype(BF16)
    gate = jnp.dot(xb, wg_ref[...], preferred_element_type=F32)
    up = jnp.dot(xb, wu_ref[...], preferred_element_type=F32)
    y = _dot(_silu(gate) * up, w2_ref[...])
    o_ref[...] = _layer_norm(ALPHA * x + y, ln_ref[0:1, :], ln_ref[1:2, :])


def _ffn(x, w1, w2, ln):
    t = x.shape[0]
    tm = min(ROW_TILE, t)
    resident = pl.Buffered(1)
    return pl.pallas_call(
        _ffn_body,
        grid=(t // tm,),
        in_specs=[pl.BlockSpec((tm, D_MODEL), lambda i: (i, 0)),
                  pl.BlockSpec((D_MODEL, D_FF), lambda i: (0, 0), pipeline_mode=resident),
                  pl.BlockSpec((D_MODEL, D_FF), lambda i: (0, 1), pipeline_mode=resident),
                  pl.BlockSpec((D_FF, D_MODEL), lambda i: (0, 0), pipeline_mode=resident),
                  pl.BlockSpec((SUBLANES, D_MODEL), lambda i: (0, 0))],
        out_specs=pl.BlockSpec((tm, D_MODEL), lambda i: (i, 0)),
        out_shape=jax.ShapeDtypeStruct(x.shape, F32),
        compiler_params=_cparams(1),
        name="ffn",
    )(x, w1, w1, w2, ln)


def _split_w_in(w):
    ab0 = 8 * BR_W
    c0 = ab0 + 2 * N_HEADS
    g0 = c0 + 3 * BR_W
    pad = jnp.zeros((w.shape[0], LANES - 2 * N_HEADS), w.dtype)
    main = jnp.concatenate([w[:, g0:], w[:, :ab0], w[:, c0:c0 + BR_W]], axis=1)
    kv = jnp.concatenate([w[:, c0 + BR_W:g0], w[:, ab0:c0], pad], axis=1)
    return main.astype(BF16), kv.astype(BF16)


def _ln_rows(g, b):
    return jnp.concatenate([g[None], b[None], jnp.zeros((SUBLANES - 2, g.shape[0]), F32)], axis=0)


def kernel(x_prompt, x_sample, mem_prompt, cache_attn_k, cache_attn_v, cache_mem_k, cache_mem_v, state_hgrn,
           state_gdn, state_gdn_conv, page_table, w_in, w_branch, w_out, lower_bounds, hgrn_norm, gdn_a_log,
           gdn_dt_bias, gdn_conv_w, gdn_norm, diff_lambda, diff_norm, rel_bias, w_xq, w_mem_kv, w_xo, ln_g, ln_b,
           w_ffn_in, w_ffn_out):
    bp, seq, _ = x_prompt.shape
    bs, ls, _ = x_sample.shape
    n_mem = mem_prompt.shape[1]
    n_pages = page_table.shape[1]
    n_past = n_pages * PAGE_SIZE
    n_pool = cache_attn_k.shape[1]
    pad = SAMPLE_PAD - ls

    lb_cum = jnp.cumsum(jax.nn.softmax(lower_bounds.astype(F32), axis=0), axis=0)
    lb = (lb_cum - lb_cum[0]).reshape(DEPTH, N_HEADS, 1, HEAD_W)
    lbp = jnp.concatenate([jnp.log(lb), jnp.log1p(-lb), 1.0 - lb,
                           jnp.zeros((DEPTH, N_HEADS, SUBLANES - 3, HEAD_W), F32)], axis=2)

    n_max = max(2 * ATT_BLOCK + 2, n_past + ls)
    bucket = t5_bucket(-jnp.arange(n_max, dtype=jnp.int32))
    thr = jnp.searchsorted(bucket, jnp.arange(N_BUCKETS, dtype=jnp.int32), side="left").astype(jnp.int32)
    rel_bias_t = rel_bias.astype(F32).T
    tiles = _bias_tiles(thr, rel_bias_t)
    dist = rel_bias_t[:, bucket]
    far_bias = dist[:, ATT_BLOCK + 1]
    q_pos = n_past + jnp.arange(ls)
    past = dist[:, q_pos[:, None] - jnp.arange(n_past)[None, :]]
    bias_s = jnp.concatenate([past, past], axis=1)
    tt = jnp.arange(ls)[:, None] - jnp.arange(ls)[None, :]
    new = jnp.where(tt >= 0, dist[:, jnp.maximum(tt, 0)], NEG_INF)
    nbias = jnp.concatenate([new, new], axis=1).transpose(0, 2, 1)[..., None]

    xp = x_prompt.reshape(bp * seq, D_MODEL)
    xs = jnp.concatenate([jnp.zeros((bs, pad, D_MODEL), F32), x_sample], axis=1).reshape(bs * SAMPLE_PAD, D_MODEL)
    mem = mem_prompt.reshape(bp * n_mem, D_MODEL)
    cache_k = cache_attn_k.reshape(DEPTH, n_pool, PAGE_SIZE * N_HEADS, HEAD_W)
    cache_v = cache_attn_v.reshape(DEPTH, n_pool, PAGE_SIZE * N_HEADS, HEAD_W)
    mem_k = cache_mem_k.reshape(DEPTH, bs, n_mem * N_HEADS, HEAD_W)
    mem_v = cache_mem_v.reshape(DEPTH, bs, n_mem * N_HEADS, HEAD_W)

    pk = pv = jnp.zeros((DEPTH, bp * seq * N_HEADS, HEAD_W), F32)
    sk = sv = jnp.zeros((DEPTH, bs * SAMPLE_PAD, BR_W), F32)
    sh = sg = jnp.zeros(state_hgrn.shape, F32)
    pmk, pmv, ph, pg, pc, sc = [], [], [], [], [], []
    b_col0 = BLK_B * BR_W

    for l in range(DEPTH):
        w_main, w_kv = _split_w_in(w_in[l])
        wb_l = w_branch[l].astype(BF16)
        wo_l = w_out[l].astype(BF16)
        wq_l = w_xq[l].astype(BF16)
        wxo_l = w_xo[l].astype(BF16)
        wkv_l = w_mem_kv[l].astype(BF16)
        w1_l = w_ffn_in[l].astype(BF16)
        w2_l = w_ffn_out[l].astype(BF16)
        lns = [_ln_rows(ln_g[l, i], ln_b[l, i]) for i in range(3)]
        hn = hgrn_norm[l][None]
        gn = gdn_norm[l][None]
        dn = diff_norm[l][None]
        head_params = jnp.stack([jnp.exp(gdn_a_log[l].astype(F32)), gdn_dt_bias[l].astype(F32)])
        lam_init = 0.8 - 0.6 * math.exp(-0.3 * l)
        lq1, lk1, lq2, lk2 = diff_lambda[l].astype(F32)
        lam = jnp.exp(jnp.sum(lq1 * lk1)) - jnp.exp(jnp.sum(lq2 * lk2)) + lam_init
        scalars = jnp.stack([jnp.stack([lam, jnp.float32(1.0 - lam_init), jnp.float32(0), jnp.float32(0)]),
                             far_bias])
        conv_w_l = gdn_conv_w[l]

        proj = _matmul(xp, w_main, 2 * ROW_TILE, MAIN_W // 3)
        pk, pv, ab, k16, v16 = _kv_proj(xp, w_kv, l, pk, pv, True)
        oa, h_new = _hgrn_prompt(proj, lbp[l], hn, bp, seq)
        qkv = _gdn_conv(proj, BLK_B, conv_w_l, SEQ_TILE, seq)
        ob, g_new = _gdn_prompt(qkv, proj, ab, head_params, gn, bp, seq)
        oc = _diff_prompt(proj, k16, v16, scalars, tiles, dn, bp, seq)
        mkv = _matmul(mem, wkv_l, ROW_TILE, 2 * BR_W).reshape(bp, n_mem, 2 * BR_W)
        x2 = _merge_xattn_prompt(oa, ob, oc, proj, xp, wb_l, wo_l, lns[0], mkv, wq_l, wxo_l, lns[1], bp, seq)
        xp = _ffn(x2, w1_l, w2_l, lns[2])
        pmk.append(mkv[:, :, :BR_W].reshape(bp, n_mem, N_HEADS, HEAD_W))
        pmv.append(mkv[:, :, BR_W:].reshape(bp, n_mem, N_HEADS, HEAD_W))
        ph.append(h_new)
        pg.append(g_new)
        pc.append(proj.reshape(bp, seq, MAIN_W)[:, seq - (CONV_W - 1):, b_col0:b_col0 + CONV_CH])

        proj_s = _matmul(xs, w_main, ROW_TILE, MAIN_W // 3)
        sk, sv, ab_s, _, _ = _kv_proj(xs, w_kv, l, sk, sv, False)
        oa, sh = _hgrn_sample(proj_s, lbp[l], hn, state_hgrn, l, sh, ls)
        conv_tok = proj_s[:, b_col0:b_col0 + CONV_CH].reshape(bs, SAMPLE_PAD, CONV_CH)
        conv_in = jnp.concatenate([conv_tok[:, :pad - (CONV_W - 1)], state_gdn_conv[l], conv_tok[:, pad:]], axis=1)
        qkv = _gdn_conv(conv_in.reshape(bs * SAMPLE_PAD, CONV_CH), 0, conv_w_l, bs * SAMPLE_PAD, SAMPLE_PAD)
        ob, sg = _gdn_sample(qkv, proj_s, ab_s, head_params, gn, state_gdn, l, sg, ls)
        oc = _diff_sample(page_table, scalars, proj_s, sk, sv, cache_k, cache_v, l, bias_s, nbias, dn, ls)
        x1 = _merge(oa, ob, oc, proj_s, xs, wb_l, wo_l, lns[0])
        x2 = _xattn_sample(x1, mem_k, mem_v, l, wq_l, wxo_l, lns[1])
        xs = _ffn(x2, w1_l, w2_l, lns[2])
        sc.append(conv_in[:, SAMPLE_PAD - (CONV_W - 1):])

    y_prompt = xp.reshape(bp, seq, D_MODEL)
    y_sample = xs.reshape(bs, SAMPLE_PAD, D_MODEL)[:, pad:]
    new_k_p = pk.reshape(DEPTH, bp, seq, N_HEADS, HEAD_W)
    new_v_p = pv.reshape(DEPTH, bp, seq, N_HEADS, HEAD_W)
    new_k_s = sk.reshape(DEPTH, bs, SAMPLE_PAD, N_HEADS, HEAD_W)[:, :, pad:]
    new_v_s = sv.reshape(DEPTH, bs, SAMPLE_PAD, N_HEADS, HEAD_W)[:, :, pad:]
    return (y_prompt, y_sample, new_k_p, new_v_p, jnp.stack(pmk), jnp.stack(pmv), jnp.stack(ph), jnp.stack(pg),
            jnp.stack(pc), new_k_s, new_v_s, sh, sg, jnp.stack(sc))
```

```python
import functools
import math

import jax
import jax.numpy as jnp
from jax import lax
from jax.experimental import pallas as pl
from jax.experimental.pallas import tpu as pltpu

F32 = jnp.float32
BF16 = jnp.bfloat16

D_MODEL = 1024
DEPTH = 4
PAGE_SIZE = 128
BR_W = D_MODEL // 2
N_HEADS = 4
HEAD_W = BR_W // N_HEADS
DH_C = HEAD_W // 2
CONV_W = 4
CONV_CH = 3 * BR_W
N_BRANCH = 3
N_BUCKETS = 32
MAX_DISTANCE = 128
D_FF = -(-8 * D_MODEL // (3 * 256)) * 256
LN_EPS = 1e-5
NORM_EPS = 1e-6
NEG_INF = -1e30
ALPHA = (2 * DEPTH) ** 0.25

LANES = 128
SUBLANES = 8
VMEM_LIMIT_BYTES = 56 * 1024 * 1024

BLK_A = N_BRANCH * D_MODEL // BR_W
BLK_B = BLK_A + 4
BLK_CQ = BLK_B + 4
MAIN_W = (BLK_CQ + 1) * BR_W
KV_W = 2 * BR_W + LANES

HGRN_SUB = 16
HGRN_ROWS = 64
GDN_CHUNK = 64
SOLVE_BLOCK = 16
SAMPLE_PAD = 8
SEQ_TILE = 512
GDN_TILE = 1024
ATT_BLOCK = 512
ATT_HEADS = 4
ROW_TILE = 512


def _cparams(n_axes):
    return pltpu.CompilerParams(dimension_semantics=("arbitrary",) * n_axes,
                                vmem_limit_bytes=VMEM_LIMIT_BYTES)


def _dot(a, b):
    return jnp.dot(a.astype(BF16), b.astype(BF16), preferred_element_type=F32)


def _dot_nt(a, b):
    return lax.dot_general(a.astype(BF16), b.astype(BF16), (((1,), (1,)), ((), ())),
                           preferred_element_type=F32)


def _dot_tn(a, b):
    return lax.dot_general(a.astype(BF16), b.astype(BF16), (((0,), (0,)), ((), ())),
                           preferred_element_type=F32)


def _dot_exact_lhs(m01, x):
    hi = x.astype(BF16)
    r1 = x - hi.astype(F32)
    mid = r1.astype(BF16)
    lo = (r1 - mid.astype(F32)).astype(BF16)
    dot = functools.partial(jnp.dot, preferred_element_type=F32)
    return dot(m01, hi) + dot(m01, mid) + dot(m01, lo)


def _sigmoid(x):
    return 1.0 / (1.0 + jnp.exp(-x))


def _silu(x):
    return x * _sigmoid(x)


def _softplus(x):
    return jnp.maximum(x, 0.0) + jnp.log1p(jnp.exp(-jnp.abs(x)))


def _layer_norm(h, g, b):
    mu = jnp.mean(h, axis=-1, keepdims=True)
    c = h - mu
    var = jnp.mean(c * c, axis=-1, keepdims=True)
    return c * lax.rsqrt(var + LN_EPS) * g + b


def _rms_norm(x, g):
    return x * lax.rsqrt(jnp.mean(x * x, axis=-1, keepdims=True) + NORM_EPS) * g


def _bcast_rows(x, rows, reps):
    return jnp.concatenate([jnp.broadcast_to(x[r:r + 1, :], (reps, x.shape[1])) for r in rows], axis=0)


def _head(x, h):
    return x[:, h * HEAD_W:(h + 1) * HEAD_W]


ALIASED = pl.BlockSpec(memory_space=pl.ANY)


def _mm_body(x_ref, w_ref, o_ref):
    o_ref[...] = _dot(x_ref[...], w_ref[...])


def _matmul(x, w, tm, tn):
    t, k = x.shape
    n = w.shape[1]
    tm = min(tm, t)
    return pl.pallas_call(
        _mm_body,
        grid=(n // tn, t // tm),
        in_specs=[pl.BlockSpec((tm, k), lambda j, i: (i, 0)),
                  pl.BlockSpec((k, tn), lambda j, i: (0, j))],
        out_specs=pl.BlockSpec((tm, tn), lambda j, i: (i, j)),
        out_shape=jax.ShapeDtypeStruct((t, n), F32),
        compiler_params=_cparams(2),
        name="proj_matmul",
    )(x, w)


def _kv_body(x_ref, w_ref, kp_ref, vp_ref, k_ref, v_ref, ab_ref, k16_ref, v16_ref, *, head_rows):
    y = _dot(x_ref[...], w_ref[...])
    k = y[:, :BR_W]
    v = y[:, BR_W:2 * BR_W]
    if head_rows:
        for h in range(N_HEADS):
            rows = pl.ds(h, k.shape[0], stride=N_HEADS)
            k_ref[rows, :] = _head(k, h)
            v_ref[rows, :] = _head(v, h)
    else:
        k_ref[...] = k
        v_ref[...] = v
    ab_ref[...] = y[:, 2 * BR_W:]
    k16_ref[...] = k.astype(BF16)
    ones = jnp.ones((v.shape[0], HEAD_W), BF16)
    v16_ref[...] = jnp.concatenate([piece for h in range(N_HEADS) for piece in (_head(v, h).astype(BF16), ones)],
                                   axis=1)


def _kv_proj(x, w, layer, k_prev, v_prev, head_rows):
    t = x.shape[0]
    tm = min(ROW_TILE, t)
    slab_shape = (tm * N_HEADS, HEAD_W) if head_rows else (tm, BR_W)
    slab = pl.BlockSpec((None,) + slab_shape, lambda i: (layer, i, 0))
    stacked = jax.ShapeDtypeStruct(k_prev.shape, F32)
    return pl.pallas_call(
        functools.partial(_kv_body, head_rows=head_rows),
        grid=(t // tm,),
        in_specs=[pl.BlockSpec((tm, D_MODEL), lambda i: (i, 0)),
                  pl.BlockSpec(w.shape, lambda i: (0, 0)), ALIASED, ALIASED],
        out_specs=[slab, slab, pl.BlockSpec((tm, LANES), lambda i: (i, 0)),
                   pl.BlockSpec((tm, BR_W), lambda i: (i, 0)), pl.BlockSpec((tm, 2 * BR_W), lambda i: (i, 0))],
        out_shape=[stacked, stacked, jax.ShapeDtypeStruct((t, LANES), F32),
                   jax.ShapeDtypeStruct((t, BR_W), BF16), jax.ShapeDtypeStruct((t, 2 * BR_W), BF16)],
        input_output_aliases={2: 0, 3: 1},
        compiler_params=_cparams(1),
        name="kv_proj",
    )(x, w, k_prev, v_prev)


def _hgrn_rows(q, z, v, lbp, sub, masks, valid):
    r = q.shape[0]
    n_sub = r // sub
    tri, pos = masks
    log_lb, log1m_lb, om_lb = lbp[0:1, :], lbp[1:2, :], lbp[2:3, :]
    log_sig = jnp.minimum(z, 0.0) - jnp.log(1.0 + jnp.exp(-jnp.abs(z)))
    t2 = log1m_lb + log_sig
    logf = jnp.maximum(log_lb, t2) + jnp.log(1.0 + jnp.exp(-jnp.abs(log_lb - t2)))
    k = om_lb * (1.0 / (1.0 + jnp.exp(z)))
    if valid is not None:
        logf = jnp.where(valid, logf, 0.0)
        k = jnp.where(valid, k, 0.0)
    bc = _dot_exact_lhs(tri, logf)
    last_rows = [i * sub + sub - 1 for i in range(n_sub)]
    b_last = _bcast_rows(bc, last_rows, sub)
    q_hat = q * jnp.exp(bc)
    k_hat = k * jnp.exp(b_last - bc)
    o = jnp.zeros_like(q)
    for s in range(sub):
        rows = [i * sub + s for i in range(n_sub)]
        ks = _bcast_rows(k, rows, sub)
        bs = _bcast_rows(bc, rows, sub)
        vs = _bcast_rows(v, rows, sub)
        a = jnp.where(pos >= s, q * ks * jnp.exp(bc - bs), 0.0)
        o = o + jnp.sum(a, axis=-1, keepdims=True) * vs
    return q_hat, k_hat, o, bc


def _sub_chunk_masks(rows, sub):
    ri = lax.broadcasted_iota(jnp.int32, (rows, rows), 0)
    ci = lax.broadcasted_iota(jnp.int32, (rows, rows), 1)
    tri = ((ri // sub == ci // sub) & (ci <= ri)).astype(BF16)
    pos = lax.broadcasted_iota(jnp.int32, (rows, 1), 0) % sub
    return tri, pos


def _hgrn_finish(o, g, norm_w):
    return _rms_norm(o, norm_w) * _silu(g)


def _hgrn_prompt_body(q_ref, f_ref, i_ref, g_ref, lbp_ref, nw_ref, o_ref, s_out_ref, st_ref, *, n_tiles):
    t = pl.program_id(1)

    @pl.when(t == 0)
    def _():
        st_ref[...] = jnp.zeros_like(st_ref)

    rows = HGRN_ROWS
    nw = nw_ref[...]
    masks = _sub_chunk_masks(rows, HGRN_SUB)

    def chunk(c, carry):
        sl = pl.ds(pl.multiple_of(c * rows, rows), rows)
        qa, za, va, ga = q_ref[sl, :], f_ref[sl, :], i_ref[sl, :], g_ref[sl, :]
        hs = range(N_HEADS)
        v = [_head(va, h) for h in hs]
        parts = [_hgrn_rows(_head(qa, h), _head(za, h), v[h], lbp_ref[h], HGRN_SUB, masks, None) for h in hs]
        st = [st_ref[h] for h in hs]
        outs = [[] for _ in hs]
        for i in range(rows // HGRN_SUB):
            lo, hi = i * HGRN_SUB, (i + 1) * HGRN_SUB
            for h in hs:
                q_hat, k_hat, o, bc = parts[h]
                outs[h].append(o[lo:hi] + _dot_nt(q_hat[lo:hi], st[h]))
                st[h] = st[h] * jnp.exp(bc[hi - 1:hi, :]) + _dot_tn(v[h][lo:hi], k_hat[lo:hi])
        for h in hs:
            st_ref[h] = st[h]
        o_ref[sl, :] = jnp.concatenate(
            [_hgrn_finish(jnp.concatenate(outs[h], axis=0), _head(ga, h), nw) for h in hs], axis=1)
        return carry

    lax.fori_loop(0, SEQ_TILE // rows, chunk, 0)

    @pl.when(t == n_tiles - 1)
    def _():
        for h in range(N_HEADS):
            s_out_ref[0, h] = st_ref[h].T


def _hgrn_prompt(proj, lbp, norm_w, batch, seq):
    n_tiles = seq // SEQ_TILE

    def col(blk):
        return pl.BlockSpec((SEQ_TILE, BR_W), lambda b, t: (b * n_tiles + t, blk))

    return pl.pallas_call(
        functools.partial(_hgrn_prompt_body, n_tiles=n_tiles),
        grid=(batch, n_tiles),
        in_specs=[col(BLK_A), col(BLK_A + 1), col(BLK_A + 2), col(BLK_A + 3),
                  pl.BlockSpec((N_HEADS, SUBLANES, HEAD_W), lambda b, t: (0, 0, 0)),
                  pl.BlockSpec((1, HEAD_W), lambda b, t: (0, 0))],
        out_specs=[pl.BlockSpec((SEQ_TILE, BR_W), lambda b, t: (b * n_tiles + t, 0)),
                   pl.BlockSpec((1, N_HEADS, HEAD_W, HEAD_W), lambda b, t: (b, 0, 0, 0))],
        out_shape=[jax.ShapeDtypeStruct((batch * seq, BR_W), F32),
                   jax.ShapeDtypeStruct((batch, N_HEADS, HEAD_W, HEAD_W), F32)],
        scratch_shapes=[pltpu.VMEM((N_HEADS, HEAD_W, HEAD_W), F32)],
        compiler_params=_cparams(2),
        name="hgrn_prompt",
    )(proj, proj, proj, proj, lbp, norm_w)


def _hgrn_sample_body(q_ref, f_ref, i_ref, g_ref, lbp_ref, nw_ref, s0_ref, *refs, n_seq, n_real):
    o_ref, s_out_ref = refs[-2:]
    nw = nw_ref[...]
    rows = n_seq * SAMPLE_PAD
    masks = _sub_chunk_masks(rows, SAMPLE_PAD)
    valid = masks[1] >= SAMPLE_PAD - n_real
    qa, za, va, ga = q_ref[...], f_ref[...], i_ref[...], g_ref[...]
    hs = range(N_HEADS)
    v = [_head(va, h) for h in hs]
    parts = [_hgrn_rows(_head(qa, h), _head(za, h), v[h], lbp_ref[h], SAMPLE_PAD, masks, valid) for h in hs]
    eye = (lax.broadcasted_iota(jnp.int32, (HEAD_W, HEAD_W), 0)
           == lax.broadcasted_iota(jnp.int32, (HEAD_W, HEAD_W), 1))
    outs = [[] for _ in hs]
    for i in range(n_seq):
        lo, hi = i * SAMPLE_PAD, (i + 1) * SAMPLE_PAD
        for h in hs:
            q_hat, k_hat, o, bc = parts[h]
            s0 = s0_ref[i, h]
            outs[h].append(o[lo:hi] + _dot(q_hat[lo:hi], s0))
            decay = jnp.exp(bc[hi - 1:hi, :])
            decay_col = jnp.sum(jnp.where(eye, decay, 0.0), axis=-1, keepdims=True)
            s_out_ref[i, h] = s0 * decay_col + _dot_tn(k_hat[lo:hi], v[h][lo:hi])
    o_ref[...] = jnp.concatenate(
        [_hgrn_finish(jnp.concatenate(outs[h], axis=0), _head(ga, h), nw) for h in hs], axis=1)


def _state_specs(layer, seq_per_step):
    return pl.BlockSpec((None, seq_per_step, N_HEADS, HEAD_W, HEAD_W), lambda g: (layer, g, 0, 0, 0))


def _hgrn_sample(proj, lbp, norm_w, states, layer, prev, n_real, seq_per_step=8):
    n_batch = states.shape[1]
    rows = seq_per_step * SAMPLE_PAD

    def col(blk):
        return pl.BlockSpec((rows, BR_W), lambda g: (g, blk))

    return pl.pallas_call(
        functools.partial(_hgrn_sample_body, n_seq=seq_per_step, n_real=n_real),
        grid=(n_batch // seq_per_step,),
        in_specs=[col(BLK_A), col(BLK_A + 1), col(BLK_A + 2), col(BLK_A + 3),
                  pl.BlockSpec((N_HEADS, SUBLANES, HEAD_W), lambda g: (0, 0, 0)),
                  pl.BlockSpec((1, HEAD_W), lambda g: (0, 0)),
                  _state_specs(layer, seq_per_step), ALIASED],
        out_specs=[pl.BlockSpec((rows, BR_W), lambda g: (g, 0)), _state_specs(layer, seq_per_step)],
        out_shape=[jax.ShapeDtypeStruct((n_batch * SAMPLE_PAD, BR_W), F32),
                   jax.ShapeDtypeStruct(states.shape, F32)],
        input_output_aliases={7: 1},
        compiler_params=_cparams(1),
        name="hgrn_sample",
    )(proj, proj, proj, proj, lbp, norm_w, states, prev)


def _conv_body(prev_ref, cur_ref, w_ref, o_ref, buf_ref, *, tile, seq_len):
    i = pl.program_id(0)
    c = pl.program_id(1)
    buf_ref[0:SUBLANES, :] = prev_ref[...]
    buf_ref[SUBLANES:, :] = cur_ref[...]
    pos = (i * tile + lax.broadcasted_iota(jnp.int32, (tile, 1), 0)) % seq_len
    w = w_ref[...]
    acc = cur_ref[...] * w[CONV_W - 1:CONV_W, :]
    for back in range(1, CONV_W):
        shifted = buf_ref[pl.ds(SUBLANES - back, tile), :]
        acc = acc + jnp.where(pos >= back, shifted, 0.0) * w[CONV_W - 1 - back:CONV_W - back, :]
    act = _silu(acc)
    q_scale = jnp.where(c == 0, HEAD_W ** -0.5, 1.0)
    heads = []
    for h in range(N_HEADS):
        a = _head(act, h)
        inv = lax.rsqrt(jnp.sum(a * a, axis=-1, keepdims=True) + NORM_EPS)
        heads.append(a * jnp.where(c < 2, inv * q_scale, 1.0))
    o_ref[...] = jnp.concatenate(heads, axis=1)


def _gdn_conv(conv_in, col0, conv_w, tile, seq_len):
    t = conv_in.shape[0]
    per = tile // SUBLANES
    return pl.pallas_call(
        functools.partial(_conv_body, tile=tile, seq_len=seq_len),
        grid=(t // tile, CONV_CH // BR_W),
        in_specs=[pl.BlockSpec((SUBLANES, BR_W), lambda i, c: (jnp.maximum(i * per - 1, 0), col0 + c)),
                  pl.BlockSpec((tile, BR_W), lambda i, c: (i, col0 + c)),
                  pl.BlockSpec((CONV_W, BR_W), lambda i, c: (0, c))],
        out_specs=pl.BlockSpec((tile, BR_W), lambda i, c: (i, c)),
        out_shape=jax.ShapeDtypeStruct((t, CONV_CH), F32),
        scratch_shapes=[pltpu.VMEM((tile + SUBLANES, BR_W), F32)],
        compiler_params=_cparams(2),
        name="gdn_conv",
    )(conv_in, conv_in, conv_w)


def _dot_f32(a, b):
    ah = a.astype(BF16)
    al = (a - ah.astype(F32)).astype(BF16)
    bh = b.astype(BF16)
    bl = (b - bh.astype(F32)).astype(BF16)
    dot = functools.partial(jnp.dot, preferred_element_type=F32)
    return dot(ah, bh) + dot(ah, bl) + dot(al, bh)


def _forward_substitute(lmats, rhss):
    n_sys = len(lmats)
    c = lmats[0].shape[0]
    nb = min(c, SOLVE_BLOCK)
    done = [[] for _ in range(n_sys)]
    for i0 in range(0, c, nb):
        xbs, lbs = [], []
        for i in range(n_sys):
            r = rhss[i][i0:i0 + nb, :]
            if i0:
                r = r - _dot_f32(lmats[i][i0:i0 + nb, :i0], jnp.concatenate(done[i], axis=0))
            xbs.append([r[s:s + SUBLANES, :] for s in range(0, nb, SUBLANES)])
            lbs.append([lmats[i][i0 + s:i0 + s + SUBLANES, :] for s in range(0, nb, SUBLANES)])
        for j in range(nb - 1):
            for i in range(n_sys):
                xb = xbs[i]
                xj = xb[j // SUBLANES][j % SUBLANES:j % SUBLANES + 1, :]
                for b in range((j + 1) // SUBLANES, len(xb)):
                    xb[b] = xb[b] - lbs[i][b][:, i0 + j:i0 + j + 1] * xj
        for i in range(n_sys):
            done[i].extend(xbs[i])
    return [jnp.concatenate(d, axis=0) if len(d) > 1 else d[0] for d in done]


def _gdn_masks(c):
    ri = lax.broadcasted_iota(jnp.int32, (c, c), 0)
    ci = lax.broadcasted_iota(jnp.int32, (c, c), 1)
    mr = lax.broadcasted_iota(jnp.int32, (c, LANES), 0)
    mc = lax.broadcasted_iota(jnp.int32, (c, LANES), 1)
    return ci <= ri, ci < ri, ((mc < mr) & (mc < c)) | (mc == c)


def _gdn_prepare(hp_ref, chunks, masks, valid):
    c = chunks[0][0].shape[0]
    incl, strict, sel = masks
    pairs = [(ci, h) for ci in range(len(chunks)) for h in range(N_HEADS)]
    ns = range(len(pairs))
    lane = lax.broadcasted_iota(jnp.int32, chunks[0][3].shape, 1)
    q = [_head(chunks[ci][0], h) for ci, h in pairs]
    k = [_head(chunks[ci][1], h) for ci, h in pairs]
    v = [_head(chunks[ci][2], h) for ci, h in pairs]
    g, beta = [], []
    for ci, h in pairs:
        ab = chunks[ci][3]
        a_col = jnp.sum(jnp.where(lane == h, ab, 0.0), axis=-1, keepdims=True)
        b_col = jnp.sum(jnp.where(lane == h + N_HEADS, ab, 0.0), axis=-1, keepdims=True)
        g_h = -hp_ref[0, h] * _softplus(a_col + hp_ref[1, h])
        beta_h = _sigmoid(b_col)
        if valid is not None:
            g_h = jnp.where(valid, g_h, 0.0)
            beta_h = jnp.where(valid, beta_h, 0.0)
        g.append(g_h)
        beta.append(beta_h)
    incl_b = incl.astype(BF16)
    m = [_dot_exact_lhs(incl_b, jnp.where(sel, g[n], 0.0)) for n in ns]
    gcum = [m[n][:, c:c + 1] for n in ns]
    decay = [jnp.where(incl, jnp.exp(jnp.where(incl, m[n][:, :c], 0.0)), 0.0) for n in ns]
    kb = [k[n] * beta[n] for n in ns]
    lmat = [jnp.where(strict, _dot_nt(kb[n], k[n]) * decay[n], 0.0) for n in ns]
    e_g = [jnp.exp(gcum[n]) for n in ns]
    sol = _forward_substitute(lmat, [jnp.concatenate([v[n] * beta[n], kb[n] * e_g[n]], axis=1) for n in ns])
    attn = [jnp.where(incl, _dot_nt(q[n], k[n]) * decay[n], 0.0) for n in ns]
    out = []
    for n in ns:
        g_last = gcum[n][c - 1:c, :]
        out.append((sol[n][:, :HEAD_W], sol[n][:, HEAD_W:], q[n] * e_g[n], k[n] * jnp.exp(g_last - gcum[n]),
                    attn[n], jnp.exp(g_last)))
    return out


def _gdn_apply(prep, gas, nw, get_state, put_state):
    pairs = [(ci, h) for ci in range(len(gas)) for h in range(N_HEADS)]
    ns = range(len(pairs))
    s = [get_state(ci, h) for ci, h in pairs]
    v_new = [prep[n][0] - _dot(prep[n][1], s[n]) for n in ns]
    o = [_dot(prep[n][2], s[n]) + _dot(prep[n][4], v_new[n]) for n in ns]
    for n, (ci, h) in enumerate(pairs):
        put_state(ci, h, s[n] * prep[n][5] + _dot_tn(prep[n][3], v_new[n]))
    return [jnp.concatenate([_rms_norm(o[ci * N_HEADS + h], nw) * _silu(_head(gas[ci], h))
                             for h in range(N_HEADS)], axis=1) for ci in range(len(gas))]


def _gdn_prompt_body(hp_ref, q_ref, k_ref, v_ref, g_ref, ab_ref, nw_ref, o_ref, s_out_ref, s_ref, pre_ref, last_ref,
                     *, n_tiles):
    t = pl.program_id(1)

    @pl.when(t == 0)
    def _():
        s_ref[...] = jnp.zeros_like(s_ref)

    nw = nw_ref[...]
    masks = _gdn_masks(GDN_CHUNK)
    n_chunks = q_ref.shape[0] // GDN_CHUNK

    def put(ci, h, s_new):
        s_ref[h] = s_new

    def rows_of(c):
        return pl.ds(pl.multiple_of(c * GDN_CHUNK, GDN_CHUNK), GDN_CHUNK)

    def prepare_into(c, slot):
        sl = rows_of(c)
        prep = _gdn_prepare(hp_ref, [(q_ref[sl, :], k_ref[sl, :], v_ref[sl, :], ab_ref[sl, :])], masks, None)
        for h, (u, w, qe, ke, attn, e_last) in enumerate(prep):
            for i, x in enumerate((u, w, qe, ke)):
                pre_ref[slot, h, i] = x
            pre_ref[slot, h, 4, :, :GDN_CHUNK] = attn
            last_ref[slot, h] = jnp.broadcast_to(e_last, (SUBLANES, LANES))

    prepare_into(0, 0)

    def chunk(c, carry):
        slot = c % 2
        prep = [(pre_ref[slot, h, 0], pre_ref[slot, h, 1], pre_ref[slot, h, 2], pre_ref[slot, h, 3],
                 pre_ref[slot, h, 4, :, :GDN_CHUNK], last_ref[slot, h, 0:1, 0:1]) for h in range(N_HEADS)]
        sl = rows_of(c)
        o_ref[sl, :] = _gdn_apply(prep, [g_ref[sl, :]], nw, lambda ci, h: s_ref[h], put)[0]
        prepare_into(jnp.minimum(c + 1, n_chunks - 1), 1 - slot)
        return carry

    lax.fori_loop(0, n_chunks, chunk, 0)

    @pl.when(t == n_tiles - 1)
    def _():
        s_out_ref[0] = s_ref[...]


def _gdn_prompt(qkv, proj, ab, head_params, norm_w, batch, seq):
    tile = min(GDN_TILE, seq)
    n_tiles = seq // tile

    def col(blk):
        return pl.BlockSpec((tile, BR_W), lambda b, t: (b * n_tiles + t, blk))

    return pl.pallas_call(
        functools.partial(_gdn_prompt_body, n_tiles=n_tiles),
        grid=(batch, n_tiles),
        in_specs=[pl.BlockSpec(memory_space=pltpu.SMEM),
                  col(0), col(1), col(2), col(BLK_B + 3),
                  pl.BlockSpec((tile, LANES), lambda b, t: (b * n_tiles + t, 0)),
                  pl.BlockSpec((1, HEAD_W), lambda b, t: (0, 0))],
        out_specs=[pl.BlockSpec((tile, BR_W), lambda b, t: (b * n_tiles + t, 0)),
                   pl.BlockSpec((1, N_HEADS, HEAD_W, HEAD_W), lambda b, t: (b, 0, 0, 0))],
        out_shape=[jax.ShapeDtypeStruct((batch * seq, BR_W), F32),
                   jax.ShapeDtypeStruct((batch, N_HEADS, HEAD_W, HEAD_W), F32)],
        scratch_shapes=[pltpu.VMEM((N_HEADS, HEAD_W, HEAD_W), F32),
                        pltpu.VMEM((2, N_HEADS, 5, GDN_CHUNK, HEAD_W), F32),
                        pltpu.VMEM((2, N_HEADS, SUBLANES, LANES), F32)],
        compiler_params=_cparams(2),
        name="gdn_prompt",
    )(head_params, qkv, qkv, qkv, proj, ab, norm_w)


def _gdn_sample_body(hp_ref, q_ref, k_ref, v_ref, g_ref, ab_ref, nw_ref, s0_ref, *refs, n_seq, n_real):
    o_ref, s_out_ref = refs[-2:]
    nw = nw_ref[...]
    masks = _gdn_masks(SAMPLE_PAD)
    valid = lax.broadcasted_iota(jnp.int32, (SAMPLE_PAD, 1), 0) >= SAMPLE_PAD - n_real

    group = 4

    def one(j, carry):
        sls = [pl.ds(pl.multiple_of((j * group + ci) * SAMPLE_PAD, SAMPLE_PAD), SAMPLE_PAD) for ci in range(group)]

        def put(ci, h, s_new):
            s_out_ref[j * group + ci, h] = s_new

        prep = _gdn_prepare(hp_ref, [(q_ref[sl, :], k_ref[sl, :], v_ref[sl, :], ab_ref[sl, :]) for sl in sls],
                            masks, valid)
        outs = _gdn_apply(prep, [g_ref[sl, :] for sl in sls], nw, lambda ci, h: s0_ref[j * group + ci, h], put)
        for sl, o in zip(sls, outs):
            o_ref[sl, :] = o
        return carry

    lax.fori_loop(0, n_seq // group, one, 0)


def _gdn_sample(qkv, proj, ab, head_params, norm_w, states, layer, prev, n_real, seq_per_step=8):
    n_batch = states.shape[1]
    rows = seq_per_step * SAMPLE_PAD

    def col(blk):
        return pl.BlockSpec((rows, BR_W), lambda g: (g, blk))

    return pl.pallas_call(
        functools.partial(_gdn_sample_body, n_seq=seq_per_step, n_real=n_real),
        grid=(n_batch // seq_per_step,),
        in_specs=[pl.BlockSpec(memory_space=pltpu.SMEM),
                  col(0), col(1), col(2), col(BLK_B + 3),
                  pl.BlockSpec((rows, LANES), lambda g: (g, 0)),
                  pl.BlockSpec((1, HEAD_W), lambda g: (0, 0)),
                  _state_specs(layer, seq_per_step), ALIASED],
        out_specs=[pl.BlockSpec((rows, BR_W), lambda g: (g, 0)), _state_specs(layer, seq_per_step)],
        out_shape=[jax.ShapeDtypeStruct((n_batch * SAMPLE_PAD, BR_W), F32),
                   jax.ShapeDtypeStruct(states.shape, F32)],
        input_output_aliases={8: 1},
        compiler_params=_cparams(1),
        name="gdn_sample",
    )(head_params, qkv, qkv, qkv, proj, ab, norm_w, states, prev)


def t5_bucket(rel):
    n = jnp.maximum(-rel, 0)
    exact = N_BUCKETS // 2
    log_part = jnp.log(jnp.maximum(n, 1).astype(F32) / exact) / math.log(MAX_DISTANCE / exact)
    large = jnp.minimum(exact + (log_part * (N_BUCKETS - exact)).astype(jnp.int32), N_BUCKETS - 1)
    return jnp.where(n < exact, n, large)


def _bias_tile_body(thr_ref, rb_ref, o_ref):
    h = pl.program_id(0)
    blk = ATT_BLOCK
    row = lax.broadcasted_iota(jnp.int32, (blk, blk), 0)
    col = lax.broadcasted_iota(jnp.int32, (blk, blk), 1)
    for which in range(2):
        n = row - col + which * blk
        val = jnp.full((blk, blk), rb_ref[h, 0], F32)
        for b in range(1, N_BUCKETS):
            val = jnp.where(n >= thr_ref[b], rb_ref[h, b], val)
        if which == 0:
            val = jnp.where(n >= 0, val, NEG_INF)
        o_ref[which] = val


def _bias_tiles(thr, rel_bias_t):
    return pl.pallas_call(
        _bias_tile_body,
        grid=(N_HEADS,),
        in_specs=[pl.BlockSpec(memory_space=pltpu.SMEM), pl.BlockSpec(memory_space=pltpu.SMEM)],
        out_specs=pl.BlockSpec((None, 2, ATT_BLOCK, ATT_BLOCK), lambda h: (h, 0, 0, 0)),
        out_shape=jax.ShapeDtypeStruct((N_HEADS, 2, ATT_BLOCK, ATT_BLOCK), F32),
        compiler_params=_cparams(1),
        name="bias_tiles",
    )(thr, rel_bias_t)


def _stack_maps(q):
    lane = lax.broadcasted_iota(jnp.int32, q.shape, 1)
    return jnp.concatenate([jnp.where(lane < DH_C, q, 0.0), jnp.where(lane >= DH_C, q, 0.0)], axis=0)


def _diff_prompt_body(sc_ref, q_ref, k_ref, v_ref, d_ref, nw_ref, o_ref, m_ref, acc_ref):
    hp = pl.program_id(1)
    qi = pl.program_id(2)
    blk = ATT_BLOCK
    lam = sc_ref[0, 0]
    out_scale = sc_ref[0, 1]
    qs = [_stack_maps(_head(q_ref[...], hh) * DH_C ** -0.5).astype(BF16) for hh in range(ATT_HEADS)]
    m_ref[...] = jnp.full_like(m_ref, -jnp.inf)
    acc_ref[...] = jnp.zeros_like(acc_ref)

    def block(kj, bias_of):
        sl = pl.ds(pl.multiple_of(kj * blk, blk), blk)
        for hh in range(ATT_HEADS):
            s = lax.dot_general(qs[hh], k_ref[sl, hh * HEAD_W:(hh + 1) * HEAD_W], (((1,), (1,)), ((), ())),
                                preferred_element_type=F32)
            bias = bias_of(hh)
            if bias.ndim == 2:
                s = (s.reshape(2, blk, blk) + bias[None]).reshape(2 * blk, blk)
            else:
                s = s + bias
            m_old = m_ref[hh]
            m_new = jnp.maximum(m_old, jnp.max(s, axis=-1, keepdims=True))
            alpha = jnp.exp(m_old - m_new)
            p = jnp.exp(s - jnp.concatenate([m_new] * (blk // LANES), axis=1))
            pv = jnp.dot(p.astype(BF16), v_ref[sl, hh * 2 * HEAD_W:(hh + 1) * 2 * HEAD_W],
                         preferred_element_type=F32)
            acc_ref[hh] = jnp.concatenate([alpha, alpha], axis=1) * acc_ref[hh] + pv
            m_ref[hh] = m_new

    def far(kj, carry):
        block(kj, lambda hh: sc_ref[1, hp * ATT_HEADS + hh])
        return carry

    lax.fori_loop(0, jnp.maximum(qi - 1, 0), far, 0)

    @pl.when(qi >= 1)
    def _():
        block(qi - 1, lambda hh: d_ref[hh, 1])

    block(qi, lambda hh: d_ref[hh, 0])
    outs = []
    for hh in range(ATT_HEADS):
        acc = acc_ref[hh]
        o = acc[:, :HEAD_W] / acc[:, HEAD_W:]
        o = o[:blk] - lam * o[blk:]
        outs.append(_rms_norm(o, nw_ref[...]) * out_scale)
    o_ref[...] = jnp.concatenate(outs, axis=1)


def _diff_prompt(proj, k16, v16, scalars, tiles, norm_w, batch, seq):
    nq = seq // ATT_BLOCK
    width = ATT_HEADS * HEAD_W
    per_row = BR_W // width
    return pl.pallas_call(
        _diff_prompt_body,
        grid=(batch, N_HEADS // ATT_HEADS, nq),
        in_specs=[pl.BlockSpec(memory_space=pltpu.SMEM),
                  pl.BlockSpec((ATT_BLOCK, width), lambda b, h, q: (b * nq + q, BLK_CQ * per_row + h)),
                  pl.BlockSpec((seq, width), lambda b, h, q: (b, h)),
                  pl.BlockSpec((seq, 2 * width), lambda b, h, q: (b, h)),
                  pl.BlockSpec((ATT_HEADS, 2, ATT_BLOCK, ATT_BLOCK), lambda b, h, q: (h, 0, 0, 0),
                               pipeline_mode=pl.Buffered(1) if ATT_HEADS == N_HEADS else None),
                  pl.BlockSpec((1, HEAD_W), lambda b, h, q: (0, 0))],
        out_specs=pl.BlockSpec((ATT_BLOCK, width), lambda b, h, q: (b * nq + q, h)),
        out_shape=jax.ShapeDtypeStruct((batch * seq, BR_W), F32),
        scratch_shapes=[pltpu.VMEM((ATT_HEADS, 2 * ATT_BLOCK, LANES), F32),
                        pltpu.VMEM((ATT_HEADS, 2 * ATT_BLOCK, 2 * HEAD_W), F32)],
        compiler_params=_cparams(3),
        name="diff_prompt",
    )(scalars, proj, k16, v16, tiles, norm_w)


def _diff_sample_body(pt_ref, sc_ref, q_ref, kn_ref, vn_ref, ck_ref, cv_ref, bias_ref, nbias_ref, nw_ref,
                      o_ref, kbuf, vbuf, sem, *, layer, n_pages, n_real, n_batch):
    b = pl.program_id(0)
    slot = b % 2
    pad = SAMPLE_PAD - n_real

    def page_copies(seq_idx, sl):
        cps = []
        for j in range(n_pages):
            page = pt_ref[seq_idx, j]
            rows = pl.ds(j * PAGE_SIZE * N_HEADS, PAGE_SIZE * N_HEADS)
            cps.append(pltpu.make_async_copy(ck_ref.at[layer, page], kbuf.at[sl, rows], sem.at[sl, 0]))
            cps.append(pltpu.make_async_copy(cv_ref.at[layer, page], vbuf.at[sl, rows], sem.at[sl, 1]))
        return cps

    @pl.when(b == 0)
    def _():
        for cp in page_copies(0, 0):
            cp.start()

    @pl.when(b + 1 < n_batch)
    def _():
        for cp in page_copies(b + 1, 1 - slot):
            cp.start()

    for cp in page_copies(b, slot):
        cp.wait()

    lam = sc_ref[0, 0]
    out_scale = sc_ref[0, 1]
    hs = range(N_HEADS)
    head_rows = [pl.ds(h, n_pages * PAGE_SIZE, stride=N_HEADS) for h in hs]
    qs = [_stack_maps(q_ref[pad:, h * HEAD_W:(h + 1) * HEAD_W] * DH_C ** -0.5) for h in hs]
    s = [_dot_nt(qs[h], kbuf[slot, head_rows[h], :]) + bias_ref[h] for h in hs]
    kn = [kn_ref[pad:, h * HEAD_W:(h + 1) * HEAD_W] for h in hs]
    vn = [vn_ref[pad:, h * HEAD_W:(h + 1) * HEAD_W] for h in hs]
    s_new = [[jnp.sum(qs[h] * kn[h][t:t + 1, :], axis=-1, keepdims=True) + nbias_ref[h, t] for t in range(n_real)]
             for h in hs]
    m = [functools.reduce(jnp.maximum, s_new[h], jnp.max(s[h], axis=-1, keepdims=True)) for h in hs]
    p = [jnp.exp(s[h] - m[h]) for h in hs]
    p_new = [[jnp.exp(sn - m[h]) for sn in s_new[h]] for h in hs]
    l = [sum(p_new[h], jnp.sum(p[h], axis=-1, keepdims=True)) for h in hs]
    acc = [_dot(p[h], vbuf[slot, head_rows[h], :]) for h in hs]
    outs = []
    for h in hs:
        a = acc[h]
        for t in range(n_real):
            a = a + p_new[h][t] * vn[h][t:t + 1, :]
        o = a / l[h]
        o = o[:n_real] - lam * o[n_real:]
        outs.append(_rms_norm(o, nw_ref[...]) * out_scale)
    o_ref[0:pad, :] = jnp.zeros((pad, BR_W), F32)
    o_ref[pad:, :] = jnp.concatenate(outs, axis=1)


def _diff_sample(page_table, scalars, proj, k_new, v_new, cache_k, cache_v, layer, bias, nbias, norm_w, n_real):
    n_batch, n_pages = page_table.shape
    n_past = n_pages * PAGE_SIZE
    new_tok = pl.BlockSpec((None, SAMPLE_PAD, BR_W), lambda b, pt: (layer, b, 0))
    grid_spec = pltpu.PrefetchScalarGridSpec(
        num_scalar_prefetch=1,
        grid=(n_batch,),
        in_specs=[pl.BlockSpec(memory_space=pltpu.SMEM),
                  pl.BlockSpec((SAMPLE_PAD, BR_W), lambda b, pt: (b, BLK_CQ)),
                  new_tok, new_tok,
                  pl.BlockSpec(memory_space=pl.ANY), pl.BlockSpec(memory_space=pl.ANY),
                  pl.BlockSpec(bias.shape, lambda b, pt: (0, 0, 0)),
                  pl.BlockSpec(nbias.shape, lambda b, pt: (0, 0, 0, 0)),
                  pl.BlockSpec((1, HEAD_W), lambda b, pt: (0, 0))],
        out_specs=pl.BlockSpec((SAMPLE_PAD, BR_W), lambda b, pt: (b, 0)),
        scratch_shapes=[pltpu.VMEM((2, n_past * N_HEADS, HEAD_W), F32), pltpu.VMEM((2, n_past * N_HEADS, HEAD_W), F32),
                        pltpu.SemaphoreType.DMA((2, 2))],
    )
    return pl.pallas_call(
        functools.partial(_diff_sample_body, layer=layer, n_pages=n_pages, n_real=n_real, n_batch=n_batch),
        grid_spec=grid_spec,
        out_shape=jax.ShapeDtypeStruct((n_batch * SAMPLE_PAD, BR_W), F32),
        compiler_params=_cparams(1),
        name="diff_sample",
    )(page_table, scalars, proj, k_new, v_new, cache_k, cache_v, bias, nbias, norm_w)


def _merge_body(oa_ref, ob_ref, oc_ref, g0_ref, g1_ref, g2_ref, x_ref, wb_ref, wo_ref, ln_ref, o_ref):
    m = (_sigmoid(g0_ref[...]) * _dot(oa_ref[...], wb_ref[0])
         + _sigmoid(g1_ref[...]) * _dot(ob_ref[...], wb_ref[1])
         + _sigmoid(g2_ref[...]) * _dot(oc_ref[...], wb_ref[2]))
    y = _dot(m, wo_ref[...])
    o_ref[...] = _layer_norm(ALPHA * x_ref[...] + y, ln_ref[0:1, :], ln_ref[1:2, :])


def _merge(oa, ob, oc, proj, x, w_branch, w_out, ln):
    t = x.shape[0]
    tm = min(ROW_TILE, t)
    br = pl.BlockSpec((tm, BR_W), lambda i: (i, 0))

    def gate(n):
        return pl.BlockSpec((tm, D_MODEL), lambda i: (i, n))

    return pl.pallas_call(
        _merge_body,
        grid=(t // tm,),
        in_specs=[br, br, br, gate(0), gate(1), gate(2),
                  pl.BlockSpec((tm, D_MODEL), lambda i: (i, 0)),
                  pl.BlockSpec(w_branch.shape, lambda i: (0, 0, 0)),
                  pl.BlockSpec(w_out.shape, lambda i: (0, 0)),
                  pl.BlockSpec((SUBLANES, D_MODEL), lambda i: (0, 0))],
        out_specs=pl.BlockSpec((tm, D_MODEL), lambda i: (i, 0)),
        out_shape=jax.ShapeDtypeStruct((t, D_MODEL), F32),
        compiler_params=_cparams(1),
        name="merge",
    )(oa, ob, oc, proj, proj, proj, x, w_branch, w_out, ln)


def _attend_memory(qs, kvs):
    n = range(len(qs))
    s = [_dot_nt(qs[i], kvs[i][0]) * HEAD_W ** -0.5 for i in n]
    p = [jnp.exp(s[i] - jnp.max(s[i], axis=-1, keepdims=True)) for i in n]
    p = [p[i] / jnp.sum(p[i], axis=-1, keepdims=True) for i in n]
    return [_dot(p[i], kvs[i][1]) for i in n]


def _merge_xattn_prompt_body(oa_ref, ob_ref, oc_ref, g0_ref, g1_ref, g2_ref, x_ref, wb_ref, wo_ref, ln0_ref,
                             mk_ref, mv_ref, wq_ref, wxo_ref, ln1_ref, o_ref):
    m = (_sigmoid(g0_ref[...]) * _dot(oa_ref[...], wb_ref[0])
         + _sigmoid(g1_ref[...]) * _dot(ob_ref[...], wb_ref[1])
         + _sigmoid(g2_ref[...]) * _dot(oc_ref[...], wb_ref[2]))
    x = _layer_norm(ALPHA * x_ref[...] + _dot(m, wo_ref[...]), ln0_ref[0:1, :], ln0_ref[1:2, :])
    mk, mv = mk_ref[0], mv_ref[0]
    q = _dot(x, wq_ref[...])
    o = jnp.concatenate(_attend_memory([_head(q, h) for h in range(N_HEADS)],
                                       [(_head(mk, h), _head(mv, h)) for h in range(N_HEADS)]), axis=1)
    o_ref[...] = _layer_norm(ALPHA * x + _dot(o, wxo_ref[...]), ln1_ref[0:1, :], ln1_ref[1:2, :])


def _merge_xattn_prompt(oa, ob, oc, proj, x, w_branch, w_out, ln0, mkv, w_xq, w_xo, ln1, batch, seq):
    tm = min(ROW_TILE, seq)
    per = seq // tm
    n_mem = mkv.shape[1]

    def rows(width, col):
        return pl.BlockSpec((tm, width), lambda b, i: (b * per + i, col))

    def whole(a):
        return pl.BlockSpec(a.shape, lambda b, i: (0,) * a.ndim)

    ln = pl.BlockSpec((SUBLANES, D_MODEL), lambda b, i: (0, 0))
    return pl.pallas_call(
        _merge_xattn_prompt_body,
        grid=(batch, per),
        in_specs=[rows(BR_W, 0), rows(BR_W, 0), rows(BR_W, 0),
                  rows(D_MODEL, 0), rows(D_MODEL, 1), rows(D_MODEL, 2), rows(D_MODEL, 0),
                  whole(w_branch), whole(w_out), ln,
                  pl.BlockSpec((1, n_mem, BR_W), lambda b, i: (b, 0, 0)),
                  pl.BlockSpec((1, n_mem, BR_W), lambda b, i: (b, 0, 1)),
                  whole(w_xq), whole(w_xo), ln],
        out_specs=rows(D_MODEL, 0),
        out_shape=jax.ShapeDtypeStruct(x.shape, F32),
        compiler_params=_cparams(2),
        name="merge_xattn_prompt",
    )(oa, ob, oc, proj, proj, proj, x, w_branch, w_out, ln0, mkv, mkv, w_xq, w_xo, ln1)


def _xattn_sample_body(x_ref, mk_ref, mv_ref, wq_ref, wo_ref, ln_ref, o_ref, att_ref, *, n_seq):
    x = x_ref[...]
    q = _dot(x, wq_ref[...])
    n_mem = mk_ref.shape[1] // N_HEADS
    tiles = [(i, h) for i in range(n_seq) for h in range(N_HEADS)]
    qs, kvs = [], []
    for i, h in tiles:
        rows = pl.ds(h, n_mem, stride=N_HEADS)
        qs.append(_head(q[i * SAMPLE_PAD:(i + 1) * SAMPLE_PAD], h))
        kvs.append((mk_ref[i, rows, :], mv_ref[i, rows, :]))
    for (i, h), o in zip(tiles, _attend_memory(qs, kvs)):
        att_ref[i * SAMPLE_PAD:(i + 1) * SAMPLE_PAD, h * HEAD_W:(h + 1) * HEAD_W] = o
    o_ref[...] = _layer_norm(ALPHA * x + _dot(att_ref[...], wo_ref[...]), ln_ref[0:1, :], ln_ref[1:2, :])


def _xattn_sample(x, mem_k, mem_v, layer, w_xq, w_xo, ln, seq_per_step=8):
    n_batch, mem_rows = mem_k.shape[1:3]
    rows = seq_per_step * SAMPLE_PAD
    mem = pl.BlockSpec((None, seq_per_step, mem_rows, HEAD_W), lambda g: (layer, g, 0, 0))
    return pl.pallas_call(
        functools.partial(_xattn_sample_body, n_seq=seq_per_step),
        grid=(n_batch // seq_per_step,),
        in_specs=[pl.BlockSpec((rows, D_MODEL), lambda g: (g, 0)), mem, mem,
                  pl.BlockSpec(w_xq.shape, lambda g: (0, 0)),
                  pl.BlockSpec(w_xo.shape, lambda g: (0, 0)),
                  pl.BlockSpec((SUBLANES, D_MODEL), lambda g: (0, 0))],
        out_specs=pl.BlockSpec((rows, D_MODEL), lambda g: (g, 0)),
        out_shape=jax.ShapeDtypeStruct(x.shape, F32),
        scratch_shapes=[pltpu.VMEM((rows, BR_W), F32)],
        compiler_params=_cparams(1),
        name="xattn_sample",
    )(x, mem_k, mem_v, w_xq, w_xo, ln)


def _ffn_body(x_ref, wg_ref, wu_ref, w2_ref, ln_ref, o_ref):
    x = x_ref[...]
    xb = x.astype(BF16)
    gate = jnp.dot(xb, wg_ref[...], preferred_element_type=F32)
    up = jnp.dot(xb, wu_ref[...], preferred_element_type=F32)
    y = _dot(_silu(gate) * up, w2_ref[...])
    o_ref[...] = _layer_norm(ALPHA * x + y, ln_ref[0:1, :], ln_ref[1:2, :])


def _ffn(x, w1, w2, ln):
    t = x.shape[0]
    tm = min(ROW_TILE, t)
    resident = pl.Buffered(1)
    return pl.pallas_call(
        _ffn_body,
        grid=(t // tm,),
        in_specs=[pl.BlockSpec((tm, D_MODEL), lambda i: (i, 0)),
                  pl.BlockSpec((D_MODEL, D_FF), lambda i: (0, 0), pipeline_mode=resident),
                  pl.BlockSpec((D_MODEL, D_FF), lambda i: (0, 1), pipeline_mode=resident),
                  pl.BlockSpec((D_FF, D_MODEL), lambda i: (0, 0), pipeline_mode=resident),
                  pl.BlockSpec((SUBLANES, D_MODEL), lambda i: (0, 0))],
        out_specs=pl.BlockSpec((tm, D_MODEL), lambda i: (i, 0)),
        out_shape=jax.ShapeDtypeStruct(x.shape, F32),
        compiler_params=_cparams(1),
        name="ffn",
    )(x, w1, w1, w2, ln)


def _split_w_in(w):
    ab0 = 8 * BR_W
    c0 = ab0 + 2 * N_HEADS
    g0 = c0 + 3 * BR_W
    pad = jnp.zeros((w.shape[0], LANES - 2 * N_HEADS), w.dtype)
    main = jnp.concatenate([w[:, g0:], w[:, :ab0], w[:, c0:c0 + BR_W]], axis=1)
    kv = jnp.concatenate([w[:, c0 + BR_W:g0], w[:, ab0:c0], pad], axis=1)
    return main.astype(BF16), kv.astype(BF16)


def _ln_rows(g, b):
    return jnp.concatenate([g[None], b[None], jnp.zeros((SUBLANES - 2, g.shape[0]), F32)], axis=0)


def kernel(x_prompt, x_sample, mem_prompt, cache_attn_k, cache_attn_v, cache_mem_k, cache_mem_v, state_hgrn,
           state_gdn, state_gdn_conv, page_table, w_in, w_branch, w_out, lower_bounds, hgrn_norm, gdn_a_log,
           gdn_dt_bias, gdn_conv_w, gdn_norm, diff_lambda, diff_norm, rel_bias, w_xq, w_mem_kv, w_xo, ln_g, ln_b,
           w_ffn_in, w_ffn_out):
    bp, seq, _ = x_prompt.shape
    bs, ls, _ = x_sample.shape
    n_mem = mem_prompt.shape[1]
    n_pages = page_table.shape[1]
    n_past = n_pages * PAGE_SIZE
    n_pool = cache_attn_k.shape[1]
    pad = SAMPLE_PAD - ls

    lb_cum = jnp.cumsum(jax.nn.softmax(lower_bounds.astype(F32), axis=0), axis=0)
    lb = (lb_cum - lb_cum[0]).reshape(DEPTH, N_HEADS, 1, HEAD_W)
    lbp = jnp.concatenate([jnp.log(lb), jnp.log1p(-lb), 1.0 - lb,
                           jnp.zeros((DEPTH, N_HEADS, SUBLANES - 3, HEAD_W), F32)], axis=2)

    n_max = max(2 * ATT_BLOCK + 2, n_past + ls)
    bucket = t5_bucket(-jnp.arange(n_max, dtype=jnp.int32))
    thr = jnp.searchsorted(bucket, jnp.arange(N_BUCKETS, dtype=jnp.int32), side="left").astype(jnp.int32)
    rel_bias_t = rel_bias.astype(F32).T
    tiles = _bias_tiles(thr, rel_bias_t)
    dist = rel_bias_t[:, bucket]
    far_bias = dist[:, ATT_BLOCK + 1]
    q_pos = n_past + jnp.arange(ls)
    past = dist[:, q_pos[:, None] - jnp.arange(n_past)[None, :]]
    bias_s = jnp.concatenate([past, past], axis=1)
    tt = jnp.arange(ls)[:, None] - jnp.arange(ls)[None, :]
    new = jnp.where(tt >= 0, dist[:, jnp.maximum(tt, 0)], NEG_INF)
    nbias = jnp.concatenate([new, new], axis=1).transpose(0, 2, 1)[..., None]

    xp = x_prompt.reshape(bp * seq, D_MODEL)
    xs = jnp.concatenate([jnp.zeros((bs, pad, D_MODEL), F32), x_sample], axis=1).reshape(bs * SAMPLE_PAD, D_MODEL)
    mem = mem_prompt.reshape(bp * n_mem, D_MODEL)
    cache_k = cache_attn_k.reshape(DEPTH, n_pool, PAGE_SIZE * N_HEADS, HEAD_W)
    cache_v = cache_attn_v.reshape(DEPTH, n_pool, PAGE_SIZE * N_HEADS, HEAD_W)
    mem_k = cache_mem_k.reshape(DEPTH, bs, n_mem * N_HEADS, HEAD_W)
    mem_v = cache_mem_v.reshape(DEPTH, bs, n_mem * N_HEADS, HEAD_W)

    pk = pv = jnp.zeros((DEPTH, bp * seq * N_HEADS, HEAD_W), F32)
    sk = sv = jnp.zeros((DEPTH, bs * SAMPLE_PAD, BR_W), F32)
    sh = sg = jnp.zeros(state_hgrn.shape, F32)
    pmk, pmv, ph, pg, pc, sc = [], [], [], [], [], []
    b_col0 = BLK_B * BR_W

    for l in range(DEPTH):
        w_main, w_kv = _split_w_in(w_in[l])
        wb_l = w_branch[l].astype(BF16)
        wo_l = w_out[l].astype(BF16)
        wq_l = w_xq[l].astype(BF16)
        wxo_l = w_xo[l].astype(BF16)
        wkv_l = w_mem_kv[l].astype(BF16)
        w1_l = w_ffn_in[l].astype(BF16)
        w2_l = w_ffn_out[l].astype(BF16)
        lns = [_ln_rows(ln_g[l, i], ln_b[l, i]) for i in range(3)]
        hn = hgrn_norm[l][None]
        gn = gdn_norm[l][None]
        dn = diff_norm[l][None]
        head_params = jnp.stack([jnp.exp(gdn_a_log[l].astype(F32)), gdn_dt_bias[l].astype(F32)])
        lam_init = 0.8 - 0.6 * math.exp(-0.3 * l)
        lq1, lk1, lq2, lk2 = diff_lambda[l].astype(F32)
        lam = jnp.exp(jnp.sum(lq1 * lk1)) - jnp.exp(jnp.sum(lq2 * lk2)) + lam_init
        scalars = jnp.stack([jnp.stack([lam, jnp.float32(1.0 - lam_init), jnp.float32(0), jnp.float32(0)]),
                             far_bias])
        conv_w_l = gdn_conv_w[l]

        proj = _matmul(xp, w_main, 2 * ROW_TILE, MAIN_W // 3)
        pk, pv, ab, k16, v16 = _kv_proj(xp, w_kv, l, pk, pv, True)
        oa, h_new = _hgrn_prompt(proj, lbp[l], hn, bp, seq)
        qkv = _gdn_conv(proj, BLK_B, conv_w_l, SEQ_TILE, seq)
        ob, g_new = _gdn_prompt(qkv, proj, ab, head_params, gn, bp, seq)
        oc = _diff_prompt(proj, k16, v16, scalars, tiles, dn, bp, seq)
        mkv = _matmul(mem, wkv_l, ROW_TILE, 2 * BR_W).reshape(bp, n_mem, 2 * BR_W)
        x2 = _merge_xattn_prompt(oa, ob, oc, proj, xp, wb_l, wo_l, lns[0], mkv, wq_l, wxo_l, lns[1], bp, seq)
        xp = _ffn(x2, w1_l, w2_l, lns[2])
        pmk.append(mkv[:, :, :BR_W].reshape(bp, n_mem, N_HEADS, HEAD_W))
        pmv.append(mkv[:, :, BR_W:].reshape(bp, n_mem, N_HEADS, HEAD_W))
        ph.append(h_new)
        pg.append(g_new)
        pc.append(proj.reshape(bp, seq, MAIN_W)[:, seq - (CONV_W - 1):, b_col0:b_col0 + CONV_CH])

        proj_s = _matmul(xs, w_main, ROW_TILE, MAIN_W // 3)
        sk, sv, ab_s, _, _ = _kv_proj(xs, w_kv, l, sk, sv, False)
        oa, sh = _hgrn_sample(proj_s, lbp[l], hn, state_hgrn, l, sh, ls)
        conv_tok = proj_s[:, b_col0:b_col0 + CONV_CH].reshape(bs, SAMPLE_PAD, CONV_CH)
        conv_in = jnp.concatenate([conv_tok[:, :pad - (CONV_W - 1)], state_gdn_conv[l], conv_tok[:, pad:]], axis=1)
        qkv = _gdn_conv(conv_in.reshape(bs * SAMPLE_PAD, CONV_CH), 0, conv_w_l, bs * SAMPLE_PAD, SAMPLE_PAD)
        ob, sg = _gdn_sample(qkv, proj_s, ab_s, head_params, gn, state_gdn, l, sg, ls)
        oc = _diff_sample(page_table, scalars, proj_s, sk, sv, cache_k, cache_v, l, bias_s, nbias, dn, ls)
        x1 = _merge(oa, ob, oc, proj_s, xs, wb_l, wo_l, lns[0])
        x2 = _xattn_sample(x1, mem_k, mem_v, l, wq_l, wxo_l, lns[1])
        xs = _ffn(x2, w1_l, w2_l, lns[2])
        sc.append(conv_in[:, SAMPLE_PAD - (CONV_W - 1):])

    y_prompt = xp.reshape(bp, seq, D_MODEL)
    y_sample = xs.reshape(bs, SAMPLE_PAD, D_MODEL)[:, pad:]
    new_k_p = pk.reshape(DEPTH, bp, seq, N_HEADS, HEAD_W)
    new_v_p = pv.reshape(DEPTH, bp, seq, N_HEADS, HEAD_W)
    new_k_s = sk.reshape(DEPTH, bs, SAMPLE_PAD, N_HEADS, HEAD_W)[:, :, pad:]
    new_v_s = sv.reshape(DEPTH, bs, SAMPLE_PAD, N_HEADS, HEAD_W)[:, :, pad:]
    return (y_prompt, y_sample, new_k_p, new_v_p, jnp.stack(pmk), jnp.stack(pmv), jnp.stack(ph), jnp.stack(pg),
            jnp.stack(pc), new_k_s, new_v_s, sh, sg, jnp.stack(sc))
```
